```python
import jax, jax.numpy as jnp
from jax import lax
import numpy as np

D_MODEL = 1024
BATCH = 8
SEQ = 4096
DEPTH = 1

N_META = 16
BLOCK = 128
LEAD_PAD = BLOCK - N_META
SB_HEAD_DIM = 64
SB_WIDTH = D_MODEL // 2
SB_HEADS = SB_WIDTH // SB_HEAD_DIM
HG_HEAD_DIM = 128
HG_WIDTH = D_MODEL - SB_WIDTH
HG_HEADS = HG_WIDTH // HG_HEAD_DIM
MIX_WIDTH = SB_WIDTH + HG_WIDTH
IN_SPLITS = [SB_WIDTH, SB_WIDTH, SB_WIDTH, HG_WIDTH, HG_WIDTH, HG_WIDTH, HG_WIDTH]
IN_PROJ_WIDTH = sum(IN_SPLITS)
N_GROUPS = 4
EXPERTS_PER_GROUP = 8
N_EXPERTS = N_GROUPS * EXPERTS_PER_GROUP
TOP_K_IN_GROUP = 2
EXPERT_FF = D_MODEL // 2
MOE_BLOCK = 128
EPS = 1e-6

kernel_name = 'hymba_stickbreak_hgrn2_hier_moe'


def rmsnorm(x, gain):
    xf = x.astype(jnp.float32)
    y = xf * lax.rsqrt(jnp.mean(xf * xf, axis=-1, keepdims=True) + EPS)
    return (y * gain.astype(jnp.float32)).astype(x.dtype)


def split_heads(t, n_heads, head_dim):
    b, l, _ = t.shape
    return t.reshape(b, l, n_heads, head_dim).transpose(0, 2, 1, 3)


def merge_heads_norm(o, gain):
    b, h, l, d = o.shape
    o = rmsnorm(o.transpose(0, 2, 1, 3), gain.reshape(h, d))
    return o.reshape(b, l, h * d)


def stick_breaking_attention(q, k, v):
    L = q.shape[2]
    scale = SB_HEAD_DIM ** -0.5
    outs = []
    for blk in range(L // BLOCK):
        t0 = blk * BLOCK
        t1 = t0 + BLOCK
        z = jnp.einsum('bhtd,bhsd->bhts', q[:, :, t0:t1], k[:, :, :t1]).astype(jnp.float32) * scale
        t_pos = jnp.arange(t0, t1)[:, None]
        s_pos = jnp.arange(t1)[None, :]
        valid = (s_pos < t_pos) & (s_pos >= LEAD_PAD)
        log_beta = jax.nn.log_sigmoid(z)
        log_keep = jnp.where(valid, jax.nn.log_sigmoid(-z), 0.0)
        log_tail = lax.cumsum(log_keep, axis=3, reverse=True) - log_keep
        w = jnp.where(valid, jnp.exp(log_beta + log_tail), 0.0)
        outs.append(jnp.einsum('bhts,bhsd->bhtd', w.astype(v.dtype), v[:, :, :t1]))
    return jnp.concatenate(outs, axis=2)


def hgrn2_chunked(q, k, v, log_f):
    b, h, L, dk = q.shape
    dv = v.shape[-1]
    nc = L // BLOCK

    def to_chunks(t):
        return jnp.moveaxis(t.reshape(b, h, nc, BLOCK, t.shape[-1]), 2, 0)

    causal = jnp.tril(jnp.ones((BLOCK, BLOCK), bool))[:, :, None]

    def step(state, inp):
        qc, kc, vc, gc = inp
        bc = jnp.cumsum(gc, axis=2)
        diff = bc[:, :, :, None, :] - bc[:, :, None, :, :]
        decay = jnp.exp(jnp.where(causal, diff, -jnp.inf))
        scores = jnp.einsum('bhtd,bhsd,bhtsd->bhts', qc, kc, decay)
        o = jnp.einsum('bhts,bhsv->bhtv', scores, vc) + jnp.einsum('bhtd,bhdv->bhtv', qc * jnp.exp(bc), state)
        b_last = bc[:, :, -1, :]
        state = jnp.exp(b_last)[..., None] * state + jnp.einsum(
            'bhsd,bhsv->bhdv', kc * jnp.exp(b_last[:, :, None, :] - bc), vc)
        return state, o

    s0 = jnp.zeros((b, h, dk, dv), jnp.float32)
    _, o = lax.scan(step, s0, (to_chunks(q), to_chunks(k), to_chunks(v), to_chunks(log_f)))
    return jnp.moveaxis(o, 0, 2).reshape(b, h, L, dv)


def hierarchical_moe(x2, w_rg, b_rg, w_re, b_re, w_gate, w_up, w_down):
    n_tok = x2.shape[0]
    grp_probs = jax.nn.softmax(jnp.dot(x2, w_rg).astype(jnp.float32) + b_rg.astype(jnp.float32), axis=-1)
    grp = jnp.argmax(grp_probs, axis=-1)
    p_grp = jnp.max(grp_probs, axis=-1)
    exp_logits = (jnp.dot(x2, w_re).astype(jnp.float32) + b_re.astype(jnp.float32)).reshape(
        n_tok, N_GROUPS, EXPERTS_PER_GROUP)
    in_grp = exp_logits[jnp.arange(n_tok), grp]
    top_val, top_idx = lax.top_k(in_grp, TOP_K_IN_GROUP)
    gate = jax.nn.softmax(top_val, axis=-1) * p_grp[:, None]
    expert_id = (grp[:, None] * EXPERTS_PER_GROUP + top_idx).reshape(-1).astype(jnp.int32)
    gate_flat = gate.reshape(-1)
    token_id = jnp.repeat(jnp.arange(n_tok, dtype=jnp.int32), TOP_K_IN_GROUP)
    n_assign = n_tok * TOP_K_IN_GROUP
    n_slots = -(-n_assign // MOE_BLOCK) * MOE_BLOCK + N_EXPERTS * MOE_BLOCK
    n_blocks = n_slots // MOE_BLOCK
    counts = jnp.zeros((N_EXPERTS,), jnp.int32).at[expert_id].add(1)
    padded = (counts + MOE_BLOCK - 1) // MOE_BLOCK * MOE_BLOCK
    padded_end = jnp.cumsum(padded)
    padded_start = padded_end - padded
    start = jnp.cumsum(counts) - counts
    order = jnp.argsort(expert_id)
    sorted_e = expert_id[order]
    dest = padded_start[sorted_e] + jnp.arange(n_assign, dtype=jnp.int32) - start[sorted_e]
    slot_tok = jnp.zeros((n_slots,), jnp.int32).at[dest].set(token_id[order])
    slot_gate = jnp.zeros((n_slots,), jnp.float32).at[dest].set(gate_flat[order])
    block_e = jnp.clip(jnp.searchsorted(padded_end, jnp.arange(n_blocks, dtype=jnp.int32) * MOE_BLOCK,
                                        side='right'), 0, N_EXPERTS - 1)

    def run_block(args):
        tok, g, e = args
        xb = x2[tok]
        hb = jax.nn.silu(xb @ w_gate[e]) * (xb @ w_up[e])
        return (hb @ w_down[e]) * g[:, None].astype(x2.dtype)

    y_blocks = lax.map(run_block, (slot_tok.reshape(n_blocks, MOE_BLOCK),
                                   slot_gate.reshape(n_blocks, MOE_BLOCK), block_e))
    return jnp.zeros_like(x2).at[slot_tok].add(y_blocks.reshape(n_slots, -1))


def setup_inputs(seed: int = 0) -> dict:
    key = jax.random.key(seed)
    ks = jax.random.split(key, 18)
    f32 = jnp.float32

    def nrm(k, shape, scale):
        return jax.random.normal(k, shape, f32) * scale

    return {
        'x': nrm(ks[0], (BATCH, SEQ, D_MODEL), 1.0),
        'meta_tokens': nrm(ks[1], (N_META, D_MODEL), 1.0),
        'lb_logits': nrm(ks[2], (DEPTH + 1, HG_WIDTH), 0.1).at[-1].add(2.0),
        'g_mix': 1.0 + nrm(ks[3], (DEPTH, D_MODEL), 0.02),
        'w_in': nrm(ks[4], (DEPTH, D_MODEL, IN_PROJ_WIDTH), D_MODEL ** -0.5),
        'sb_gain': 1.0 + nrm(ks[5], (DEPTH, SB_WIDTH), 0.02),
        'hg_gain': 1.0 + nrm(ks[6], (DEPTH, HG_WIDTH), 0.02),
        'w_out': nrm(ks[7], (DEPTH, MIX_WIDTH, D_MODEL), MIX_WIDTH ** -0.5),
        'g_ffn': 1.0 + nrm(ks[8], (DEPTH, D_MODEL), 0.02),
        'w_router_group': nrm(ks[9], (DEPTH, D_MODEL, N_GROUPS), D_MODEL ** -0.5),
        'b_router_group': nrm(ks[10], (DEPTH, N_GROUPS), 0.01),
        'w_router_expert': nrm(ks[11], (DEPTH, D_MODEL, N_EXPERTS), D_MODEL ** -0.5),
        'b_router_expert': nrm(ks[12], (DEPTH, N_EXPERTS), 0.01),
        'w_expert_gate': nrm(ks[13], (DEPTH, N_EXPERTS, D_MODEL, EXPERT_FF), D_MODEL ** -0.5),
        'w_expert_up': nrm(ks[14], (DEPTH, N_EXPERTS, D_MODEL, EXPERT_FF), D_MODEL ** -0.5),
        'w_expert_down': nrm(ks[15], (DEPTH, N_EXPERTS, EXPERT_FF, D_MODEL), EXPERT_FF ** -0.5),
        'g_final': 1.0 + nrm(ks[16], (D_MODEL,), 0.02),
    }


def reference(x, meta_tokens, lb_logits, g_mix, w_in, sb_gain, hg_gain, w_out, g_ffn,
              w_router_group, b_router_group, w_router_expert, b_router_expert,
              w_expert_gate, w_expert_up, w_expert_down, g_final):
    b = x.shape[0]
    lead = jnp.zeros((b, LEAD_PAD, D_MODEL), x.dtype)
    meta = jnp.broadcast_to(meta_tokens.astype(x.dtype)[None], (b, N_META, D_MODEL))
    h = jnp.concatenate([lead, meta, x], axis=1)
    L = h.shape[1]
    real = (jnp.arange(L) >= LEAD_PAD)[None, :, None]
    lower_bounds = jnp.cumsum(jax.nn.softmax(lb_logits.astype(jnp.float32), axis=0), axis=0)
    split_idx = np.cumsum(IN_SPLITS)[:-1].tolist()

    for layer in range(DEPTH):
        a = rmsnorm(h, g_mix[layer])
        proj = jnp.einsum('bld,de->ble', a, w_in[layer])
        sb_q, sb_k, sb_v, hg_q, hg_f, hg_i, hg_g = jnp.split(proj, split_idx, axis=-1)

        o_sb = stick_breaking_attention(split_heads(sb_q, SB_HEADS, SB_HEAD_DIM),
                                        split_heads(sb_k, SB_HEADS, SB_HEAD_DIM),
                                        split_heads(sb_v, SB_HEADS, SB_HEAD_DIM))
        o_sb = merge_heads_norm(o_sb, sb_gain[layer]).astype(proj.dtype)

        lb = lower_bounds[layer]
        f_pre = hg_f.astype(jnp.float32)
        log_f = jnp.log(lb + (1.0 - lb) * jax.nn.sigmoid(f_pre))
        k_in = jnp.where(real, (1.0 - lb) * jax.nn.sigmoid(-f_pre), 0.0)
        q_in = jax.nn.silu(hg_q.astype(jnp.float32))
        o_hg = hgrn2_chunked(split_heads(q_in, HG_HEADS, HG_HEAD_DIM),
                             split_heads(k_in, HG_HEADS, HG_HEAD_DIM),
                             split_heads(hg_i.astype(jnp.float32), HG_HEADS, HG_HEAD_DIM),
                             split_heads(log_f, HG_HEADS, HG_HEAD_DIM))
        o_hg = (merge_heads_norm(o_hg, hg_gain[layer]) * jax.nn.silu(hg_g.astype(jnp.float32))).astype(proj.dtype)

        mixed = jnp.concatenate([o_sb, o_hg], axis=-1)
        h = h + jnp.einsum('ble,ed->bld', mixed, w_out[layer])

        m = rmsnorm(h[:, LEAD_PAD:], g_ffn[layer])
        n_pos = m.shape[1]
        y = hierarchical_moe(m.reshape(b * n_pos, D_MODEL), w_router_group[layer], b_router_group[layer],
                             w_router_expert[layer], b_router_expert[layer], w_expert_gate[layer],
                             w_expert_up[layer], w_expert_down[layer])
        h = h + jnp.pad(y.reshape(b, n_pos, D_MODEL), ((0, 0), (LEAD_PAD, 0), (0, 0)))

    out = rmsnorm(h[:, BLOCK:], g_final)
    return out
```

```python
import functools

import numpy as np
import jax
import jax.numpy as jnp
from jax import lax
from jax.experimental import pallas as pl
from jax.experimental.pallas import tpu as pltpu

BLOCK = 128
N_META = 16
LEAD_PAD = BLOCK - N_META
SB_HEAD_DIM = 64
HG_HEAD_DIM = 128
SUB = 16
N_GROUPS = 4
EXPERTS_PER_GROUP = 8
N_EXPERTS = N_GROUPS * EXPERTS_PER_GROUP
TOP_K = 2
MOE_BM = 128
EPS = 1e-6
LANES = 128

F32 = jnp.float32
BF16 = jnp.bfloat16


def _dot(a, b):
    return jnp.dot(a, b, preferred_element_type=F32)


def _dot_nt(a, b):
    return lax.dot_general(a, b, (((1,), (1,)), ((), ())), preferred_element_type=F32)


def _dot_tn(a, b):
    return lax.dot_general(a, b, (((0,), (0,)), ((), ())), preferred_element_type=F32)


def _rms(x):
    return x * lax.rsqrt(jnp.mean(x * x, axis=-1, keepdims=True) + EPS)


def _proj_kernel(h_ref, mask_ref, g_ref, w_ref, lb_ref,
                 q_ref, k0_ref, k1_ref, v0_ref, v1_ref,
                 hq_ref, hk_ref, hv_ref, lf_ref, hg_ref, *, width):
    a = (_rms(h_ref[...]) * g_ref[...]).astype(BF16)

    def p(i):
        return _dot(a, w_ref[:, i * width:(i + 1) * width])

    lane = lax.broadcasted_iota(jnp.int32, (1, width), 1)
    head0 = (lane & (LANES - 1)) < SB_HEAD_DIM
    q_ref[...] = (p(0) * (SB_HEAD_DIM ** -0.5)).astype(BF16)
    k = p(1)
    k0_ref[...] = jnp.where(head0, k, 0.0).astype(BF16)
    k1_ref[...] = jnp.where(head0, 0.0, k).astype(BF16)
    v = p(2)
    v0_ref[...] = jnp.where(head0, v, 0.0).astype(BF16)
    v1_ref[...] = jnp.where(head0, 0.0, v).astype(BF16)
    hq = p(3)
    hq_ref[...] = hq * jax.nn.sigmoid(hq)
    f = p(4)
    lb = lb_ref[...]
    sig = jax.nn.sigmoid(f)
    lf_ref[...] = jnp.log(lb + (1.0 - lb) * sig)
    hk_ref[...] = mask_ref[...] * ((1.0 - lb) * jax.nn.sigmoid(-f))
    hv_ref[...] = p(5)
    g = p(6)
    hg_ref[...] = g * jax.nn.sigmoid(g)


def _proj(h, mask, g_mix, w_in, lb, tm):
    t, d = h.shape
    width = d // 2
    kern = functools.partial(_proj_kernel, width=width)
    row = lambda i: (i, 0)
    const = lambda i: (0, 0)
    outs = ([jax.ShapeDtypeStruct((t, width), BF16)] * 5
            + [jax.ShapeDtypeStruct((t, width), F32)] * 5)
    return pl.pallas_call(
        kern,
        grid=(t // tm,),
        in_specs=[pl.BlockSpec((tm, d), row), pl.BlockSpec((tm, 1), row),
                  pl.BlockSpec((1, d), const), pl.BlockSpec((d, 7 * width), const),
                  pl.BlockSpec((1, width), const)],
        out_specs=[pl.BlockSpec((tm, width), row)] * 10,
        out_shape=outs,
        compiler_params=pltpu.CompilerParams(
            dimension_semantics=("arbitrary",), vmem_limit_bytes=52 * 1024 * 1024),
        name="proj",
    )(h, mask, g_mix, w_in, lb)


def _sb_kernel(q_ref, k0_ref, k1_ref, v0_ref, v1_ref, mo_ref, gain_ref, o_ref, c_ref, acc_ref):
    qi = pl.program_id(2)
    q = q_ref[...]
    c_ref[...] = jnp.zeros_like(c_ref)
    acc_ref[...] = jnp.zeros_like(acc_ref)
    row = lax.broadcasted_iota(jnp.int32, (BLOCK, 2 * BLOCK), 0)
    col = lax.broadcasted_iota(jnp.int32, (BLOCK, 2 * BLOCK), 1) & (BLOCK - 1)
    tpos = qi * BLOCK + row

    def body(it, carry):
        j = qi - it
        off = pl.multiple_of(j * BLOCK, BLOCK)
        k2 = jnp.concatenate([k0_ref[pl.ds(off, BLOCK), :], k1_ref[pl.ds(off, BLOCK), :]], axis=0)
        v2 = jnp.concatenate([v0_ref[pl.ds(off, BLOCK), :], v1_ref[pl.ds(off, BLOCK), :]], axis=0)
        zz = _dot_nt(q, k2)
        spos = j * BLOCK + col
        valid = (spos < tpos) & (spos >= LEAD_PAD)
        sp = jnp.maximum(zz, 0.0) + jnp.log(1.0 + jnp.exp(-jnp.abs(zz)))
        spm = jnp.where(valid, sp, 0.0)
        hi = spm.astype(BF16)
        lo = (spm - hi.astype(F32)).astype(BF16)
        mo = mo_ref[...]
        res = _dot(hi, mo) + _dot(lo, mo)
        c = c_ref[...]
        logw = zz - sp + res[:, :2 * BLOCK] + c
        w = jnp.where(valid, jnp.exp(logw), 0.0)
        acc_ref[...] += _dot(w.astype(BF16), v2)
        c_ref[...] = c + res[:, 2 * BLOCK:]
        return carry

    lax.fori_loop(0, qi + 1, body, 0)

    o = acc_ref[...]
    o2 = o * o
    head0 = lax.broadcasted_iota(jnp.int32, (BLOCK, LANES), 1) < SB_HEAD_DIM
    s0 = jnp.sum(jnp.where(head0, o2, 0.0), axis=-1, keepdims=True) * (1.0 / SB_HEAD_DIM)
    s1 = jnp.sum(jnp.where(head0, 0.0, o2), axis=-1, keepdims=True) * (1.0 / SB_HEAD_DIM)
    r = jnp.where(head0, lax.rsqrt(s0 + EPS), lax.rsqrt(s1 + EPS))
    o_ref[...] = (o * r * gain_ref[...]).astype(o_ref.dtype)


def _sb_tri_matrix():
    j = np.arange(2 * BLOCK)[:, None]
    s = np.arange(2 * BLOCK)[None, :]
    same = (j // BLOCK) == (s // BLOCK)
    tri = np.where(same & ((j % BLOCK) > (s % BLOCK)), -1.0, 0.0)
    tot = np.where(same, -1.0, 0.0)
    return jnp.asarray(np.concatenate([tri, tot], axis=1), dtype=BF16)


def _sb_attention(q, k0, k1, v0, v1, gain, b, lp):
    t, width = q.shape
    nq = lp // BLOCK
    npair = width // LANES
    k3 = [a.reshape(b, lp, width) for a in (k0, k1, v0, v1)]
    kv_spec = pl.BlockSpec((None, lp, LANES), lambda bi, hp, qi: (bi, 0, hp))
    return pl.pallas_call(
        _sb_kernel,
        grid=(b, npair, nq),
        in_specs=[pl.BlockSpec((BLOCK, LANES), lambda bi, hp, qi: (bi * nq + qi, hp)),
                  kv_spec, kv_spec, kv_spec, kv_spec,
                  pl.BlockSpec((2 * BLOCK, 4 * BLOCK), lambda bi, hp, qi: (0, 0)),
                  pl.BlockSpec((1, LANES), lambda bi, hp, qi: (0, hp))],
        out_specs=pl.BlockSpec((BLOCK, LANES), lambda bi, hp, qi: (bi * nq + qi, hp)),
        out_shape=jax.ShapeDtypeStruct((t, width), BF16),
        scratch_shapes=[pltpu.VMEM((BLOCK, 2 * BLOCK), F32), pltpu.VMEM((BLOCK, LANES), F32)],
        compiler_params=pltpu.CompilerParams(
            dimension_semantics=("arbitrary", "arbitrary", "arbitrary"),
            vmem_limit_bytes=40 * 1024 * 1024),
        name="sb_attn",
    )(q, *k3, _sb_tri_matrix(), gain)


def _hg_kernel(hq_ref, hk_ref, hv_ref, lf_ref, gate_ref, gain_ref, lmat_ref, o_ref,
               st_ref, a_ref, qt_ref, kh_ref, oacc_ref, *, n_heads):
    @pl.when(pl.program_id(1) == 0)
    def _():
        st_ref[...] = jnp.zeros_like(st_ref)

    lf = lf_ref[...]
    h1 = lf.astype(BF16)
    r1 = lf - h1.astype(F32)
    h2 = r1.astype(BF16)
    h3 = (r1 - h2.astype(F32)).astype(BF16)
    lmat = lmat_ref[...]
    cs = _dot(lmat, h1) + _dot(lmat, h2) + _dot(lmat, h3)
    a = cs[:BLOCK]
    alast = cs[BLOCK:]
    a_ref[...] = a
    qt_ref[...] = hq_ref[...] * jnp.exp(a)
    kh_ref[...] = hk_ref[...] * jnp.exp(alast - a)
    ridx = lax.broadcasted_iota(jnp.int32, (SUB, 1), 0)

    def sub_body(i, carry):
        r0 = pl.multiple_of(i * SUB, SUB)
        rows = pl.ds(r0, SUB)
        for hd in range(n_heads):
            cols = slice(hd * HG_HEAD_DIM, (hd + 1) * HG_HEAD_DIM)
            a_i = a_ref[rows, cols]
            q_i = hq_ref[rows, cols]
            k_i = hk_ref[rows, cols]
            v_i = hv_ref[rows, cols]
            st = st_ref[hd]
            o_i = _dot_nt(qt_ref[rows, cols], st)
            for s in range(SUB):
                dec = jnp.exp(jnp.minimum(a_i - a_i[s:s + 1, :], 0.0))
                sc = jnp.sum(q_i * dec * k_i[s:s + 1, :], axis=-1, keepdims=True)
                sc = jnp.where(ridx >= s, sc, 0.0)
                o_i = o_i + sc * v_i[s:s + 1, :]
            oacc_ref[rows, cols] = o_i
            kh_i = kh_ref[rows, cols]
            decay = jnp.exp(a_i[SUB - 1:SUB, :])
            st_ref[hd] = st * decay + _dot_tn(v_i, kh_i)
        return carry

    lax.fori_loop(0, BLOCK // SUB, sub_body, 0)

    for hd in range(n_heads):
        cols = slice(hd * HG_HEAD_DIM, (hd + 1) * HG_HEAD_DIM)
        o = _rms(oacc_ref[:, cols]) * gain_ref[:, cols] * gate_ref[:, cols]
        o_ref[:, cols] = o.astype(o_ref.dtype)


def _hg_cumsum_matrix():
    t = np.arange(BLOCK)[:, None]
    s = np.arange(BLOCK)[None, :]
    same = (t // SUB) == (s // SUB)
    incl = np.where(same & (s <= t), 1.0, 0.0)
    full = np.where(same, 1.0, 0.0)
    return jnp.asarray(np.concatenate([incl, full], axis=0), dtype=BF16)


def _hgrn2(hq, hk, hv, lf, gate, gain, b, lp):
    t, width = hq.shape
    nc = lp // BLOCK
    n_heads = width // HG_HEAD_DIM
    blk = pl.BlockSpec((BLOCK, width), lambda bi, ci: (bi * nc + ci, 0))
    kern = functools.partial(_hg_kernel, n_heads=n_heads)
    return pl.pallas_call(
        kern,
        grid=(b, nc),
        in_specs=[blk, blk, blk, blk, blk,
                  pl.BlockSpec((1, width), lambda bi, ci: (0, 0)),
                  pl.BlockSpec((2 * BLOCK, BLOCK), lambda bi, ci: (0, 0))],
        out_specs=blk,
        out_shape=jax.ShapeDtypeStruct((t, width), BF16),
        scratch_shapes=[pltpu.VMEM((n_heads, HG_HEAD_DIM, HG_HEAD_DIM), F32),
                        pltpu.VMEM((BLOCK, width), F32), pltpu.VMEM((BLOCK, width), F32),
                        pltpu.VMEM((BLOCK, width), F32), pltpu.VMEM((BLOCK, width), F32)],
        compiler_params=pltpu.CompilerParams(dimension_semantics=("arbitrary", "arbitrary")),
        name="hgrn2",
    )(hq, hk, hv, lf, gate, gain, _hg_cumsum_matrix())


def _outproj_kernel(osb_ref, ohg_ref, h_ref, w_ref, g_ref, wr_ref, br_ref, h2_ref, m_ref, lg_ref, *, width):
    h2 = h_ref[...] + _dot(osb_ref[...], w_ref[:width, :]) + _dot(ohg_ref[...], w_ref[width:, :])
    h2_ref[...] = h2
    m = _rms(h2) * g_ref[...]
    m_ref[...] = m
    lg_ref[...] = _dot(m, wr_ref[...]) + br_ref[...]


def _outproj(o_sb, o_hg, h, w_out, g_ffn, w_r, b_r, tm):
    t, d = h.shape
    width = o_sb.shape[1]
    row = lambda i: (i, 0)
    const = lambda i: (0, 0)
    kern = functools.partial(_outproj_kernel, width=width)
    return pl.pallas_call(
        kern,
        grid=(t // tm,),
        in_specs=[pl.BlockSpec((tm, width), row), pl.BlockSpec((tm, width), row),
                  pl.BlockSpec((tm, d), row), pl.BlockSpec((2 * width, d), const),
                  pl.BlockSpec((1, d), const), pl.BlockSpec((d, LANES), const),
                  pl.BlockSpec((1, LANES), const)],
        out_specs=[pl.BlockSpec((tm, d), row), pl.BlockSpec((tm, d), row),
                   pl.BlockSpec((tm, LANES), row)],
        out_shape=[jax.ShapeDtypeStruct((t, d), F32), jax.ShapeDtypeStruct((t, d), F32),
                   jax.ShapeDtypeStruct((t, LANES), F32)],
        compiler_params=pltpu.CompilerParams(
            dimension_semantics=("arbitrary",), vmem_limit_bytes=40 * 1024 * 1024),
        name="outproj",
    )(o_sb, o_hg, h, w_out, g_ffn, w_r, b_r)


def _row_gather(src_hbm, dst_ref, idx_ref, base, n, sem):
    def issue(r, carry):
        pltpu.make_async_copy(src_hbm.at[pl.ds(idx_ref[base + r], 1), :],
                              dst_ref.at[pl.ds(r, 1), :], sem).start()
        return carry
    lax.fori_loop(0, n, issue, 0)

    def drain(r, carry):
        pltpu.make_async_copy(src_hbm.at[pl.ds(0, 1), :], dst_ref.at[pl.ds(r, 1), :], sem).wait()
        return carry
    lax.fori_loop(0, n, drain, 0)


def _expert_kernel(be_ref, nu_ref, tok_ref, m_hbm, gate_ref, wg_ref, wu_ref, wd_ref, y_ref, xs_ref, sem):
    i = pl.program_id(0)

    @pl.when(i < nu_ref[0])
    def _():
        _row_gather(m_hbm, xs_ref, tok_ref, i * MOE_BM, MOE_BM, sem)
        xs = xs_ref[...]
        gt = _dot(xs, wg_ref[...])
        hb = gt * jax.nn.sigmoid(gt) * _dot(xs, wu_ref[...])
        y_ref[...] = _dot(hb, wd_ref[...]) * gate_ref[...]

    @pl.when(i >= nu_ref[0])
    def _():
        y_ref[...] = jnp.zeros_like(y_ref)


def _experts(block_e, n_used, slot_tok, m, slot_gate, w_gate, w_up, w_down):
    t, d = m.shape
    n_slots = slot_tok.shape[0]
    ff = w_gate.shape[-1]
    grid_spec = pltpu.PrefetchScalarGridSpec(
        num_scalar_prefetch=3,
        grid=(n_slots // MOE_BM,),
        in_specs=[pl.BlockSpec(memory_space=pl.ANY),
                  pl.BlockSpec((MOE_BM, 1), lambda i, be, nu, tok: (i, 0)),
                  pl.BlockSpec((None, d, ff), lambda i, be, nu, tok: (be[i], 0, 0)),
                  pl.BlockSpec((None, d, ff), lambda i, be, nu, tok: (be[i], 0, 0)),
                  pl.BlockSpec((None, ff, d), lambda i, be, nu, tok: (be[i], 0, 0))],
        out_specs=pl.BlockSpec((MOE_BM, d), lambda i, be, nu, tok: (i, 0)),
        scratch_shapes=[pltpu.VMEM((MOE_BM, d), F32), pltpu.SemaphoreType.DMA(())],
    )
    return pl.pallas_call(
        _expert_kernel,
        grid_spec=grid_spec,
        out_shape=jax.ShapeDtypeStruct((n_slots, d), F32),
        compiler_params=pltpu.CompilerParams(
            dimension_semantics=("arbitrary",), vmem_limit_bytes=40 * 1024 * 1024),
        name="experts",
    )(block_e, n_used, slot_tok, m, slot_gate, w_gate, w_up, w_down)


def _combine_kernel(pa_ref, pb_ref, h2_ref, ys_hbm, g_ref, o_ref, ya_ref, yb_ref, sem_a, sem_b, *, nblk):
    bi = pl.program_id(0)
    j = pl.program_id(1)
    base = (bi * nblk + 1 + j) * BLOCK
    _row_gather(ys_hbm, ya_ref, pa_ref, base, BLOCK, sem_a)
    _row_gather(ys_hbm, yb_ref, pb_ref, base, BLOCK, sem_b)
    h = h2_ref[...] + (ya_ref[...] + yb_ref[...])
    o_ref[...] = _rms(h) * g_ref[...]


def _combine(pos_a, pos_b, h2, ys, g_final, b, lp):
    t, d = h2.shape
    nblk = lp // BLOCK
    nreal = nblk - 1
    kern = functools.partial(_combine_kernel, nblk=nblk)
    grid_spec = pltpu.PrefetchScalarGridSpec(
        num_scalar_prefetch=2,
        grid=(b, nreal),
        in_specs=[pl.BlockSpec((BLOCK, d), lambda bi, j, pa, pb: (bi * nblk + 1 + j, 0)),
                  pl.BlockSpec(memory_space=pl.ANY),
                  pl.BlockSpec((1, d), lambda bi, j, pa, pb: (0, 0))],
        out_specs=pl.BlockSpec((BLOCK, d), lambda bi, j, pa, pb: (bi * nreal + j, 0)),
        scratch_shapes=[pltpu.VMEM((BLOCK, d), F32), pltpu.VMEM((BLOCK, d), F32),
                        pltpu.SemaphoreType.DMA(()), pltpu.SemaphoreType.DMA(())],
    )
    return pl.pallas_call(
        kern,
        grid_spec=grid_spec,
        out_shape=jax.ShapeDtypeStruct((b * nreal * BLOCK, d), F32),
        compiler_params=pltpu.CompilerParams(dimension_semantics=("arbitrary", "arbitrary")),
        name="combine",
    )(pos_a, pos_b, h2, ys, g_final)


def _route(logits, n_tok):
    grp_logits = logits[:, :N_GROUPS]
    exp_logits = logits[:, N_GROUPS:N_GROUPS + N_EXPERTS].reshape(n_tok, N_GROUPS, EXPERTS_PER_GROUP)
    grp_probs = jax.nn.softmax(grp_logits, axis=-1)
    grp = jnp.argmax(grp_probs, axis=-1)
    p_grp = jnp.max(grp_probs, axis=-1)
    in_grp = jnp.take_along_axis(exp_logits, grp[:, None, None], axis=1)[:, 0]
    top_val, top_idx = lax.top_k(in_grp, TOP_K)
    gate = jax.nn.softmax(top_val, axis=-1) * p_grp[:, None]
    expert_id = (grp[:, None] * EXPERTS_PER_GROUP + top_idx).reshape(-1).astype(jnp.int32)
    gate_flat = gate.reshape(-1)
    token_id = jnp.repeat(jnp.arange(n_tok, dtype=jnp.int32), TOP_K)

    n_assign = n_tok * TOP_K
    n_slots = -(-n_assign // MOE_BM) * MOE_BM + N_EXPERTS * MOE_BM
    n_blocks = n_slots // MOE_BM
    onehot = (expert_id[:, None] == jnp.arange(N_EXPERTS, dtype=jnp.int32)[None, :]).astype(jnp.int32)
    run = jnp.cumsum(onehot, axis=0)
    counts = run[-1]
    rank = jnp.sum((run - onehot) * onehot, axis=1)
    padded = (counts + MOE_BM - 1) // MOE_BM * MOE_BM
    padded_end = jnp.cumsum(padded)
    padded_start = padded_end - padded
    dest = (padded_start[expert_id] + rank).astype(jnp.int32)
    slot_tok = jnp.zeros((n_slots,), jnp.int32).at[dest].set(token_id)
    slot_gate = jnp.zeros((n_slots,), F32).at[dest].set(gate_flat)
    block_e = jnp.clip(jnp.searchsorted(padded_end, jnp.arange(n_blocks, dtype=jnp.int32) * MOE_BM,
                                        side='right'), 0, N_EXPERTS - 1).astype(jnp.int32)
    n_used = (padded_end[-1] // MOE_BM).astype(jnp.int32).reshape(1)
    pos = dest.reshape(n_tok, TOP_K)
    return block_e, n_used, slot_tok, slot_gate.reshape(n_slots, 1), pos[:, 0], pos[:, 1]


def kernel(x, meta_tokens, lb_logits, g_mix, w_in, sb_gain, hg_gain, w_out, g_ffn, w_router_group,
           b_router_group, w_router_expert, b_router_expert, w_expert_gate, w_expert_up, w_expert_down,
           g_final):
    b, seq, d = x.shape
    depth = w_in.shape[0]
    assert depth == 1, "single-layer block"
    lp = seq + BLOCK
    t = b * lp
    tm = 512 if t % 512 == 0 else BLOCK

    lead = jnp.zeros((b, LEAD_PAD, d), x.dtype)
    meta = jnp.broadcast_to(meta_tokens.astype(x.dtype)[None], (b, N_META, d))
    h = jnp.concatenate([lead, meta, x], axis=1).reshape(t, d)
    real = (jnp.arange(lp) >= LEAD_PAD).astype(F32)
    mask = jnp.broadcast_to(real[None, :], (b, lp)).reshape(t, 1)
    lower_bounds = jnp.cumsum(jax.nn.softmax(lb_logits.astype(F32), axis=0), axis=0)

    layer = 0
    q, k0, k1, v0, v1, hq, hk, hv, lf, hg = _proj(
        h, mask, g_mix[layer][None, :], w_in[layer].astype(BF16), lower_bounds[layer][None, :], tm)
    o_sb = _sb_attention(q, k0, k1, v0, v1, sb_gain[layer][None, :], b, lp)
    o_hg = _hgrn2(hq, hk, hv, lf, hg, hg_gain[layer][None, :], b, lp)

    w_r = jnp.zeros((d, LANES), F32)
    w_r = w_r.at[:, :N_GROUPS].set(w_router_group[layer])
    w_r = w_r.at[:, N_GROUPS:N_GROUPS + N_EXPERTS].set(w_router_expert[layer])
    b_r = jnp.zeros((1, LANES), F32)
    b_r = b_r.at[0, :N_GROUPS].set(b_router_group[layer])
    b_r = b_r.at[0, N_GROUPS:N_GROUPS + N_EXPERTS].set(b_router_expert[layer])
    h2, m, logits = _outproj(o_sb, o_hg, h, w_out[layer].astype(BF16), g_ffn[layer][None, :], w_r, b_r, tm)

    block_e, n_used, slot_tok, slot_gate, pos_a, pos_b = _route(logits, t)
    ys = _experts(block_e, n_used, slot_tok, m, slot_gate,
                  w_expert_gate[layer], w_expert_up[layer], w_expert_down[layer])
    out = _combine(pos_a, pos_b, h2, ys, g_final[None, :], b, lp)
    return out.reshape(b, seq, d)
```

```python
import functools

import numpy as np
import jax
import jax.numpy as jnp
from jax import lax
from jax.experimental import pallas as pl
from jax.experimental.pallas import tpu as pltpu

BLOCK = 128
N_META = 16
LEAD_PAD = BLOCK - N_META
SB_HEAD_DIM = 64
HG_HEAD_DIM = 128
SUB = 16
Q_SUB = 4
N_GROUPS = 4
EXPERTS_PER_GROUP = 8
N_EXPERTS = N_GROUPS * EXPERTS_PER_GROUP
TOP_K = 2
MOE_BM = 128
EPS = 1e-6
LANES = 128

F32 = jnp.float32
BF16 = jnp.bfloat16


def _dot(a, b):
    return jnp.dot(a, b, preferred_element_type=F32)


def _dot_nt(a, b):
    return lax.dot_general(a, b, (((1,), (1,)), ((), ())), preferred_element_type=F32)


def _dot_tn(a, b):
    return lax.dot_general(a, b, (((0,), (0,)), ((), ())), preferred_element_type=F32)


def _rms(x):
    return x * lax.rsqrt(jnp.mean(x * x, axis=-1, keepdims=True) + EPS)


def _proj_kernel(h_ref, mask_ref, g_ref, w_ref, lb_ref,
                 q_ref, k0_ref, k1_ref, v0_ref, v1_ref,
                 hq_ref, hk_ref, hv_ref, lf_ref, hg_ref, *, width):
    a = (_rms(h_ref[...]) * g_ref[...]).astype(BF16)

    def p(i):
        return _dot(a, w_ref[:, i * width:(i + 1) * width])

    lane = lax.broadcasted_iota(jnp.int32, (1, width), 1)
    head0 = (lane & (LANES - 1)) < SB_HEAD_DIM
    q_ref[...] = (p(0) * (SB_HEAD_DIM ** -0.5)).astype(BF16)
    k = p(1)
    k0_ref[...] = jnp.where(head0, k, 0.0).astype(BF16)
    k1_ref[...] = jnp.where(head0, 0.0, k).astype(BF16)
    v = p(2)
    v0_ref[...] = jnp.where(head0, v, 0.0).astype(BF16)
    v1_ref[...] = jnp.where(head0, 0.0, v).astype(BF16)
    hq = p(3)
    hq_ref[...] = hq * jax.nn.sigmoid(hq)
    f = p(4)
    lb = lb_ref[...]
    sig = jax.nn.sigmoid(f)
    lf_ref[...] = jnp.log(lb + (1.0 - lb) * sig)
    hk_ref[...] = mask_ref[...] * ((1.0 - lb) * jax.nn.sigmoid(-f))
    hv_ref[...] = p(5)
    g = p(6)
    hg_ref[...] = g * jax.nn.sigmoid(g)


def _proj(h, mask, g_mix, w_in, lb, tm):
    t, d = h.shape
    width = d // 2
    kern = functools.partial(_proj_kernel, width=width)
    row = lambda i: (i, 0)
    const = lambda i: (0, 0)
    outs = ([jax.ShapeDtypeStruct((t, width), BF16)] * 5
            + [jax.ShapeDtypeStruct((t, width), F32)] * 5)
    return pl.pallas_call(
        kern,
        grid=(t // tm,),
        in_specs=[pl.BlockSpec((tm, d), row), pl.BlockSpec((tm, 1), row),
                  pl.BlockSpec((1, d), const), pl.BlockSpec((d, 7 * width), const),
                  pl.BlockSpec((1, width), const)],
        out_specs=[pl.BlockSpec((tm, width), row)] * 10,
        out_shape=outs,
        compiler_params=pltpu.CompilerParams(
            dimension_semantics=("arbitrary",), vmem_limit_bytes=52 * 1024 * 1024),
        name="proj",
    )(h, mask, g_mix, w_in, lb)


def _sb_kernel(q_ref, k0_ref, k1_ref, v0_ref, v1_ref, mk0_ref, mk1_ref, mv0_ref, mv1_ref,
               tt_ref, gain_ref, o_ref, c_ref, acc_ref, zz_ref, w_ref):
    tq = Q_SUB * BLOCK
    base = pl.program_id(2) * Q_SUB
    c_ref[...] = jnp.zeros_like(c_ref)
    acc_ref[...] = jnp.zeros_like(acc_ref)
    row = lax.broadcasted_iota(jnp.int32, (tq, 2 * BLOCK), 0)
    col = lax.broadcasted_iota(jnp.int32, (tq, 2 * BLOCK), 1) & (BLOCK - 1)

    def k_real(kb):
        rows = pl.ds(pl.multiple_of(kb * BLOCK, BLOCK), BLOCK)
        return jnp.concatenate([k0_ref[rows, :], k1_ref[rows, :]], axis=0)

    def v_real(kb):
        rows = pl.ds(pl.multiple_of(kb * BLOCK, BLOCK), BLOCK)
        return jnp.concatenate([v0_ref[rows, :], v1_ref[rows, :]], axis=0)

    def k_real_or_prefix(kb):
        k_prefix = jnp.concatenate([mk0_ref[...], mk1_ref[...]], axis=0)
        return jnp.where(kb >= 0, k_real(jnp.maximum(kb, 0)), k_prefix)

    def step(mask, k_next, v_prev):
        zz = zz_ref[...]
        if k_next is not None:
            zz_ref[...] = _dot_nt(q_ref[...], k_next)
        if v_prev is not None:
            acc_ref[...] += _dot(w_ref[...], v_prev)
        sp = jnp.maximum(zz, 0.0) + jnp.log(1.0 + jnp.exp(-jnp.abs(zz)))
        spm = sp if mask is None else jnp.where(mask, sp, 0.0)
        hi = spm.astype(BF16)
        lo = (spm - hi.astype(F32)).astype(BF16)
        tt = tt_ref[...]
        tail = jnp.concatenate(
            [_dot(jnp.concatenate([hi[:, h * BLOCK:(h + 1) * BLOCK], lo[:, h * BLOCK:(h + 1) * BLOCK]], axis=1), tt)
             for h in range(2)], axis=1)
        c = c_ref[...]
        w = jnp.exp(zz - sp + tail + c)
        if mask is not None:
            w = jnp.where(mask, w, 0.0)
        w_ref[...] = w.astype(BF16)
        tot0 = jnp.sum(spm[:, :BLOCK], axis=-1, keepdims=True)
        tot1 = jnp.sum(spm[:, BLOCK:], axis=-1, keepdims=True)
        c_ref[...] = c - jnp.concatenate([jnp.broadcast_to(tot0, (tq, BLOCK)),
                                          jnp.broadcast_to(tot1, (tq, BLOCK))], axis=1)

    top = base + Q_SUB - 1
    zz_ref[...] = _dot_nt(q_ref[...], k_real(top))
    for jj in reversed(range(Q_SUB)):
        kb = base + jj
        step(col < row - jj * BLOCK,
             k_real(kb - 1) if jj > 0 else k_real_or_prefix(kb - 1),
             v_real(kb + 1) if jj < Q_SUB - 1 else None)

    def body(it, carry):
        kb = base - 1 - it
        step(None, k_real_or_prefix(kb - 1), v_real(kb + 1))
        return carry

    lax.fori_loop(0, base, body, 0)

    step(col >= LEAD_PAD, None, v_real(0))
    acc_ref[...] += _dot(w_ref[...], jnp.concatenate([mv0_ref[...], mv1_ref[...]], axis=0))

    head0 = lax.broadcasted_iota(jnp.int32, (tq, LANES), 1) < SB_HEAD_DIM
    o = acc_ref[...]
    o2 = o * o
    s0 = jnp.sum(jnp.where(head0, o2, 0.0), axis=-1, keepdims=True) * (1.0 / SB_HEAD_DIM)
    s1 = jnp.sum(jnp.where(head0, 0.0, o2), axis=-1, keepdims=True) * (1.0 / SB_HEAD_DIM)
    r = jnp.where(head0, lax.rsqrt(s0 + EPS), lax.rsqrt(s1 + EPS))
    o_ref[...] = (o * r * gain_ref[...]).astype(o_ref.dtype)


def _sb_tail_matrix():
    j = np.arange(BLOCK)[:, None]
    s = np.arange(BLOCK)[None, :]
    tri = np.where(j > s, -1.0, 0.0)
    return jnp.asarray(np.concatenate([tri, tri], axis=0), dtype=BF16)


def _sb_attention(q, kv, kv_meta, gain, b, seq):
    t, width = q.shape
    nqt = seq // (Q_SUB * BLOCK)
    npair = width // LANES
    kv3 = [a.reshape(b, seq, width) for a in kv]
    tq = Q_SUB * BLOCK
    kv_spec = pl.BlockSpec((None, seq, LANES), lambda bi, hp, qt: (bi, 0, hp))
    meta_spec = pl.BlockSpec((BLOCK, LANES), lambda bi, hp, qt: (0, hp))
    return pl.pallas_call(
        _sb_kernel,
        grid=(b, npair, nqt),
        in_specs=[pl.BlockSpec((tq, LANES), lambda bi, hp, qt: (bi * nqt + qt, hp)),
                  kv_spec, kv_spec, kv_spec, kv_spec,
                  meta_spec, meta_spec, meta_spec, meta_spec,
                  pl.BlockSpec((2 * BLOCK, BLOCK), lambda bi, hp, qt: (0, 0)),
                  pl.BlockSpec((1, LANES), lambda bi, hp, qt: (0, hp))],
        out_specs=pl.BlockSpec((tq, LANES), lambda bi, hp, qt: (bi * nqt + qt, hp)),
        out_shape=jax.ShapeDtypeStruct((t, width), BF16),
        scratch_shapes=[pltpu.VMEM((tq, 2 * BLOCK), F32), pltpu.VMEM((tq, LANES), F32),
                        pltpu.VMEM((tq, 2 * BLOCK), F32), pltpu.VMEM((tq, 2 * BLOCK), BF16)],
        compiler_params=pltpu.CompilerParams(
            dimension_semantics=("arbitrary", "arbitrary", "arbitrary"),
            vmem_limit_bytes=40 * 1024 * 1024),
        name="sb_attn",
    )(q, *kv3, *kv_meta, _sb_tail_matrix(), gain)


def _hg_block(hq_ref, hk_ref, hv_ref, lf_ref, lmat_ref, st_ref, a_ref, qt_ref, kh_ref, oacc_ref, n_heads,
              with_output):
    lf = lf_ref[...]
    h1 = lf.astype(BF16)
    r1 = lf - h1.astype(F32)
    h2 = r1.astype(BF16)
    h3 = (r1 - h2.astype(F32)).astype(BF16)
    lmat = lmat_ref[...]
    cs = _dot(lmat, h1) + _dot(lmat, h2) + _dot(lmat, h3)
    a = cs[:BLOCK]
    alast = cs[BLOCK:]
    a_ref[...] = a
    kh_ref[...] = hk_ref[...] * jnp.exp(alast - a)
    if with_output:
        qt_ref[...] = hq_ref[...] * jnp.exp(a)
    ridx = lax.broadcasted_iota(jnp.int32, (SUB, 1), 0)

    def sub_body(i, carry):
        r0 = pl.multiple_of(i * SUB, SUB)
        rows = pl.ds(r0, SUB)
        for hd in range(n_heads):
            cols = slice(hd * HG_HEAD_DIM, (hd + 1) * HG_HEAD_DIM)
            a_i = a_ref[rows, cols]
            v_i = hv_ref[rows, cols]
            st = st_ref[hd]
            if with_output:
                q_i = hq_ref[rows, cols]
                k_i = hk_ref[rows, cols]
                o_i = _dot_nt(qt_ref[rows, cols], st)
                for s in range(SUB):
                    dec = jnp.exp(jnp.minimum(a_i - a_i[s:s + 1, :], 0.0))
                    sc = jnp.sum(q_i * dec * k_i[s:s + 1, :], axis=-1, keepdims=True)
                    sc = jnp.where(ridx >= s, sc, 0.0)
                    o_i = o_i + sc * v_i[s:s + 1, :]
                oacc_ref[rows, cols] = o_i
            decay = jnp.exp(a_i[SUB - 1:SUB, :])
            st_ref[hd] = st * decay + _dot_tn(v_i, kh_ref[rows, cols])
        return carry

    lax.fori_loop(0, BLOCK // SUB, sub_body, 0)


def _hg_kernel(hq_ref, hk_ref, hv_ref, lf_ref, gate_ref, mk_ref, mv_ref, mlf_ref, gain_ref, lmat_ref, o_ref,
               st_ref, a_ref, qt_ref, kh_ref, oacc_ref, *, n_heads):
    scratch = (st_ref, a_ref, qt_ref, kh_ref, oacc_ref)

    @pl.when(pl.program_id(1) == 0)
    def _():
        st_ref[...] = jnp.zeros_like(st_ref)
        _hg_block(None, mk_ref, mv_ref, mlf_ref, lmat_ref, *scratch, n_heads, with_output=False)

    _hg_block(hq_ref, hk_ref, hv_ref, lf_ref, lmat_ref, *scratch, n_heads, with_output=True)
    for hd in range(n_heads):
        cols = slice(hd * HG_HEAD_DIM, (hd + 1) * HG_HEAD_DIM)
        o = _rms(oacc_ref[:, cols]) * gain_ref[:, cols] * gate_ref[:, cols]
        o_ref[:, cols] = o.astype(o_ref.dtype)


def _hg_cumsum_matrix():
    t = np.arange(BLOCK)[:, None]
    s = np.arange(BLOCK)[None, :]
    same = (t // SUB) == (s // SUB)
    incl = np.where(same & (s <= t), 1.0, 0.0)
    full = np.where(same, 1.0, 0.0)
    return jnp.asarray(np.concatenate([incl, full], axis=0), dtype=BF16)


def _hgrn2(hq, hk, hv, lf, gate, meta, gain, b, seq):
    t, width = hq.shape
    nc = seq // BLOCK
    n_heads = width // HG_HEAD_DIM
    blk = pl.BlockSpec((BLOCK, width), lambda bi, ci: (bi * nc + ci, 0))
    mblk = pl.BlockSpec((BLOCK, width), lambda bi, ci: (0, 0))
    kern = functools.partial(_hg_kernel, n_heads=n_heads)
    return pl.pallas_call(
        kern,
        grid=(b, nc),
        in_specs=[blk, blk, blk, blk, blk, mblk, mblk, mblk,
                  pl.BlockSpec((1, width), lambda bi, ci: (0, 0)),
                  pl.BlockSpec((2 * BLOCK, BLOCK), lambda bi, ci: (0, 0))],
        out_specs=blk,
        out_shape=jax.ShapeDtypeStruct((t, width), BF16),
        scratch_shapes=[pltpu.VMEM((n_heads, HG_HEAD_DIM, HG_HEAD_DIM), F32),
                        pltpu.VMEM((BLOCK, width), F32), pltpu.VMEM((BLOCK, width), F32),
                        pltpu.VMEM((BLOCK, width), F32), pltpu.VMEM((BLOCK, width), F32)],
        compiler_params=pltpu.CompilerParams(dimension_semantics=("arbitrary", "arbitrary")),
        name="hgrn2",
    )(hq, hk, hv, lf, gate, *meta, gain, _hg_cumsum_matrix())


def _outproj_kernel(osb_ref, ohg_ref, h_ref, w_ref, g_ref, wr_ref, br_ref, h2_ref, m_ref, lg_ref, *, width):
    h2 = h_ref[...] + _dot(osb_ref[...], w_ref[:width, :]) + _dot(ohg_ref[...], w_ref[width:, :])
    h2_ref[...] = h2
    m = _rms(h2) * g_ref[...]
    m_ref[...] = m
    lg_ref[...] = _dot(m, wr_ref[...]) + br_ref[...]


def _outproj(o_sb, o_hg, h, w_out, g_ffn, w_r, b_r, tm):
    t, d = h.shape
    width = o_sb.shape[1]
    row = lambda i: (i, 0)
    const = lambda i: (0, 0)
    kern = functools.partial(_outproj_kernel, width=width)
    return pl.pallas_call(
        kern,
        grid=(t // tm,),
        in_specs=[pl.BlockSpec((tm, width), row), pl.BlockSpec((tm, width), row),
                  pl.BlockSpec((tm, d), row), pl.BlockSpec((2 * width, d), const),
                  pl.BlockSpec((1, d), const), pl.BlockSpec((d, LANES), const),
                  pl.BlockSpec((1, LANES), const)],
        out_specs=[pl.BlockSpec((tm, d), row), pl.BlockSpec((tm, d), row),
                   pl.BlockSpec((tm, LANES), row)],
        out_shape=[jax.ShapeDtypeStruct((t, d), F32), jax.ShapeDtypeStruct((t, d), F32),
                   jax.ShapeDtypeStruct((t, LANES), F32)],
        compiler_params=pltpu.CompilerParams(
            dimension_semantics=("arbitrary",), vmem_limit_bytes=40 * 1024 * 1024),
        name="outproj",
    )(o_sb, o_hg, h, w_out, g_ffn, w_r, b_r)


def _row_gather(src_hbm, dst_ref, idx_ref, base, n, sem):
    def issue(r, carry):
        pltpu.make_async_copy(src_hbm.at[pl.ds(idx_ref[base + r], 1), :],
                              dst_ref.at[pl.ds(r, 1), :], sem).start()
        return carry
    lax.fori_loop(0, n, issue, 0)

    def drain(r, carry):
        pltpu.make_async_copy(src_hbm.at[pl.ds(0, 1), :], dst_ref.at[pl.ds(r, 1), :], sem).wait()
        return carry
    lax.fori_loop(0, n, drain, 0)


def _expert_kernel(be_ref, nu_ref, tok_ref, m_hbm, gate_ref, wg_ref, wu_ref, wd_ref, y_ref, xs_ref, sem):
    i = pl.program_id(0)

    @pl.when(i < nu_ref[0])
    def _():
        _row_gather(m_hbm, xs_ref, tok_ref, i * MOE_BM, MOE_BM, sem)
        xs = xs_ref[...]
        gt = _dot(xs, wg_ref[...])
        hb = gt * jax.nn.sigmoid(gt) * _dot(xs, wu_ref[...])
        y_ref[...] = _dot(hb, wd_ref[...]) * gate_ref[...]

    @pl.when(i >= nu_ref[0])
    def _():
        y_ref[...] = jnp.zeros_like(y_ref)


def _experts(block_e, n_used, slot_tok, m, slot_gate, w_gate, w_up, w_down):
    t, d = m.shape
    n_slots = slot_tok.shape[0]
    ff = w_gate.shape[-1]
    grid_spec = pltpu.PrefetchScalarGridSpec(
        num_scalar_prefetch=3,
        grid=(n_slots // MOE_BM,),
        in_specs=[pl.BlockSpec(memory_space=pl.ANY),
                  pl.BlockSpec((MOE_BM, 1), lambda i, be, nu, tok: (i, 0)),
                  pl.BlockSpec((None, d, ff), lambda i, be, nu, tok: (be[i], 0, 0)),
                  pl.BlockSpec((None, d, ff), lambda i, be, nu, tok: (be[i], 0, 0)),
                  pl.BlockSpec((None, ff, d), lambda i, be, nu, tok: (be[i], 0, 0))],
        out_specs=pl.BlockSpec((MOE_BM, d), lambda i, be, nu, tok: (i, 0)),
        scratch_shapes=[pltpu.VMEM((MOE_BM, d), F32), pltpu.SemaphoreType.DMA(())],
    )
    return pl.pallas_call(
        _expert_kernel,
        grid_spec=grid_spec,
        out_shape=jax.ShapeDtypeStruct((n_slots, d), F32),
        compiler_params=pltpu.CompilerParams(
            dimension_semantics=("arbitrary",), vmem_limit_bytes=40 * 1024 * 1024),
        name="experts",
    )(block_e, n_used, slot_tok, m, slot_gate, w_gate, w_up, w_down)


def _combine_kernel(pa_ref, pb_ref, h2_ref, ys_hbm, g_ref, o_ref, ya_ref, yb_ref, sem_a, sem_b):
    base = pl.program_id(0) * BLOCK
    _row_gather(ys_hbm, ya_ref, pa_ref, base, BLOCK, sem_a)
    _row_gather(ys_hbm, yb_ref, pb_ref, base, BLOCK, sem_b)
    h = h2_ref[...] + (ya_ref[...] + yb_ref[...])
    o_ref[...] = _rms(h) * g_ref[...]


def _combine(pos_a, pos_b, h2, ys, g_final):
    t, d = h2.shape
    grid_spec = pltpu.PrefetchScalarGridSpec(
        num_scalar_prefetch=2,
        grid=(t // BLOCK,),
        in_specs=[pl.BlockSpec((BLOCK, d), lambda i, pa, pb: (i, 0)),
                  pl.BlockSpec(memory_space=pl.ANY),
                  pl.BlockSpec((1, d), lambda i, pa, pb: (0, 0))],
        out_specs=pl.BlockSpec((BLOCK, d), lambda i, pa, pb: (i, 0)),
        scratch_shapes=[pltpu.VMEM((BLOCK, d), F32), pltpu.VMEM((BLOCK, d), F32),
                        pltpu.SemaphoreType.DMA(()), pltpu.SemaphoreType.DMA(())],
    )
    return pl.pallas_call(
        _combine_kernel,
        grid_spec=grid_spec,
        out_shape=jax.ShapeDtypeStruct((t, d), F32),
        compiler_params=pltpu.CompilerParams(dimension_semantics=("arbitrary",)),
        name="combine",
    )(pos_a, pos_b, h2, ys, g_final)


def _route(logits, n_tok):
    grp_logits = logits[:, :N_GROUPS]
    exp_logits = logits[:, N_GROUPS:N_GROUPS + N_EXPERTS].reshape(n_tok, N_GROUPS, EXPERTS_PER_GROUP)
    grp_probs = jax.nn.softmax(grp_logits, axis=-1)
    grp = jnp.argmax(grp_probs, axis=-1)
    p_grp = jnp.max(grp_probs, axis=-1)
    in_grp = jnp.take_along_axis(exp_logits, grp[:, None, None], axis=1)[:, 0]
    top_val, top_idx = lax.top_k(in_grp, TOP_K)
    gate = jax.nn.softmax(top_val, axis=-1) * p_grp[:, None]
    expert_id = (grp[:, None] * EXPERTS_PER_GROUP + top_idx).reshape(-1).astype(jnp.int32)
    gate_flat = gate.reshape(-1)
    token_id = jnp.repeat(jnp.arange(n_tok, dtype=jnp.int32), TOP_K)

    n_assign = n_tok * TOP_K
    n_slots = -(-n_assign // MOE_BM) * MOE_BM + N_EXPERTS * MOE_BM
    n_blocks = n_slots // MOE_BM
    onehot = (expert_id[:, None] == jnp.arange(N_EXPERTS, dtype=jnp.int32)[None, :]).astype(jnp.int32)
    run = jnp.cumsum(onehot, axis=0)
    counts = run[-1]
    rank = jnp.sum((run - onehot) * onehot, axis=1)
    padded = (counts + MOE_BM - 1) // MOE_BM * MOE_BM
    padded_end = jnp.cumsum(padded)
    padded_start = padded_end - padded
    dest = (padded_start[expert_id] + rank).astype(jnp.int32)
    slot_tok = jnp.zeros((n_slots,), jnp.int32).at[dest].set(token_id)
    slot_gate = jnp.zeros((n_slots,), F32).at[dest].set(gate_flat)
    block_e = jnp.clip(jnp.searchsorted(padded_end, jnp.arange(n_blocks, dtype=jnp.int32) * MOE_BM,
                                        side='right'), 0, N_EXPERTS - 1).astype(jnp.int32)
    n_used = (padded_end[-1] // MOE_BM).astype(jnp.int32).reshape(1)
    pos = dest.reshape(n_tok, TOP_K)
    return block_e, n_used, slot_tok, slot_gate.reshape(n_slots, 1), pos[:, 0], pos[:, 1]


def kernel(x, meta_tokens, lb_logits, g_mix, w_in, sb_gain, hg_gain, w_out, g_ffn, w_router_group,
           b_router_group, w_router_expert, b_router_expert, w_expert_gate, w_expert_up, w_expert_down,
           g_final):
    b, seq, d = x.shape
    depth = w_in.shape[0]
    assert depth == 1, "single-layer block"
    assert seq % (Q_SUB * BLOCK) == 0
    t = b * seq
    tm = 512
    layer = 0

    xr = x.reshape(t, d)
    prefix = jnp.concatenate([jnp.zeros((LEAD_PAD, d), x.dtype), meta_tokens.astype(x.dtype)], axis=0)
    prefix_mask = (jnp.arange(BLOCK) >= LEAD_PAD).astype(F32)[:, None]
    lower_bounds = jnp.cumsum(jax.nn.softmax(lb_logits.astype(F32), axis=0), axis=0)
    lb = lower_bounds[layer][None, :]
    g_mix_l = g_mix[layer][None, :]
    w_in_l = w_in[layer].astype(BF16)

    q, k0, k1, v0, v1, hq, hk, hv, lf, hg = _proj(xr, jnp.ones((t, 1), F32), g_mix_l, w_in_l, lb, tm)
    _, mk0, mk1, mv0, mv1, _, mhk, mhv, mlf, _ = _proj(prefix, prefix_mask, g_mix_l, w_in_l, lb, BLOCK)

    o_sb = _sb_attention(q, (k0, k1, v0, v1), (mk0, mk1, mv0, mv1), sb_gain[layer][None, :], b, seq)
    o_hg = _hgrn2(hq, hk, hv, lf, hg, (mhk, mhv, mlf), hg_gain[layer][None, :], b, seq)

    w_r = jnp.zeros((d, LANES), F32)
    w_r = w_r.at[:, :N_GROUPS].set(w_router_group[layer])
    w_r = w_r.at[:, N_GROUPS:N_GROUPS + N_EXPERTS].set(w_router_expert[layer])
    b_r = jnp.zeros((1, LANES), F32)
    b_r = b_r.at[0, :N_GROUPS].set(b_router_group[layer])
    b_r = b_r.at[0, N_GROUPS:N_GROUPS + N_EXPERTS].set(b_router_expert[layer])
    h2, m, logits = _outproj(o_sb, o_hg, xr, w_out[layer].astype(BF16), g_ffn[layer][None, :], w_r, b_r, tm)

    block_e, n_used, slot_tok, slot_gate, pos_a, pos_b = _route(logits, t)
    ys = _experts(block_e, n_used, slot_tok, m, slot_gate,
                  w_expert_gate[layer], w_expert_up[layer], w_expert_down[layer])
    out = _combine(pos_a, pos_b, h2, ys, g_final[None, :])
    return out.reshape(b, seq, d)
```

```python
import functools

import numpy as np
import jax
import jax.numpy as jnp
from jax import lax
from jax.experimental import pallas as pl
from jax.experimental.pallas import tpu as pltpu

BLOCK = 128
N_META = 16
LEAD_PAD = BLOCK - N_META
SB_HEAD_DIM = 64
HG_HEAD_DIM = 128
SUB = 16
Q_SUB = 4
N_GROUPS = 4
EXPERTS_PER_GROUP = 8
N_EXPERTS = N_GROUPS * EXPERTS_PER_GROUP
TOP_K = 2
MOE_BM = 128
EPS = 1e-6
LANES = 128

F32 = jnp.float32
BF16 = jnp.bfloat16


def _dot(a, b):
    return jnp.dot(a, b, preferred_element_type=F32)


def _dot_nt(a, b):
    return lax.dot_general(a, b, (((1,), (1,)), ((), ())), preferred_element_type=F32)


def _dot_tn(a, b):
    return lax.dot_general(a, b, (((0,), (0,)), ((), ())), preferred_element_type=F32)


def _rms(x):
    return x * lax.rsqrt(jnp.mean(x * x, axis=-1, keepdims=True) + EPS)


def _proj_kernel(h_ref, mask_ref, g_ref, w_ref, lb_ref,
                 q_ref, k0_ref, k1_ref, v0_ref, v1_ref,
                 hq_ref, hk_ref, hv_ref, lf_ref, hg_ref, *, width):
    a = (_rms(h_ref[...]) * g_ref[...]).astype(BF16)

    def p(i):
        return _dot(a, w_ref[:, i * width:(i + 1) * width])

    lane = lax.broadcasted_iota(jnp.int32, (1, width), 1)
    head0 = (lane & (LANES - 1)) < SB_HEAD_DIM
    q_ref[...] = (p(0) * (SB_HEAD_DIM ** -0.5)).astype(BF16)
    k = p(1)
    k0_ref[...] = jnp.where(head0, k, 0.0).astype(BF16)
    k1_ref[...] = jnp.where(head0, 0.0, k).astype(BF16)
    v = p(2)
    v0_ref[...] = jnp.where(head0, v, 0.0).astype(BF16)
    v1_ref[...] = jnp.where(head0, 0.0, v).astype(BF16)
    hq = p(3)
    hq_ref[...] = hq * jax.nn.sigmoid(hq)
    f = p(4)
    lb = lb_ref[...]
    sig = jax.nn.sigmoid(f)
    lf_ref[...] = jnp.log(lb + (1.0 - lb) * sig)
    hk_ref[...] = mask_ref[...] * ((1.0 - lb) * jax.nn.sigmoid(-f))
    hv_ref[...] = p(5)
    g = p(6)
    hg_ref[...] = g * jax.nn.sigmoid(g)


def _proj(h, mask, g_mix, w_in, lb, tm):
    t, d = h.shape
    width = d // 2
    kern = functools.partial(_proj_kernel, width=width)
    row = lambda i: (i, 0)
    const = lambda i: (0, 0)
    outs = ([jax.ShapeDtypeStruct((t, width), BF16)] * 5
            + [jax.ShapeDtypeStruct((t, width), F32)] * 5)
    return pl.pallas_call(
        kern,
        grid=(t // tm,),
        in_specs=[pl.BlockSpec((tm, d), row), pl.BlockSpec((tm, 1), row),
                  pl.BlockSpec((1, d), const), pl.BlockSpec((d, 7 * width), const),
                  pl.BlockSpec((1, width), const)],
        out_specs=[pl.BlockSpec((tm, width), row)] * 10,
        out_shape=outs,
        compiler_params=pltpu.CompilerParams(
            dimension_semantics=("arbitrary",), vmem_limit_bytes=52 * 1024 * 1024),
        name="proj",
    )(h, mask, g_mix, w_in, lb)


def _sb_kernel(q_ref, k0_ref, k1_ref, v0_ref, v1_ref, mk0_ref, mk1_ref, mv0_ref, mv1_ref,
               tt_ref, gain_ref, o_ref, c_ref, acc_ref, zz_ref, w_ref):
    tq = Q_SUB * BLOCK
    base = pl.program_id(2) * Q_SUB
    c_ref[...] = jnp.zeros_like(c_ref)
    acc_ref[...] = jnp.zeros_like(acc_ref)
    row = lax.broadcasted_iota(jnp.int32, (tq, 2 * BLOCK), 0)
    col = lax.broadcasted_iota(jnp.int32, (tq, 2 * BLOCK), 1) & (BLOCK - 1)

    def k_real(kb):
        rows = pl.ds(pl.multiple_of(kb * BLOCK, BLOCK), BLOCK)
        return jnp.concatenate([k0_ref[rows, :], k1_ref[rows, :]], axis=0)

    def v_real(kb):
        rows = pl.ds(pl.multiple_of(kb * BLOCK, BLOCK), BLOCK)
        return jnp.concatenate([v0_ref[rows, :], v1_ref[rows, :]], axis=0)

    def k_real_or_prefix(kb):
        k_prefix = jnp.concatenate([mk0_ref[...], mk1_ref[...]], axis=0)
        return jnp.where(kb >= 0, k_real(jnp.maximum(kb, 0)), k_prefix)

    def step(mask, k_next, v_prev):
        zz = zz_ref[...]
        if k_next is not None:
            zz_ref[...] = _dot_nt(q_ref[...], k_next)
        if v_prev is not None:
            acc_ref[...] += _dot(w_ref[...], v_prev)
        sp = jnp.maximum(zz, 0.0) + jnp.log(1.0 + jnp.exp(-jnp.abs(zz)))
        spm = sp if mask is None else jnp.where(mask, sp, 0.0)
        hi = spm.astype(BF16)
        lo = (spm - hi.astype(F32)).astype(BF16)
        tt = tt_ref[...]
        tail = jnp.concatenate(
            [_dot(jnp.concatenate([hi[:, h * BLOCK:(h + 1) * BLOCK], lo[:, h * BLOCK:(h + 1) * BLOCK]], axis=1), tt)
             for h in range(2)], axis=1)
        c = c_ref[...]
        w = jnp.exp(zz - sp + tail + c)
        if mask is not None:
            w = jnp.where(mask, w, 0.0)
        w_ref[...] = w.astype(BF16)
        tot0 = jnp.sum(spm[:, :BLOCK], axis=-1, keepdims=True)
        tot1 = jnp.sum(spm[:, BLOCK:], axis=-1, keepdims=True)
        c_ref[...] = c - jnp.concatenate([jnp.broadcast_to(tot0, (tq, BLOCK)),
                                          jnp.broadcast_to(tot1, (tq, BLOCK))], axis=1)

    top = base + Q_SUB - 1
    zz_ref[...] = _dot_nt(q_ref[...], k_real(top))
    for jj in reversed(range(Q_SUB)):
        kb = base + jj
        step(col < row - jj * BLOCK,
             k_real(kb - 1) if jj > 0 else k_real_or_prefix(kb - 1),
             v_real(kb + 1) if jj < Q_SUB - 1 else None)

    def body(it, carry):
        kb = base - 1 - it
        step(None, k_real_or_prefix(kb - 1), v_real(kb + 1))
        return carry

    lax.fori_loop(0, base, body, 0)

    step(col >= LEAD_PAD, None, v_real(0))
    acc_ref[...] += _dot(w_ref[...], jnp.concatenate([mv0_ref[...], mv1_ref[...]], axis=0))

    head0 = lax.broadcasted_iota(jnp.int32, (tq, LANES), 1) < SB_HEAD_DIM
    o = acc_ref[...]
    o2 = o * o
    s0 = jnp.sum(jnp.where(head0, o2, 0.0), axis=-1, keepdims=True) * (1.0 / SB_HEAD_DIM)
    s1 = jnp.sum(jnp.where(head0, 0.0, o2), axis=-1, keepdims=True) * (1.0 / SB_HEAD_DIM)
    r = jnp.where(head0, lax.rsqrt(s0 + EPS), lax.rsqrt(s1 + EPS))
    o_ref[...] = (o * r * gain_ref[...]).astype(o_ref.dtype)


def _sb_tail_matrix():
    j = np.arange(BLOCK)[:, None]
    s = np.arange(BLOCK)[None, :]
    tri = np.where(j > s, -1.0, 0.0)
    return jnp.asarray(np.concatenate([tri, tri], axis=0), dtype=BF16)


def _sb_attention(q, kv, kv_meta, gain, b, seq):
    t, width = q.shape
    nqt = seq // (Q_SUB * BLOCK)
    npair = width // LANES
    kv3 = [a.reshape(b, seq, width) for a in kv]
    tq = Q_SUB * BLOCK
    kv_spec = pl.BlockSpec((None, seq, LANES), lambda bi, hp, qt: (bi, 0, hp))
    meta_spec = pl.BlockSpec((BLOCK, LANES), lambda bi, hp, qt: (0, hp))
    return pl.pallas_call(
        _sb_kernel,
        grid=(b, npair, nqt),
        in_specs=[pl.BlockSpec((tq, LANES), lambda bi, hp, qt: (bi * nqt + qt, hp)),
                  kv_spec, kv_spec, kv_spec, kv_spec,
                  meta_spec, meta_spec, meta_spec, meta_spec,
                  pl.BlockSpec((2 * BLOCK, BLOCK), lambda bi, hp, qt: (0, 0)),
                  pl.BlockSpec((1, LANES), lambda bi, hp, qt: (0, hp))],
        out_specs=pl.BlockSpec((tq, LANES), lambda bi, hp, qt: (bi * nqt + qt, hp)),
        out_shape=jax.ShapeDtypeStruct((t, width), BF16),
        scratch_shapes=[pltpu.VMEM((tq, 2 * BLOCK), F32), pltpu.VMEM((tq, LANES), F32),
                        pltpu.VMEM((tq, 2 * BLOCK), F32), pltpu.VMEM((tq, 2 * BLOCK), BF16)],
        compiler_params=pltpu.CompilerParams(
            dimension_semantics=("arbitrary", "arbitrary", "arbitrary"),
            vmem_limit_bytes=40 * 1024 * 1024),
        name="sb_attn",
    )(q, *kv3, *kv_meta, _sb_tail_matrix(), gain)


def _hg_block(hq_ref, hk_ref, hv_ref, lf_ref, lmat_ref, st_ref, a_ref, qt_ref, kh_ref, oacc_ref, n_heads,
              with_output):
    lf = lf_ref[...]
    h1 = lf.astype(BF16)
    r1 = lf - h1.astype(F32)
    h2 = r1.astype(BF16)
    h3 = (r1 - h2.astype(F32)).astype(BF16)
    lmat = lmat_ref[...]
    cs = _dot(lmat, h1) + _dot(lmat, h2) + _dot(lmat, h3)
    a = cs[:BLOCK]
    alast = cs[BLOCK:]
    a_ref[...] = a
    kh_ref[...] = hk_ref[...] * jnp.exp(alast - a)
    if with_output:
        qt_ref[...] = hq_ref[...] * jnp.exp(a)
    ridx = lax.broadcasted_iota(jnp.int32, (SUB, 1), 0)

    def sub_body(i, carry):
        r0 = pl.multiple_of(i * SUB, SUB)
        rows = pl.ds(r0, SUB)
        for hd in range(n_heads):
            cols = slice(hd * HG_HEAD_DIM, (hd + 1) * HG_HEAD_DIM)
            a_i = a_ref[rows, cols]
            v_i = hv_ref[rows, cols]
            st = st_ref[hd]
            if with_output:
                q_i = hq_ref[rows, cols]
                k_i = hk_ref[rows, cols]
                o_i = _dot_nt(qt_ref[rows, cols], st)
                for s in range(SUB):
                    dec = jnp.exp(jnp.minimum(a_i - a_i[s:s + 1, :], 0.0))
                    sc = jnp.sum(q_i * dec * k_i[s:s + 1, :], axis=-1, keepdims=True)
                    sc = jnp.where(ridx >= s, sc, 0.0)
                    o_i = o_i + sc * v_i[s:s + 1, :]
                oacc_ref[rows, cols] = o_i
            decay = jnp.exp(a_i[SUB - 1:SUB, :])
            st_ref[hd] = st * decay + _dot_tn(v_i, kh_ref[rows, cols])
        return carry

    lax.fori_loop(0, BLOCK // SUB, sub_body, 0)


def _hg_kernel(hq_ref, hk_ref, hv_ref, lf_ref, gate_ref, mk_ref, mv_ref, mlf_ref, gain_ref, lmat_ref, o_ref,
               st_ref, a_ref, qt_ref, kh_ref, oacc_ref, *, n_heads):
    scratch = (st_ref, a_ref, qt_ref, kh_ref, oacc_ref)

    @pl.when(pl.program_id(1) == 0)
    def _():
        st_ref[...] = jnp.zeros_like(st_ref)
        _hg_block(None, mk_ref, mv_ref, mlf_ref, lmat_ref, *scratch, n_heads, with_output=False)

    _hg_block(hq_ref, hk_ref, hv_ref, lf_ref, lmat_ref, *scratch, n_heads, with_output=True)
    for hd in range(n_heads):
        cols = slice(hd * HG_HEAD_DIM, (hd + 1) * HG_HEAD_DIM)
        o = _rms(oacc_ref[:, cols]) * gain_ref[:, cols] * gate_ref[:, cols]
        o_ref[:, cols] = o.astype(o_ref.dtype)


def _hg_cumsum_matrix():
    t = np.arange(BLOCK)[:, None]
    s = np.arange(BLOCK)[None, :]
    same = (t // SUB) == (s // SUB)
    incl = np.where(same & (s <= t), 1.0, 0.0)
    full = np.where(same, 1.0, 0.0)
    return jnp.asarray(np.concatenate([incl, full], axis=0), dtype=BF16)


def _hgrn2(hq, hk, hv, lf, gate, meta, gain, b, seq):
    t, width = hq.shape
    nc = seq // BLOCK
    n_heads = width // HG_HEAD_DIM
    blk = pl.BlockSpec((BLOCK, width), lambda bi, ci: (bi * nc + ci, 0))
    mblk = pl.BlockSpec((BLOCK, width), lambda bi, ci: (0, 0))
    kern = functools.partial(_hg_kernel, n_heads=n_heads)
    return pl.pallas_call(
        kern,
        grid=(b, nc),
        in_specs=[blk, blk, blk, blk, blk, mblk, mblk, mblk,
                  pl.BlockSpec((1, width), lambda bi, ci: (0, 0)),
                  pl.BlockSpec((2 * BLOCK, BLOCK), lambda bi, ci: (0, 0))],
        out_specs=blk,
        out_shape=jax.ShapeDtypeStruct((t, width), BF16),
        scratch_shapes=[pltpu.VMEM((n_heads, HG_HEAD_DIM, HG_HEAD_DIM), F32),
                        pltpu.VMEM((BLOCK, width), F32), pltpu.VMEM((BLOCK, width), F32),
                        pltpu.VMEM((BLOCK, width), F32), pltpu.VMEM((BLOCK, width), F32)],
        compiler_params=pltpu.CompilerParams(dimension_semantics=("arbitrary", "arbitrary")),
        name="hgrn2",
    )(hq, hk, hv, lf, gate, *meta, gain, _hg_cumsum_matrix())


R_E1, R_E2, R_RANK1, R_RANK2, R_G1, R_G2 = range(6)


def _outproj_kernel(osb_ref, ohg_ref, h_ref, w_ref, g_ref, wr_ref, br_ref, tri_ref,
                    h2_ref, m_ref, route_ref, cnt_ref, carry_ref, *, width):
    @pl.when(pl.program_id(0) == 0)
    def _():
        carry_ref[...] = jnp.zeros_like(carry_ref)

    h2 = h_ref[...] + _dot(osb_ref[...], w_ref[:width, :]) + _dot(ohg_ref[...], w_ref[width:, :])
    h2_ref[...] = h2
    m = _rms(h2) * g_ref[...]
    m_ref[...] = m
    lg = _dot(m, wr_ref[...]) + br_ref[...]
    tm = lg.shape[0]
    lane = lax.broadcasted_iota(jnp.int32, (tm, LANES), 1)
    neg = jnp.float32(-1e30)

    def first_argmax(vals):
        vmax = jnp.max(vals, axis=-1, keepdims=True)
        idx = jnp.min(jnp.where(vals == vmax, lane, LANES), axis=-1, keepdims=True)
        return vmax, idx

    is_grp = lane < N_GROUPS
    gl = jnp.where(is_grp, lg, neg)
    gmax, gidx = first_argmax(gl)
    p_grp = 1.0 / jnp.sum(jnp.where(is_grp, jnp.exp(gl - gmax), 0.0), axis=-1, keepdims=True)
    lo = N_GROUPS + gidx * EXPERTS_PER_GROUP
    el = jnp.where((lane >= lo) & (lane < lo + EXPERTS_PER_GROUP), lg, neg)
    v1, i1 = first_argmax(el)
    sel1 = lane == i1
    v2, i2 = first_argmax(jnp.where(sel1, neg, el))
    sel2 = lane == i2
    dlt = jnp.exp(v2 - v1)
    g1 = p_grp / (1.0 + dlt)
    g2 = g1 * dlt

    chosen = jnp.where(sel1 | sel2, 1.0, 0.0)
    carry = carry_ref[0:1, :]
    before = _dot(tri_ref[...], chosen.astype(BF16)) + carry
    r1 = jnp.sum(jnp.where(sel1, before, 0.0), axis=-1, keepdims=True)
    r2 = jnp.sum(jnp.where(sel2, before, 0.0), axis=-1, keepdims=True)
    carry = carry + jnp.sum(chosen, axis=0, keepdims=True)
    carry_ref[0:1, :] = carry
    cnt_ref[...] = jnp.broadcast_to(carry, cnt_ref.shape)

    rec = jnp.zeros((tm, LANES), F32)
    for ln, val in ((R_E1, (i1 - N_GROUPS).astype(F32)), (R_E2, (i2 - N_GROUPS).astype(F32)),
                    (R_RANK1, r1), (R_RANK2, r2), (R_G1, g1), (R_G2, g2)):
        rec = jnp.where(lane == ln, val, rec)
    route_ref[...] = rec


def _outproj(o_sb, o_hg, h, w_out, g_ffn, w_r, b_r, tm):
    t, d = h.shape
    width = o_sb.shape[1]
    row = lambda i: (i, 0)
    const = lambda i: (0, 0)
    tri = jnp.asarray(np.tril(np.ones((tm, tm), np.float32), -1), dtype=BF16)
    kern = functools.partial(_outproj_kernel, width=width)
    return pl.pallas_call(
        kern,
        grid=(t // tm,),
        in_specs=[pl.BlockSpec((tm, width), row), pl.BlockSpec((tm, width), row),
                  pl.BlockSpec((tm, d), row), pl.BlockSpec((2 * width, d), const),
                  pl.BlockSpec((1, d), const), pl.BlockSpec((d, LANES), const),
                  pl.BlockSpec((1, LANES), const), pl.BlockSpec((tm, tm), const)],
        out_specs=[pl.BlockSpec((tm, d), row), pl.BlockSpec((tm, d), row),
                   pl.BlockSpec((tm, LANES), row), pl.BlockSpec((8, LANES), const)],
        out_shape=[jax.ShapeDtypeStruct((t, d), F32), jax.ShapeDtypeStruct((t, d), F32),
                   jax.ShapeDtypeStruct((t, LANES), F32), jax.ShapeDtypeStruct((8, LANES), F32)],
        scratch_shapes=[pltpu.VMEM((8, LANES), F32)],
        compiler_params=pltpu.CompilerParams(
            dimension_semantics=("arbitrary",), vmem_limit_bytes=40 * 1024 * 1024),
        name="outproj",
    )(o_sb, o_hg, h, w_out, g_ffn, w_r, b_r, tri)


def _row_copy(src, dst, sem):
    return pltpu.make_async_copy(src, dst, sem)


def _dispatch_kernel(d1_ref, d2_ref, zs_ref, zn_ref, m_hbm, xs_hbm, zero_ref, sems, zsem, *, tile, n_tiles):
    i = pl.program_id(0)
    slot = i % 2

    def tile_wait(s):
        _row_copy(m_hbm.at[pl.ds(0, 2 * tile), :], xs_hbm.at[pl.ds(0, 2 * tile), :], sems.at[s]).wait()

    @pl.when(i == 0)
    def _():
        zero_ref[...] = jnp.zeros_like(zero_ref)

        def zero_block(j):
            return _row_copy(zero_ref, xs_hbm.at[pl.ds(pl.multiple_of(zs_ref[j], MOE_BM), MOE_BM), :], zsem)

        for j in range(2 * N_EXPERTS):
            @pl.when(zn_ref[j] > 0)
            def _():
                zero_block(j).start()
        for j in range(2 * N_EXPERTS):
            @pl.when(zn_ref[j] > 0)
            def _():
                zero_block(j).wait()

    @pl.when(i >= 2)
    def _():
        tile_wait(slot)

    base = i * tile
    for r in range(tile):
        src = m_hbm.at[pl.ds(base + r, 1), :]
        _row_copy(src, xs_hbm.at[pl.ds(d1_ref[base + r], 1), :], sems.at[slot]).start()
        _row_copy(src, xs_hbm.at[pl.ds(d2_ref[base + r], 1), :], sems.at[slot]).start()

    @pl.when(i == n_tiles - 1)
    def _():
        if n_tiles > 1:
            tile_wait(1 - slot)
        tile_wait(slot)


def _dispatch(d1, d2, zero_start, zero_n, m, n_slots, tile=BLOCK):
    t, d = m.shape
    n_tiles = t // tile
    kern = functools.partial(_dispatch_kernel, tile=tile, n_tiles=n_tiles)
    grid_spec = pltpu.PrefetchScalarGridSpec(
        num_scalar_prefetch=4,
        grid=(n_tiles,),
        in_specs=[pl.BlockSpec(memory_space=pl.ANY)],
        out_specs=pl.BlockSpec(memory_space=pl.ANY),
        scratch_shapes=[pltpu.VMEM((MOE_BM, d), F32), pltpu.SemaphoreType.DMA((2,)),
                        pltpu.SemaphoreType.DMA(())],
    )
    return pl.pallas_call(
        kern,
        grid_spec=grid_spec,
        out_shape=jax.ShapeDtypeStruct((n_slots, d), F32),
        compiler_params=pltpu.CompilerParams(dimension_semantics=("arbitrary",)),
        name="dispatch",
    )(d1, d2, zero_start, zero_n, m)


def _expert_kernel(be_ref, nu_ref, xs_ref, wg_ref, wu_ref, wd_ref, y_ref):
    i = pl.program_id(0)

    @pl.when(i < nu_ref[0])
    def _():
        xs = xs_ref[...]
        gt = _dot(xs, wg_ref[...])
        hb = gt * jax.nn.sigmoid(gt) * _dot(xs, wu_ref[...])
        y_ref[...] = _dot(hb, wd_ref[...])

    @pl.when(i >= nu_ref[0])
    def _():
        y_ref[...] = jnp.zeros_like(y_ref)


def _experts(block_e, n_used, xs, w_gate, w_up, w_down):
    n_slots, d = xs.shape
    ff = w_gate.shape[-1]
    last = lambda i, be, nu: (jnp.minimum(i, nu[0] - 1), 0)
    grid_spec = pltpu.PrefetchScalarGridSpec(
        num_scalar_prefetch=2,
        grid=(n_slots // MOE_BM,),
        in_specs=[pl.BlockSpec((MOE_BM, d), last),
                  pl.BlockSpec((None, d, ff), lambda i, be, nu: (be[i], 0, 0)),
                  pl.BlockSpec((None, d, ff), lambda i, be, nu: (be[i], 0, 0)),
                  pl.BlockSpec((None, ff, d), lambda i, be, nu: (be[i], 0, 0))],
        out_specs=pl.BlockSpec((MOE_BM, d), lambda i, be, nu: (i, 0)),
    )
    return pl.pallas_call(
        _expert_kernel,
        grid_spec=grid_spec,
        out_shape=jax.ShapeDtypeStruct((n_slots, d), F32),
        compiler_params=pltpu.CompilerParams(
            dimension_semantics=("arbitrary",), vmem_limit_bytes=40 * 1024 * 1024),
        name="experts",
    )(block_e, n_used, xs, w_gate, w_up, w_down)


def _combine_kernel(d1_ref, d2_ref, h2_ref, route_ref, ys_hbm, g_ref, o_ref, ya_ref, yb_ref, sems,
                    *, tile, n_tiles):
    i = pl.program_id(0)
    slot = i % 2

    def issue(step, s):
        base = step * tile
        for r in range(tile):
            _row_copy(ys_hbm.at[pl.ds(d1_ref[base + r], 1), :], ya_ref.at[s, pl.ds(r, 1), :], sems.at[s]).start()
            _row_copy(ys_hbm.at[pl.ds(d2_ref[base + r], 1), :], yb_ref.at[s, pl.ds(r, 1), :], sems.at[s]).start()

    @pl.when(i == 0)
    def _():
        issue(0, 0)

    @pl.when(i + 1 < n_tiles)
    def _():
        issue(i + 1, 1 - slot)

    _row_copy(ys_hbm.at[pl.ds(0, tile), :], ya_ref.at[slot], sems.at[slot]).wait()
    _row_copy(ys_hbm.at[pl.ds(0, tile), :], yb_ref.at[slot], sems.at[slot]).wait()
    rec = route_ref[...]
    g1 = rec[:, R_G1:R_G1 + 1]
    g2 = rec[:, R_G2:R_G2 + 1]
    h = h2_ref[...] + (g1 * ya_ref[slot] + g2 * yb_ref[slot])
    o_ref[...] = _rms(h) * g_ref[...]


def _combine(d1, d2, h2, route, ys, g_final, tile=BLOCK):
    t, d = h2.shape
    n_tiles = t // tile
    kern = functools.partial(_combine_kernel, tile=tile, n_tiles=n_tiles)
    grid_spec = pltpu.PrefetchScalarGridSpec(
        num_scalar_prefetch=2,
        grid=(n_tiles,),
        in_specs=[pl.BlockSpec((tile, d), lambda i, a, b: (i, 0)),
                  pl.BlockSpec((tile, LANES), lambda i, a, b: (i, 0)),
                  pl.BlockSpec(memory_space=pl.ANY),
                  pl.BlockSpec((1, d), lambda i, a, b: (0, 0))],
        out_specs=pl.BlockSpec((tile, d), lambda i, a, b: (i, 0)),
        scratch_shapes=[pltpu.VMEM((2, tile, d), F32), pltpu.VMEM((2, tile, d), F32),
                        pltpu.SemaphoreType.DMA((2,))],
    )
    return pl.pallas_call(
        kern,
        grid_spec=grid_spec,
        out_shape=jax.ShapeDtypeStruct((t, d), F32),
        compiler_params=pltpu.CompilerParams(dimension_semantics=("arbitrary",)),
        name="combine",
    )(d1, d2, h2, route, ys, g_final)


def _segment_layout(route, counts_row, n_tok):
    counts = counts_row[N_GROUPS:N_GROUPS + N_EXPERTS].astype(jnp.int32)
    n_slots = n_tok * TOP_K + N_EXPERTS * MOE_BM
    n_blocks = n_slots // MOE_BM
    padded = (counts + MOE_BM - 1) // MOE_BM * MOE_BM
    padded_end = jnp.cumsum(padded)
    padded_start = padded_end - padded
    e1 = route[:, R_E1].astype(jnp.int32)
    e2 = route[:, R_E2].astype(jnp.int32)
    onehot = jnp.arange(N_EXPERTS, dtype=jnp.int32)[None, :]
    start1 = jnp.sum(jnp.where(e1[:, None] == onehot, padded_start[None, :], 0), axis=1)
    start2 = jnp.sum(jnp.where(e2[:, None] == onehot, padded_start[None, :], 0), axis=1)
    d1 = start1 + route[:, R_RANK1].astype(jnp.int32)
    d2 = start2 + route[:, R_RANK2].astype(jnp.int32)
    block_e = jnp.sum((jnp.arange(n_blocks, dtype=jnp.int32)[:, None] * MOE_BM >= padded_end[None, :])
                      .astype(jnp.int32), axis=1)
    block_e = jnp.minimum(block_e, N_EXPERTS - 1)
    n_used = (padded_end[-1] // MOE_BM).reshape(1)
    trailing = n_used[0] + jnp.arange(N_EXPERTS, dtype=jnp.int32)
    zero_start = jnp.concatenate([jnp.maximum(padded_end - MOE_BM, 0),
                                  jnp.minimum(trailing, n_blocks - 1) * MOE_BM])
    zero_flag = jnp.concatenate([counts, (trailing < n_blocks).astype(jnp.int32)])
    return d1, d2, block_e, n_used, zero_start, zero_flag, n_slots


def kernel(x, meta_tokens, lb_logits, g_mix, w_in, sb_gain, hg_gain, w_out, g_ffn, w_router_group,
           b_router_group, w_router_expert, b_router_expert, w_expert_gate, w_expert_up, w_expert_down,
           g_final):
    b, seq, d = x.shape
    depth = w_in.shape[0]
    assert depth == 1, "single-layer block"
    assert seq % (Q_SUB * BLOCK) == 0
    t = b * seq
    tm = 512
    layer = 0

    xr = x.reshape(t, d)
    prefix = jnp.concatenate([jnp.zeros((LEAD_PAD, d), x.dtype), meta_tokens.astype(x.dtype)], axis=0)
    prefix_mask = (jnp.arange(BLOCK) >= LEAD_PAD).astype(F32)[:, None]
    lower_bounds = jnp.cumsum(jax.nn.softmax(lb_logits.astype(F32), axis=0), axis=0)
    lb = lower_bounds[layer][None, :]
    g_mix_l = g_mix[layer][None, :]
    w_in_l = w_in[layer].astype(BF16)

    q, k0, k1, v0, v1, hq, hk, hv, lf, hg = _proj(xr, jnp.ones((t, 1), F32), g_mix_l, w_in_l, lb, tm)
    _, mk0, mk1, mv0, mv1, _, mhk, mhv, mlf, _ = _proj(prefix, prefix_mask, g_mix_l, w_in_l, lb, BLOCK)

    o_sb = _sb_attention(q, (k0, k1, v0, v1), (mk0, mk1, mv0, mv1), sb_gain[layer][None, :], b, seq)
    o_hg = _hgrn2(hq, hk, hv, lf, hg, (mhk, mhv, mlf), hg_gain[layer][None, :], b, seq)

    w_r = jnp.zeros((d, LANES), F32)
    w_r = w_r.at[:, :N_GROUPS].set(w_router_group[layer])
    w_r = w_r.at[:, N_GROUPS:N_GROUPS + N_EXPERTS].set(w_router_expert[layer])
    b_r = jnp.zeros((1, LANES), F32)
    b_r = b_r.at[0, :N_GROUPS].set(b_router_group[layer])
    b_r = b_r.at[0, N_GROUPS:N_GROUPS + N_EXPERTS].set(b_router_expert[layer])
    h2, m, route, counts = _outproj(o_sb, o_hg, xr, w_out[layer].astype(BF16), g_ffn[layer][None, :],
                                    w_r, b_r, tm)

    d1, d2, block_e, n_used, zero_start, zero_n, n_slots = _segment_layout(route, counts[0], t)
    xs = _dispatch(d1, d2, zero_start, zero_n, m, n_slots)
    ys = _experts(block_e, n_used, xs, w_expert_gate[layer], w_expert_up[layer], w_expert_down[layer])
    out = _combine(d1, d2, h2, route, ys, g_final[None, :])
    return out.reshape(b, seq, d)
```

```python
import functools

import numpy as np
import jax
import jax.numpy as jnp
from jax import lax
from jax.experimental import pallas as pl
from jax.experimental.pallas import tpu as pltpu

BLOCK = 128
N_META = 16
LEAD_PAD = BLOCK - N_META
SB_HEAD_DIM = 64
HG_HEAD_DIM = 128
SUB = 16
Q_SUB = 4
N_GROUPS = 4
EXPERTS_PER_GROUP = 8
N_EXPERTS = N_GROUPS * EXPERTS_PER_GROUP
TOP_K = 2
MOE_BM = 256
EPS = 1e-6
LANES = 128

F32 = jnp.float32
BF16 = jnp.bfloat16


def _dot(a, b):
    return jnp.dot(a, b, preferred_element_type=F32)


def _dot_nt(a, b):
    return lax.dot_general(a, b, (((1,), (1,)), ((), ())), preferred_element_type=F32)


def _dot_tn(a, b):
    return lax.dot_general(a, b, (((0,), (0,)), ((), ())), preferred_element_type=F32)


def _rms(x):
    return x * lax.rsqrt(jnp.mean(x * x, axis=-1, keepdims=True) + EPS)


def _proj_kernel(h_ref, mask_ref, g_ref, w_ref, lb_ref,
                 q_ref, k0_ref, k1_ref, v0_ref, v1_ref,
                 hq_ref, hk_ref, hv_ref, lf_ref, hg_ref, *, width):
    a = (_rms(h_ref[...]) * g_ref[...]).astype(BF16)

    def p(i):
        return _dot(a, w_ref[:, i * width:(i + 1) * width])

    lane = lax.broadcasted_iota(jnp.int32, (1, width), 1)
    head0 = (lane & (LANES - 1)) < SB_HEAD_DIM
    q_ref[...] = (p(0) * (SB_HEAD_DIM ** -0.5)).astype(BF16)
    k = p(1)
    k0_ref[...] = jnp.where(head0, k, 0.0).astype(BF16)
    k1_ref[...] = jnp.where(head0, 0.0, k).astype(BF16)
    v = p(2)
    v0_ref[...] = jnp.where(head0, v, 0.0).astype(BF16)
    v1_ref[...] = jnp.where(head0, 0.0, v).astype(BF16)
    hq = p(3)
    hq_ref[...] = hq * jax.nn.sigmoid(hq)
    f = p(4)
    lb = lb_ref[...]
    sig = jax.nn.sigmoid(f)
    lf_ref[...] = jnp.log(lb + (1.0 - lb) * sig)
    hk_ref[...] = mask_ref[...] * ((1.0 - lb) * jax.nn.sigmoid(-f))
    hv_ref[...] = p(5)
    g = p(6)
    hg_ref[...] = g * jax.nn.sigmoid(g)


def _proj(h, mask, g_mix, w_in, lb, tm):
    t, d = h.shape
    width = d // 2
    kern = functools.partial(_proj_kernel, width=width)
    row = lambda i: (i, 0)
    const = lambda i: (0, 0)
    outs = ([jax.ShapeDtypeStruct((t, width), BF16)] * 5
            + [jax.ShapeDtypeStruct((t, width), F32)] * 5)
    return pl.pallas_call(
        kern,
        grid=(t // tm,),
        in_specs=[pl.BlockSpec((tm, d), row), pl.BlockSpec((tm, 1), row),
                  pl.BlockSpec((1, d), const), pl.BlockSpec((d, 7 * width), const),
                  pl.BlockSpec((1, width), const)],
        out_specs=[pl.BlockSpec((tm, width), row)] * 10,
        out_shape=outs,
        compiler_params=pltpu.CompilerParams(
            dimension_semantics=("arbitrary",), vmem_limit_bytes=52 * 1024 * 1024),
        name="proj",
    )(h, mask, g_mix, w_in, lb)


def _sb_kernel(q_ref, k0_ref, k1_ref, v0_ref, v1_ref, mk0_ref, mk1_ref, mv0_ref, mv1_ref,
               tt_ref, gain_ref, o_ref, c_ref, acc_ref, zz_ref, w_ref):
    tq = Q_SUB * BLOCK
    base = pl.program_id(2) * Q_SUB
    c_ref[...] = jnp.zeros_like(c_ref)
    acc_ref[...] = jnp.zeros_like(acc_ref)
    row = lax.broadcasted_iota(jnp.int32, (tq, 2 * BLOCK), 0)
    col = lax.broadcasted_iota(jnp.int32, (tq, 2 * BLOCK), 1) & (BLOCK - 1)

    def k_real(kb):
        rows = pl.ds(pl.multiple_of(kb * BLOCK, BLOCK), BLOCK)
        return jnp.concatenate([k0_ref[rows, :], k1_ref[rows, :]], axis=0)

    def v_real(kb):
        rows = pl.ds(pl.multiple_of(kb * BLOCK, BLOCK), BLOCK)
        return jnp.concatenate([v0_ref[rows, :], v1_ref[rows, :]], axis=0)

    def k_real_or_prefix(kb):
        k_prefix = jnp.concatenate([mk0_ref[...], mk1_ref[...]], axis=0)
        return jnp.where(kb >= 0, k_real(jnp.maximum(kb, 0)), k_prefix)

    def step(mask, k_next, v_prev):
        zz = zz_ref[...]
        if k_next is not None:
            zz_ref[...] = _dot_nt(q_ref[...], k_next)
        if v_prev is not None:
            acc_ref[...] += _dot(w_ref[...], v_prev)
        sp = jnp.maximum(zz, 0.0) + jnp.log(1.0 + jnp.exp(-jnp.abs(zz)))
        spm = sp if mask is None else jnp.where(mask, sp, 0.0)
        hi = spm.astype(BF16)
        lo = (spm - hi.astype(F32)).astype(BF16)
        tt = tt_ref[...]
        tail = jnp.concatenate(
            [_dot(jnp.concatenate([hi[:, h * BLOCK:(h + 1) * BLOCK], lo[:, h * BLOCK:(h + 1) * BLOCK]], axis=1), tt)
             for h in range(2)], axis=1)
        c = c_ref[...]
        w = jnp.exp(zz - sp + tail + c)
        if mask is not None:
            w = jnp.where(mask, w, 0.0)
        w_ref[...] = w.astype(BF16)
        tot0 = jnp.sum(spm[:, :BLOCK], axis=-1, keepdims=True)
        tot1 = jnp.sum(spm[:, BLOCK:], axis=-1, keepdims=True)
        c_ref[...] = c - jnp.concatenate([jnp.broadcast_to(tot0, (tq, BLOCK)),
                                          jnp.broadcast_to(tot1, (tq, BLOCK))], axis=1)

    top = base + Q_SUB - 1
    zz_ref[...] = _dot_nt(q_ref[...], k_real(top))
    for jj in reversed(range(Q_SUB)):
        kb = base + jj
        step(col < row - jj * BLOCK,
             k_real(kb - 1) if jj > 0 else k_real_or_prefix(kb - 1),
             v_real(kb + 1) if jj < Q_SUB - 1 else None)

    def body(it, carry):
        kb = base - 1 - it
        step(None, k_real_or_prefix(kb - 1), v_real(kb + 1))
        return carry

    lax.fori_loop(0, base, body, 0)

    step(col >= LEAD_PAD, None, v_real(0))
    acc_ref[...] += _dot(w_ref[...], jnp.concatenate([mv0_ref[...], mv1_ref[...]], axis=0))

    head0 = lax.broadcasted_iota(jnp.int32, (tq, LANES), 1) < SB_HEAD_DIM
    o = acc_ref[...]
    o2 = o * o
    s0 = jnp.sum(jnp.where(head0, o2, 0.0), axis=-1, keepdims=True) * (1.0 / SB_HEAD_DIM)
    s1 = jnp.sum(jnp.where(head0, 0.0, o2), axis=-1, keepdims=True) * (1.0 / SB_HEAD_DIM)
    r = jnp.where(head0, lax.rsqrt(s0 + EPS), lax.rsqrt(s1 + EPS))
    o_ref[...] = (o * r * gain_ref[...]).astype(o_ref.dtype)


def _sb_tail_matrix():
    j = np.arange(BLOCK)[:, None]
    s = np.arange(BLOCK)[None, :]
    tri = np.where(j > s, -1.0, 0.0)
    return jnp.asarray(np.concatenate([tri, tri], axis=0), dtype=BF16)


def _sb_attention(q, kv, kv_meta, gain, b, seq):
    t, width = q.shape
    nqt = seq // (Q_SUB * BLOCK)
    npair = width // LANES
    kv3 = [a.reshape(b, seq, width) for a in kv]
    tq = Q_SUB * BLOCK
    kv_spec = pl.BlockSpec((None, seq, LANES), lambda bi, hp, qt: (bi, 0, hp))
    meta_spec = pl.BlockSpec((BLOCK, LANES), lambda bi, hp, qt: (0, hp))
    return pl.pallas_call(
        _sb_kernel,
        grid=(b, npair, nqt),
        in_specs=[pl.BlockSpec((tq, LANES), lambda bi, hp, qt: (bi * nqt + qt, hp)),
                  kv_spec, kv_spec, kv_spec, kv_spec,
                  meta_spec, meta_spec, meta_spec, meta_spec,
                  pl.BlockSpec((2 * BLOCK, BLOCK), lambda bi, hp, qt: (0, 0)),
                  pl.BlockSpec((1, LANES), lambda bi, hp, qt: (0, hp))],
        out_specs=pl.BlockSpec((tq, LANES), lambda bi, hp, qt: (bi * nqt + qt, hp)),
        out_shape=jax.ShapeDtypeStruct((t, width), BF16),
        scratch_shapes=[pltpu.VMEM((tq, 2 * BLOCK), F32), pltpu.VMEM((tq, LANES), F32),
                        pltpu.VMEM((tq, 2 * BLOCK), F32), pltpu.VMEM((tq, 2 * BLOCK), BF16)],
        compiler_params=pltpu.CompilerParams(
            dimension_semantics=("arbitrary", "arbitrary", "arbitrary"),
            vmem_limit_bytes=40 * 1024 * 1024),
        name="sb_attn",
    )(q, *kv3, *kv_meta, _sb_tail_matrix(), gain)


def _hg_block(hq_ref, hk_ref, hv_ref, lf_ref, lmat_ref, st_ref, a_ref, qt_ref, kh_ref, oacc_ref, n_heads,
              with_output):
    lf = lf_ref[...]
    h1 = lf.astype(BF16)
    r1 = lf - h1.astype(F32)
    h2 = r1.astype(BF16)
    h3 = (r1 - h2.astype(F32)).astype(BF16)
    lmat = lmat_ref[...]
    cs = _dot(lmat, h1) + _dot(lmat, h2) + _dot(lmat, h3)
    a = cs[:BLOCK]
    alast = cs[BLOCK:]
    a_ref[...] = a
    kh_ref[...] = hk_ref[...] * jnp.exp(alast - a)
    if with_output:
        qt_ref[...] = hq_ref[...] * jnp.exp(a)
    ridx = lax.broadcasted_iota(jnp.int32, (SUB, 1), 0)

    def sub_body(i, carry):
        r0 = pl.multiple_of(i * SUB, SUB)
        rows = pl.ds(r0, SUB)
        for hd in range(n_heads):
            cols = slice(hd * HG_HEAD_DIM, (hd + 1) * HG_HEAD_DIM)
            a_i = a_ref[rows, cols]
            v_i = hv_ref[rows, cols]
            st = st_ref[hd]
            if with_output:
                q_i = hq_ref[rows, cols]
                k_i = hk_ref[rows, cols]
                o_i = _dot_nt(qt_ref[rows, cols], st)
                for s in range(SUB):
                    dec = jnp.exp(jnp.minimum(a_i - a_i[s:s + 1, :], 0.0))
                    sc = jnp.sum(q_i * dec * k_i[s:s + 1, :], axis=-1, keepdims=True)
                    sc = jnp.where(ridx >= s, sc, 0.0)
                    o_i = o_i + sc * v_i[s:s + 1, :]
                oacc_ref[rows, cols] = o_i
            decay = jnp.exp(a_i[SUB - 1:SUB, :])
            st_ref[hd] = st * decay + _dot_tn(v_i, kh_ref[rows, cols])
        return carry

    lax.fori_loop(0, BLOCK // SUB, sub_body, 0)


def _hg_kernel(hq_ref, hk_ref, hv_ref, lf_ref, gate_ref, mk_ref, mv_ref, mlf_ref, gain_ref, lmat_ref, o_ref,
               st_ref, a_ref, qt_ref, kh_ref, oacc_ref, *, n_heads):
    scratch = (st_ref, a_ref, qt_ref, kh_ref, oacc_ref)

    @pl.when(pl.program_id(1) == 0)
    def _():
        st_ref[...] = jnp.zeros_like(st_ref)
        _hg_block(None, mk_ref, mv_ref, mlf_ref, lmat_ref, *scratch, n_heads, with_output=False)

    _hg_block(hq_ref, hk_ref, hv_ref, lf_ref, lmat_ref, *scratch, n_heads, with_output=True)
    for hd in range(n_heads):
        cols = slice(hd * HG_HEAD_DIM, (hd + 1) * HG_HEAD_DIM)
        o = _rms(oacc_ref[:, cols]) * gain_ref[:, cols] * gate_ref[:, cols]
        o_ref[:, cols] = o.astype(o_ref.dtype)


def _hg_cumsum_matrix():
    t = np.arange(BLOCK)[:, None]
    s = np.arange(BLOCK)[None, :]
    same = (t // SUB) == (s // SUB)
    incl = np.where(same & (s <= t), 1.0, 0.0)
    full = np.where(same, 1.0, 0.0)
    return jnp.asarray(np.concatenate([incl, full], axis=0), dtype=BF16)


def _hgrn2(hq, hk, hv, lf, gate, meta, gain, b, seq):
    t, width = hq.shape
    nc = seq // BLOCK
    n_heads = width // HG_HEAD_DIM
    blk = pl.BlockSpec((BLOCK, width), lambda bi, ci: (bi * nc + ci, 0))
    mblk = pl.BlockSpec((BLOCK, width), lambda bi, ci: (0, 0))
    kern = functools.partial(_hg_kernel, n_heads=n_heads)
    return pl.pallas_call(
        kern,
        grid=(b, nc),
        in_specs=[blk, blk, blk, blk, blk, mblk, mblk, mblk,
                  pl.BlockSpec((1, width), lambda bi, ci: (0, 0)),
                  pl.BlockSpec((2 * BLOCK, BLOCK), lambda bi, ci: (0, 0))],
        out_specs=blk,
        out_shape=jax.ShapeDtypeStruct((t, width), BF16),
        scratch_shapes=[pltpu.VMEM((n_heads, HG_HEAD_DIM, HG_HEAD_DIM), F32),
                        pltpu.VMEM((BLOCK, width), F32), pltpu.VMEM((BLOCK, width), F32),
                        pltpu.VMEM((BLOCK, width), F32), pltpu.VMEM((BLOCK, width), F32)],
        compiler_params=pltpu.CompilerParams(dimension_semantics=("arbitrary", "arbitrary")),
        name="hgrn2",
    )(hq, hk, hv, lf, gate, *meta, gain, _hg_cumsum_matrix())


R_E1, R_E2, R_RANK1, R_RANK2, R_G1, R_G2 = range(6)


def _outproj_kernel(osb_ref, ohg_ref, h_ref, w_ref, g_ref, wr_ref, br_ref, tri_ref,
                    h2_ref, m_ref, route_ref, cnt_ref, carry_ref, *, width):
    @pl.when(pl.program_id(0) == 0)
    def _():
        carry_ref[...] = jnp.zeros_like(carry_ref)

    h2 = h_ref[...] + _dot(osb_ref[...], w_ref[:width, :]) + _dot(ohg_ref[...], w_ref[width:, :])
    h2_ref[...] = h2
    m = _rms(h2) * g_ref[...]
    m_ref[...] = m
    lg = _dot(m, wr_ref[...]) + br_ref[...]
    tm = lg.shape[0]
    lane = lax.broadcasted_iota(jnp.int32, (tm, LANES), 1)
    neg = jnp.float32(-1e30)

    def first_argmax(vals):
        vmax = jnp.max(vals, axis=-1, keepdims=True)
        idx = jnp.min(jnp.where(vals == vmax, lane, LANES), axis=-1, keepdims=True)
        return vmax, idx

    is_grp = lane < N_GROUPS
    gl = jnp.where(is_grp, lg, neg)
    gmax, gidx = first_argmax(gl)
    p_grp = 1.0 / jnp.sum(jnp.where(is_grp, jnp.exp(gl - gmax), 0.0), axis=-1, keepdims=True)
    lo = N_GROUPS + gidx * EXPERTS_PER_GROUP
    el = jnp.where((lane >= lo) & (lane < lo + EXPERTS_PER_GROUP), lg, neg)
    v1, i1 = first_argmax(el)
    sel1 = lane == i1
    v2, i2 = first_argmax(jnp.where(sel1, neg, el))
    sel2 = lane == i2
    dlt = jnp.exp(v2 - v1)
    g1 = p_grp / (1.0 + dlt)
    g2 = g1 * dlt

    chosen = jnp.where(sel1 | sel2, 1.0, 0.0)
    carry = carry_ref[0:1, :]
    before = _dot(tri_ref[...], chosen.astype(BF16)) + carry
    r1 = jnp.sum(jnp.where(sel1, before, 0.0), axis=-1, keepdims=True)
    r2 = jnp.sum(jnp.where(sel2, before, 0.0), axis=-1, keepdims=True)
    carry = carry + jnp.sum(chosen, axis=0, keepdims=True)
    carry_ref[0:1, :] = carry
    cnt_ref[...] = jnp.broadcast_to(carry, cnt_ref.shape)

    rec = jnp.zeros((tm, LANES), F32)
    for ln, val in ((R_E1, (i1 - N_GROUPS).astype(F32)), (R_E2, (i2 - N_GROUPS).astype(F32)),
                    (R_RANK1, r1), (R_RANK2, r2), (R_G1, g1), (R_G2, g2)):
        rec = jnp.where(lane == ln, val, rec)
    route_ref[...] = rec


def _outproj(o_sb, o_hg, h, w_out, g_ffn, w_r, b_r, tm):
    t, d = h.shape
    width = o_sb.shape[1]
    row = lambda i: (i, 0)
    const = lambda i: (0, 0)
    tri = jnp.asarray(np.tril(np.ones((tm, tm), np.float32), -1), dtype=BF16)
    kern = functools.partial(_outproj_kernel, width=width)
    return pl.pallas_call(
        kern,
        grid=(t // tm,),
        in_specs=[pl.BlockSpec((tm, width), row), pl.BlockSpec((tm, width), row),
                  pl.BlockSpec((tm, d), row), pl.BlockSpec((2 * width, d), const),
                  pl.BlockSpec((1, d), const), pl.BlockSpec((d, LANES), const),
                  pl.BlockSpec((1, LANES), const), pl.BlockSpec((tm, tm), const)],
        out_specs=[pl.BlockSpec((tm, d), row), pl.BlockSpec((tm, d), row),
                   pl.BlockSpec((tm, LANES), row), pl.BlockSpec((8, LANES), const)],
        out_shape=[jax.ShapeDtypeStruct((t, d), F32), jax.ShapeDtypeStruct((t, d), F32),
                   jax.ShapeDtypeStruct((t, LANES), F32), jax.ShapeDtypeStruct((8, LANES), F32)],
        scratch_shapes=[pltpu.VMEM((8, LANES), F32)],
        compiler_params=pltpu.CompilerParams(
            dimension_semantics=("arbitrary",), vmem_limit_bytes=40 * 1024 * 1024),
        name="outproj",
    )(o_sb, o_hg, h, w_out, g_ffn, w_r, b_r, tri)


def _row_copy(src, dst, sem):
    return pltpu.make_async_copy(src, dst, sem)


def _dispatch_kernel(d1_ref, d2_ref, zs_ref, zn_ref, m_hbm, xs_hbm, zero_ref, stage_ref, sems, lsems, zsem,
                     *, tile, n_tiles):
    i = pl.program_id(0)
    slot = i % 3

    def tile_wait(s):
        _row_copy(m_hbm.at[pl.ds(0, 2 * tile), :], xs_hbm.at[pl.ds(0, 2 * tile), :], sems.at[s]).wait()

    def tile_load(step, s):
        rows = pl.ds(pl.multiple_of(step * tile, tile), tile)
        return _row_copy(m_hbm.at[rows, :], stage_ref.at[s], lsems.at[s])

    @pl.when(i == 0)
    def _():
        tile_load(0, 0).start()
        zero_ref[...] = jnp.zeros_like(zero_ref)

        def zero_block(j):
            return _row_copy(zero_ref, xs_hbm.at[pl.ds(pl.multiple_of(zs_ref[j], MOE_BM), MOE_BM), :], zsem)

        for j in range(2 * N_EXPERTS):
            @pl.when(zn_ref[j] > 0)
            def _():
                zero_block(j).start()
        for j in range(2 * N_EXPERTS):
            @pl.when(zn_ref[j] > 0)
            def _():
                zero_block(j).wait()

    nxt = (i + 1) % 3

    @pl.when(i >= 2)
    def _():
        tile_wait(nxt)

    @pl.when(i + 1 < n_tiles)
    def _():
        tile_load(i + 1, nxt).start()

    tile_load(i, slot).wait()
    base = i * tile
    for r in range(tile):
        src = stage_ref.at[slot, pl.ds(r, 1), :]
        _row_copy(src, xs_hbm.at[pl.ds(d1_ref[base + r], 1), :], sems.at[slot]).start()
        _row_copy(src, xs_hbm.at[pl.ds(d2_ref[base + r], 1), :], sems.at[slot]).start()

    @pl.when(i == n_tiles - 1)
    def _():
        if n_tiles > 1:
            tile_wait((i + 2) % 3)
        tile_wait(slot)


def _dispatch(d1, d2, zero_start, zero_n, m, n_slots, tile=BLOCK):
    t, d = m.shape
    n_tiles = t // tile
    kern = functools.partial(_dispatch_kernel, tile=tile, n_tiles=n_tiles)
    grid_spec = pltpu.PrefetchScalarGridSpec(
        num_scalar_prefetch=4,
        grid=(n_tiles,),
        in_specs=[pl.BlockSpec(memory_space=pl.ANY)],
        out_specs=pl.BlockSpec(memory_space=pl.ANY),
        scratch_shapes=[pltpu.VMEM((MOE_BM, d), F32), pltpu.VMEM((3, tile, d), F32),
                        pltpu.SemaphoreType.DMA((3,)), pltpu.SemaphoreType.DMA((3,)),
                        pltpu.SemaphoreType.DMA(())],
    )
    return pl.pallas_call(
        kern,
        grid_spec=grid_spec,
        out_shape=jax.ShapeDtypeStruct((n_slots, d), F32),
        compiler_params=pltpu.CompilerParams(dimension_semantics=("arbitrary",)),
        name="dispatch",
    )(d1, d2, zero_start, zero_n, m)


def _expert_kernel(be_ref, nu_ref, xs_ref, wg_ref, wu_ref, wd_ref, y_ref):
    i = pl.program_id(0)

    @pl.when(i < nu_ref[0])
    def _():
        xs = xs_ref[...]
        gt = _dot(xs, wg_ref[...])
        hb = gt * jax.nn.sigmoid(gt) * _dot(xs, wu_ref[...])
        y_ref[...] = _dot(hb, wd_ref[...])

    @pl.when(i >= nu_ref[0])
    def _():
        y_ref[...] = jnp.zeros_like(y_ref)


def _experts(block_e, n_used, xs, w_gate, w_up, w_down):
    n_slots, d = xs.shape
    ff = w_gate.shape[-1]
    last = lambda i, be, nu: (jnp.minimum(i, nu[0] - 1), 0)
    grid_spec = pltpu.PrefetchScalarGridSpec(
        num_scalar_prefetch=2,
        grid=(n_slots // MOE_BM,),
        in_specs=[pl.BlockSpec((MOE_BM, d), last),
                  pl.BlockSpec((None, d, ff), lambda i, be, nu: (be[i], 0, 0)),
                  pl.BlockSpec((None, d, ff), lambda i, be, nu: (be[i], 0, 0)),
                  pl.BlockSpec((None, ff, d), lambda i, be, nu: (be[i], 0, 0))],
        out_specs=pl.BlockSpec((MOE_BM, d), lambda i, be, nu: (i, 0)),
    )
    return pl.pallas_call(
        _expert_kernel,
        grid_spec=grid_spec,
        out_shape=jax.ShapeDtypeStruct((n_slots, d), F32),
        compiler_params=pltpu.CompilerParams(
            dimension_semantics=("arbitrary",), vmem_limit_bytes=40 * 1024 * 1024),
        name="experts",
    )(block_e, n_used, xs, w_gate, w_up, w_down)


def _combine_kernel(d1_ref, d2_ref, h2_ref, route_ref, ys_hbm, g_ref, o_ref, ya_ref, yb_ref, sems,
                    *, tile, n_tiles):
    i = pl.program_id(0)
    slot = i % 2

    def issue(step, s):
        base = step * tile
        for r in range(tile):
            _row_copy(ys_hbm.at[pl.ds(d1_ref[base + r], 1), :], ya_ref.at[s, pl.ds(r, 1), :], sems.at[s]).start()
            _row_copy(ys_hbm.at[pl.ds(d2_ref[base + r], 1), :], yb_ref.at[s, pl.ds(r, 1), :], sems.at[s]).start()

    @pl.when(i == 0)
    def _():
        issue(0, 0)

    @pl.when(i + 1 < n_tiles)
    def _():
        issue(i + 1, 1 - slot)

    _row_copy(ys_hbm.at[pl.ds(0, tile), :], ya_ref.at[slot], sems.at[slot]).wait()
    _row_copy(ys_hbm.at[pl.ds(0, tile), :], yb_ref.at[slot], sems.at[slot]).wait()
    rec = route_ref[...]
    g1 = rec[:, R_G1:R_G1 + 1]
    g2 = rec[:, R_G2:R_G2 + 1]
    h = h2_ref[...] + (g1 * ya_ref[slot] + g2 * yb_ref[slot])
    o_ref[...] = _rms(h) * g_ref[...]


def _combine(d1, d2, h2, route, ys, g_final, tile=BLOCK):
    t, d = h2.shape
    n_tiles = t // tile
    kern = functools.partial(_combine_kernel, tile=tile, n_tiles=n_tiles)
    grid_spec = pltpu.PrefetchScalarGridSpec(
        num_scalar_prefetch=2,
        grid=(n_tiles,),
        in_specs=[pl.BlockSpec((tile, d), lambda i, a, b: (i, 0)),
                  pl.BlockSpec((tile, LANES), lambda i, a, b: (i, 0)),
                  pl.BlockSpec(memory_space=pl.ANY),
                  pl.BlockSpec((1, d), lambda i, a, b: (0, 0))],
        out_specs=pl.BlockSpec((tile, d), lambda i, a, b: (i, 0)),
        scratch_shapes=[pltpu.VMEM((2, tile, d), F32), pltpu.VMEM((2, tile, d), F32),
                        pltpu.SemaphoreType.DMA((2,))],
    )
    return pl.pallas_call(
        kern,
        grid_spec=grid_spec,
        out_shape=jax.ShapeDtypeStruct((t, d), F32),
        compiler_params=pltpu.CompilerParams(dimension_semantics=("arbitrary",)),
        name="combine",
    )(d1, d2, h2, route, ys, g_final)


def _segment_layout(route, counts_row, n_tok):
    counts = counts_row[N_GROUPS:N_GROUPS + N_EXPERTS].astype(jnp.int32)
    n_slots = n_tok * TOP_K + N_EXPERTS * MOE_BM
    n_blocks = n_slots // MOE_BM
    padded = (counts + MOE_BM - 1) // MOE_BM * MOE_BM
    padded_end = jnp.cumsum(padded)
    padded_start = padded_end - padded
    e1 = route[:, R_E1].astype(jnp.int32)
    e2 = route[:, R_E2].astype(jnp.int32)
    onehot = jnp.arange(N_EXPERTS, dtype=jnp.int32)[None, :]
    start1 = jnp.sum(jnp.where(e1[:, None] == onehot, padded_start[None, :], 0), axis=1)
    start2 = jnp.sum(jnp.where(e2[:, None] == onehot, padded_start[None, :], 0), axis=1)
    d1 = start1 + route[:, R_RANK1].astype(jnp.int32)
    d2 = start2 + route[:, R_RANK2].astype(jnp.int32)
    block_e = jnp.sum((jnp.arange(n_blocks, dtype=jnp.int32)[:, None] * MOE_BM >= padded_end[None, :])
                      .astype(jnp.int32), axis=1)
    block_e = jnp.minimum(block_e, N_EXPERTS - 1)
    n_used = (padded_end[-1] // MOE_BM).reshape(1)
    trailing = n_used[0] + jnp.arange(N_EXPERTS, dtype=jnp.int32)
    zero_start = jnp.concatenate([jnp.maximum(padded_end - MOE_BM, 0),
                                  jnp.minimum(trailing, n_blocks - 1) * MOE_BM])
    zero_flag = jnp.concatenate([counts, (trailing < n_blocks).astype(jnp.int32)])
    return d1, d2, block_e, n_used, zero_start, zero_flag, n_slots


def kernel(x, meta_tokens, lb_logits, g_mix, w_in, sb_gain, hg_gain, w_out, g_ffn, w_router_group,
           b_router_group, w_router_expert, b_router_expert, w_expert_gate, w_expert_up, w_expert_down,
           g_final):
    b, seq, d = x.shape
    depth = w_in.shape[0]
    assert depth == 1, "single-layer block"
    assert seq % (Q_SUB * BLOCK) == 0
    t = b * seq
    tm = 512
    layer = 0

    xr = x.reshape(t, d)
    prefix = jnp.concatenate([jnp.zeros((LEAD_PAD, d), x.dtype), meta_tokens.astype(x.dtype)], axis=0)
    prefix_mask = (jnp.arange(BLOCK) >= LEAD_PAD).astype(F32)[:, None]
    lower_bounds = jnp.cumsum(jax.nn.softmax(lb_logits.astype(F32), axis=0), axis=0)
    lb = lower_bounds[layer][None, :]
    g_mix_l = g_mix[layer][None, :]
    w_in_l = w_in[layer].astype(BF16)

    q, k0, k1, v0, v1, hq, hk, hv, lf, hg = _proj(xr, jnp.ones((t, 1), F32), g_mix_l, w_in_l, lb, tm)
    _, mk0, mk1, mv0, mv1, _, mhk, mhv, mlf, _ = _proj(prefix, prefix_mask, g_mix_l, w_in_l, lb, BLOCK)

    o_sb = _sb_attention(q, (k0, k1, v0, v1), (mk0, mk1, mv0, mv1), sb_gain[layer][None, :], b, seq)
    o_hg = _hgrn2(hq, hk, hv, lf, hg, (mhk, mhv, mlf), hg_gain[layer][None, :], b, seq)

    w_r = jnp.zeros((d, LANES), F32)
    w_r = w_r.at[:, :N_GROUPS].set(w_router_group[layer])
    w_r = w_r.at[:, N_GROUPS:N_GROUPS + N_EXPERTS].set(w_router_expert[layer])
    b_r = jnp.zeros((1, LANES), F32)
    b_r = b_r.at[0, :N_GROUPS].set(b_router_group[layer])
    b_r = b_r.at[0, N_GROUPS:N_GROUPS + N_EXPERTS].set(b_router_expert[layer])
    h2, m, route, counts = _outproj(o_sb, o_hg, xr, w_out[layer].astype(BF16), g_ffn[layer][None, :],
                                    w_r, b_r, tm)

    d1, d2, block_e, n_used, zero_start, zero_n, n_slots = _segment_layout(route, counts[0], t)
    xs = _dispatch(d1, d2, zero_start, zero_n, m, n_slots)
    ys = _experts(block_e, n_used, xs, w_expert_gate[layer], w_expert_up[layer], w_expert_down[layer])
    out = _combine(d1, d2, h2, route, ys, g_final[None, :])
    return out.reshape(b, seq, d)
```

```python
import functools

import numpy as np
import jax
import jax.numpy as jnp
from jax import lax
from jax.experimental import pallas as pl
from jax.experimental.pallas import tpu as pltpu

BLOCK = 128
N_META = 16
LEAD_PAD = BLOCK - N_META
SB_HEAD_DIM = 64
HG_HEAD_DIM = 128
SUB = 16
Q_SUB = 4
DEAD_LOG_TAIL = -104.0
N_GROUPS = 4
EXPERTS_PER_GROUP = 8
N_EXPERTS = N_GROUPS * EXPERTS_PER_GROUP
TOP_K = 2
MOE_BM = 256
EPS = 1e-6
LANES = 128

F32 = jnp.float32
BF16 = jnp.bfloat16


def _dot(a, b):
    return jnp.dot(a, b, preferred_element_type=F32)


def _dot_nt(a, b):
    return lax.dot_general(a, b, (((1,), (1,)), ((), ())), preferred_element_type=F32)


def _dot_tn(a, b):
    return lax.dot_general(a, b, (((0,), (0,)), ((), ())), preferred_element_type=F32)


def _rms(x):
    return x * lax.rsqrt(jnp.mean(x * x, axis=-1, keepdims=True) + EPS)


def _proj_kernel(h_ref, mask_ref, g_ref, w_ref, lb_ref,
                 q_ref, k0_ref, k1_ref, v0_ref, v1_ref,
                 hq_ref, hk_ref, hv_ref, lf_ref, hg_ref, *, width):
    a = (_rms(h_ref[...]) * g_ref[...]).astype(BF16)

    def p(i):
        return _dot(a, w_ref[:, i * width:(i + 1) * width])

    lane = lax.broadcasted_iota(jnp.int32, (1, width), 1)
    head0 = (lane & (LANES - 1)) < SB_HEAD_DIM
    q_ref[...] = (p(0) * (SB_HEAD_DIM ** -0.5)).astype(BF16)
    k = p(1)
    k0_ref[...] = jnp.where(head0, k, 0.0).astype(BF16)
    k1_ref[...] = jnp.where(head0, 0.0, k).astype(BF16)
    v = p(2)
    v0_ref[...] = jnp.where(head0, v, 0.0).astype(BF16)
    v1_ref[...] = jnp.where(head0, 0.0, v).astype(BF16)
    hq = p(3)
    hq_ref[...] = hq * jax.nn.sigmoid(hq)
    f = p(4)
    lb = lb_ref[...]
    sig = jax.nn.sigmoid(f)
    lf_ref[...] = jnp.log(lb + (1.0 - lb) * sig)
    hk_ref[...] = mask_ref[...] * ((1.0 - lb) * jax.nn.sigmoid(-f))
    hv_ref[...] = p(5)
    g = p(6)
    hg_ref[...] = g * jax.nn.sigmoid(g)


def _proj(h, mask, g_mix, w_in, lb, tm):
    t, d = h.shape
    width = d // 2
    kern = functools.partial(_proj_kernel, width=width)
    row = lambda i: (i, 0)
    const = lambda i: (0, 0)
    outs = ([jax.ShapeDtypeStruct((t, width), BF16)] * 5
            + [jax.ShapeDtypeStruct((t, width), F32)] * 5)
    return pl.pallas_call(
        kern,
        grid=(t // tm,),
        in_specs=[pl.BlockSpec((tm, d), row), pl.BlockSpec((tm, 1), row),
                  pl.BlockSpec((1, d), const), pl.BlockSpec((d, 7 * width), const),
                  pl.BlockSpec((1, width), const)],
        out_specs=[pl.BlockSpec((tm, width), row)] * 10,
        out_shape=outs,
        compiler_params=pltpu.CompilerParams(
            dimension_semantics=("arbitrary",), vmem_limit_bytes=52 * 1024 * 1024),
        name="proj",
    )(h, mask, g_mix, w_in, lb)


def _sb_kernel(q_ref, k0_ref, k1_ref, v0_ref, v1_ref, mk0_ref, mk1_ref, mv0_ref, mv1_ref,
               tt_ref, gain_ref, o_ref, c_ref, acc_ref, zz_ref, w_ref):
    tq = Q_SUB * BLOCK
    base = pl.program_id(2) * Q_SUB
    c_ref[...] = jnp.zeros_like(c_ref)
    acc_ref[...] = jnp.zeros_like(acc_ref)
    row = lax.broadcasted_iota(jnp.int32, (tq, 2 * BLOCK), 0)
    col = lax.broadcasted_iota(jnp.int32, (tq, 2 * BLOCK), 1) & (BLOCK - 1)

    def k_real(kb):
        rows = pl.ds(pl.multiple_of(kb * BLOCK, BLOCK), BLOCK)
        return jnp.concatenate([k0_ref[rows, :], k1_ref[rows, :]], axis=0)

    def v_real(kb):
        rows = pl.ds(pl.multiple_of(kb * BLOCK, BLOCK), BLOCK)
        return jnp.concatenate([v0_ref[rows, :], v1_ref[rows, :]], axis=0)

    def k_real_or_prefix(kb):
        k_prefix = jnp.concatenate([mk0_ref[...], mk1_ref[...]], axis=0)
        return jnp.where(kb >= 0, k_real(jnp.maximum(kb, 0)), k_prefix)

    def step(mask, k_next, v_prev):
        zz = zz_ref[...]
        if k_next is not None:
            zz_ref[...] = _dot_nt(q_ref[...], k_next)
        if v_prev is not None:
            acc_ref[...] += _dot(w_ref[...], v_prev)
        sp = jnp.maximum(zz, 0.0) + jnp.log(1.0 + jnp.exp(-jnp.abs(zz)))
        spm = sp if mask is None else jnp.where(mask, sp, 0.0)
        hi = spm.astype(BF16)
        lo = (spm - hi.astype(F32)).astype(BF16)
        tt = tt_ref[...]
        tail = jnp.concatenate(
            [_dot(jnp.concatenate([hi[:, h * BLOCK:(h + 1) * BLOCK], lo[:, h * BLOCK:(h + 1) * BLOCK]], axis=1), tt)
             for h in range(2)], axis=1)
        c = c_ref[...]
        w = jnp.exp(zz - sp + tail + c)
        if mask is not None:
            w = jnp.where(mask, w, 0.0)
        w_ref[...] = w.astype(BF16)
        tot0 = jnp.sum(spm[:, :BLOCK], axis=-1, keepdims=True)
        tot1 = jnp.sum(spm[:, BLOCK:], axis=-1, keepdims=True)
        c_ref[...] = c - jnp.concatenate([jnp.broadcast_to(tot0, (tq, BLOCK)),
                                          jnp.broadcast_to(tot1, (tq, BLOCK))], axis=1)

    top = base + Q_SUB - 1
    zz_ref[...] = _dot_nt(q_ref[...], k_real(top))
    for jj in reversed(range(Q_SUB)):
        kb = base + jj
        step(col < row - jj * BLOCK,
             k_real(kb - 1) if jj > 0 else k_real_or_prefix(kb - 1),
             v_real(kb + 1) if jj < Q_SUB - 1 else None)

    def alive():
        return jnp.max(c_ref[...]) > DEAD_LOG_TAIL

    def cond(carry):
        kb, live = carry
        return jnp.logical_and(kb >= 0, live)

    def body(carry):
        kb, _ = carry
        step(None, k_real_or_prefix(kb - 1), v_real(kb + 1))
        return kb - 1, alive()

    kb, live = lax.while_loop(cond, body, (base - 1, alive()))
    acc_ref[...] += _dot(w_ref[...], v_real(kb + 1))

    @pl.when(jnp.logical_and(kb < 0, live))
    def _():
        step(col >= LEAD_PAD, None, None)
        acc_ref[...] += _dot(w_ref[...], jnp.concatenate([mv0_ref[...], mv1_ref[...]], axis=0))

    head0 = lax.broadcasted_iota(jnp.int32, (tq, LANES), 1) < SB_HEAD_DIM
    o = acc_ref[...]
    o2 = o * o
    s0 = jnp.sum(jnp.where(head0, o2, 0.0), axis=-1, keepdims=True) * (1.0 / SB_HEAD_DIM)
    s1 = jnp.sum(jnp.where(head0, 0.0, o2), axis=-1, keepdims=True) * (1.0 / SB_HEAD_DIM)
    r = jnp.where(head0, lax.rsqrt(s0 + EPS), lax.rsqrt(s1 + EPS))
    o_ref[...] = (o * r * gain_ref[...]).astype(o_ref.dtype)


def _sb_tail_matrix():
    j = np.arange(BLOCK)[:, None]
    s = np.arange(BLOCK)[None, :]
    tri = np.where(j > s, -1.0, 0.0)
    return jnp.asarray(np.concatenate([tri, tri], axis=0), dtype=BF16)


def _sb_attention(q, kv, kv_meta, gain, b, seq):
    t, width = q.shape
    nqt = seq // (Q_SUB * BLOCK)
    npair = width // LANES
    kv3 = [a.reshape(b, seq, width) for a in kv]
    tq = Q_SUB * BLOCK
    kv_spec = pl.BlockSpec((None, seq, LANES), lambda bi, hp, qt: (bi, 0, hp))
    meta_spec = pl.BlockSpec((BLOCK, LANES), lambda bi, hp, qt: (0, hp))
    return pl.pallas_call(
        _sb_kernel,
        grid=(b, npair, nqt),
        in_specs=[pl.BlockSpec((tq, LANES), lambda bi, hp, qt: (bi * nqt + qt, hp)),
                  kv_spec, kv_spec, kv_spec, kv_spec,
                  meta_spec, meta_spec, meta_spec, meta_spec,
                  pl.BlockSpec((2 * BLOCK, BLOCK), lambda bi, hp, qt: (0, 0)),
                  pl.BlockSpec((1, LANES), lambda bi, hp, qt: (0, hp))],
        out_specs=pl.BlockSpec((tq, LANES), lambda bi, hp, qt: (bi * nqt + qt, hp)),
        out_shape=jax.ShapeDtypeStruct((t, width), BF16),
        scratch_shapes=[pltpu.VMEM((tq, 2 * BLOCK), F32), pltpu.VMEM((tq, LANES), F32),
                        pltpu.VMEM((tq, 2 * BLOCK), F32), pltpu.VMEM((tq, 2 * BLOCK), BF16)],
        compiler_params=pltpu.CompilerParams(
            dimension_semantics=("arbitrary", "arbitrary", "arbitrary"),
            vmem_limit_bytes=40 * 1024 * 1024),
        name="sb_attn",
    )(q, *kv3, *kv_meta, _sb_tail_matrix(), gain)


def _hg_block(hq_ref, hk_ref, hv_ref, lf_ref, lmat_ref, st_ref, a_ref, qt_ref, kh_ref, oacc_ref, n_heads,
              with_output):
    lf = lf_ref[...]
    h1 = lf.astype(BF16)
    r1 = lf - h1.astype(F32)
    h2 = r1.astype(BF16)
    h3 = (r1 - h2.astype(F32)).astype(BF16)
    lmat = lmat_ref[...]
    cs = _dot(lmat, h1) + _dot(lmat, h2) + _dot(lmat, h3)
    a = cs[:BLOCK]
    alast = cs[BLOCK:]
    a_ref[...] = a
    kh_ref[...] = hk_ref[...] * jnp.exp(alast - a)
    if with_output:
        qt_ref[...] = hq_ref[...] * jnp.exp(a)
    ridx = lax.broadcasted_iota(jnp.int32, (SUB, 1), 0)

    def sub_body(i, carry):
        r0 = pl.multiple_of(i * SUB, SUB)
        rows = pl.ds(r0, SUB)
        for hd in range(n_heads):
            cols = slice(hd * HG_HEAD_DIM, (hd + 1) * HG_HEAD_DIM)
            a_i = a_ref[rows, cols]
            v_i = hv_ref[rows, cols]
            st = st_ref[hd]
            if with_output:
                q_i = hq_ref[rows, cols]
                k_i = hk_ref[rows, cols]
                o_i = _dot_nt(qt_ref[rows, cols], st)
                for s in range(SUB):
                    dec = jnp.exp(jnp.minimum(a_i - a_i[s:s + 1, :], 0.0))
                    sc = jnp.sum(q_i * dec * k_i[s:s + 1, :], axis=-1, keepdims=True)
                    sc = jnp.where(ridx >= s, sc, 0.0)
                    o_i = o_i + sc * v_i[s:s + 1, :]
                oacc_ref[rows, cols] = o_i
            decay = jnp.exp(a_i[SUB - 1:SUB, :])
            st_ref[hd] = st * decay + _dot_tn(v_i, kh_ref[rows, cols])
        return carry

    lax.fori_loop(0, BLOCK // SUB, sub_body, 0)


def _hg_kernel(hq_ref, hk_ref, hv_ref, lf_ref, gate_ref, mk_ref, mv_ref, mlf_ref, gain_ref, lmat_ref, o_ref,
               st_ref, a_ref, qt_ref, kh_ref, oacc_ref, *, n_heads):
    scratch = (st_ref, a_ref, qt_ref, kh_ref, oacc_ref)

    @pl.when(pl.program_id(1) == 0)
    def _():
        st_ref[...] = jnp.zeros_like(st_ref)
        _hg_block(None, mk_ref, mv_ref, mlf_ref, lmat_ref, *scratch, n_heads, with_output=False)

    _hg_block(hq_ref, hk_ref, hv_ref, lf_ref, lmat_ref, *scratch, n_heads, with_output=True)
    for hd in range(n_heads):
        cols = slice(hd * HG_HEAD_DIM, (hd + 1) * HG_HEAD_DIM)
        o = _rms(oacc_ref[:, cols]) * gain_ref[:, cols] * gate_ref[:, cols]
        o_ref[:, cols] = o.astype(o_ref.dtype)


def _hg_cumsum_matrix():
    t = np.arange(BLOCK)[:, None]
    s = np.arange(BLOCK)[None, :]
    same = (t // SUB) == (s // SUB)
    incl = np.where(same & (s <= t), 1.0, 0.0)
    full = np.where(same, 1.0, 0.0)
    return jnp.asarray(np.concatenate([incl, full], axis=0), dtype=BF16)


def _hgrn2(hq, hk, hv, lf, gate, meta, gain, b, seq):
    t, width = hq.shape
    nc = seq // BLOCK
    n_heads = width // HG_HEAD_DIM
    blk = pl.BlockSpec((BLOCK, width), lambda bi, ci: (bi * nc + ci, 0))
    mblk = pl.BlockSpec((BLOCK, width), lambda bi, ci: (0, 0))
    kern = functools.partial(_hg_kernel, n_heads=n_heads)
    return pl.pallas_call(
        kern,
        grid=(b, nc),
        in_specs=[blk, blk, blk, blk, blk, mblk, mblk, mblk,
                  pl.BlockSpec((1, width), lambda bi, ci: (0, 0)),
                  pl.BlockSpec((2 * BLOCK, BLOCK), lambda bi, ci: (0, 0))],
        out_specs=blk,
        out_shape=jax.ShapeDtypeStruct((t, width), BF16),
        scratch_shapes=[pltpu.VMEM((n_heads, HG_HEAD_DIM, HG_HEAD_DIM), F32),
                        pltpu.VMEM((BLOCK, width), F32), pltpu.VMEM((BLOCK, width), F32),
                        pltpu.VMEM((BLOCK, width), F32), pltpu.VMEM((BLOCK, width), F32)],
        compiler_params=pltpu.CompilerParams(dimension_semantics=("arbitrary", "arbitrary")),
        name="hgrn2",
    )(hq, hk, hv, lf, gate, *meta, gain, _hg_cumsum_matrix())


R_E1, R_E2, R_RANK1, R_RANK2, R_G1, R_G2 = range(6)


def _outproj_kernel(osb_ref, ohg_ref, h_ref, w_ref, g_ref, wr_ref, br_ref, tri_ref,
                    h2_ref, m_ref, route_ref, cnt_ref, carry_ref, *, width):
    @pl.when(pl.program_id(0) == 0)
    def _():
        carry_ref[...] = jnp.zeros_like(carry_ref)

    h2 = h_ref[...] + _dot(osb_ref[...], w_ref[:width, :]) + _dot(ohg_ref[...], w_ref[width:, :])
    h2_ref[...] = h2
    m = _rms(h2) * g_ref[...]
    m_ref[...] = m
    lg = _dot(m, wr_ref[...]) + br_ref[...]
    tm = lg.shape[0]
    lane = lax.broadcasted_iota(jnp.int32, (tm, LANES), 1)
    neg = jnp.float32(-1e30)

    def first_argmax(vals):
        vmax = jnp.max(vals, axis=-1, keepdims=True)
        idx = jnp.min(jnp.where(vals == vmax, lane, LANES), axis=-1, keepdims=True)
        return vmax, idx

    is_grp = lane < N_GROUPS
    gl = jnp.where(is_grp, lg, neg)
    gmax, gidx = first_argmax(gl)
    p_grp = 1.0 / jnp.sum(jnp.where(is_grp, jnp.exp(gl - gmax), 0.0), axis=-1, keepdims=True)
    lo = N_GROUPS + gidx * EXPERTS_PER_GROUP
    el = jnp.where((lane >= lo) & (lane < lo + EXPERTS_PER_GROUP), lg, neg)
    v1, i1 = first_argmax(el)
    sel1 = lane == i1
    v2, i2 = first_argmax(jnp.where(sel1, neg, el))
    sel2 = lane == i2
    dlt = jnp.exp(v2 - v1)
    g1 = p_grp / (1.0 + dlt)
    g2 = g1 * dlt

    chosen = jnp.where(sel1 | sel2, 1.0, 0.0)
    carry = carry_ref[0:1, :]
    before = _dot(tri_ref[...], chosen.astype(BF16)) + carry
    r1 = jnp.sum(jnp.where(sel1, before, 0.0), axis=-1, keepdims=True)
    r2 = jnp.sum(jnp.where(sel2, before, 0.0), axis=-1, keepdims=True)
    carry = carry + jnp.sum(chosen, axis=0, keepdims=True)
    carry_ref[0:1, :] = carry
    cnt_ref[...] = jnp.broadcast_to(carry, cnt_ref.shape)

    rec = jnp.zeros((tm, LANES), F32)
    for ln, val in ((R_E1, (i1 - N_GROUPS).astype(F32)), (R_E2, (i2 - N_GROUPS).astype(F32)),
                    (R_RANK1, r1), (R_RANK2, r2), (R_G1, g1), (R_G2, g2)):
        rec = jnp.where(lane == ln, val, rec)
    route_ref[...] = rec


def _outproj(o_sb, o_hg, h, w_out, g_ffn, w_r, b_r, tm):
    t, d = h.shape
    width = o_sb.shape[1]
    row = lambda i: (i, 0)
    const = lambda i: (0, 0)
    tri = jnp.asarray(np.tril(np.ones((tm, tm), np.float32), -1), dtype=BF16)
    kern = functools.partial(_outproj_kernel, width=width)
    return pl.pallas_call(
        kern,
        grid=(t // tm,),
        in_specs=[pl.BlockSpec((tm, width), row), pl.BlockSpec((tm, width), row),
                  pl.BlockSpec((tm, d), row), pl.BlockSpec((2 * width, d), const),
                  pl.BlockSpec((1, d), const), pl.BlockSpec((d, LANES), const),
                  pl.BlockSpec((1, LANES), const), pl.BlockSpec((tm, tm), const)],
        out_specs=[pl.BlockSpec((tm, d), row), pl.BlockSpec((tm, d), row),
                   pl.BlockSpec((tm, LANES), row), pl.BlockSpec((8, LANES), const)],
        out_shape=[jax.ShapeDtypeStruct((t, d), F32), jax.ShapeDtypeStruct((t, d), F32),
                   jax.ShapeDtypeStruct((t, LANES), F32), jax.ShapeDtypeStruct((8, LANES), F32)],
        scratch_shapes=[pltpu.VMEM((8, LANES), F32)],
        compiler_params=pltpu.CompilerParams(
            dimension_semantics=("arbitrary",), vmem_limit_bytes=40 * 1024 * 1024),
        name="outproj",
    )(o_sb, o_hg, h, w_out, g_ffn, w_r, b_r, tri)


def _row_copy(src, dst, sem):
    return pltpu.make_async_copy(src, dst, sem)


def _dispatch_kernel(d1_ref, d2_ref, zs_ref, zn_ref, m_hbm, xs_hbm, zero_ref, stage_ref, sems, lsems, zsem,
                     *, tile, n_tiles):
    i = pl.program_id(0)
    slot = i % 3

    def tile_wait(s):
        _row_copy(m_hbm.at[pl.ds(0, 2 * tile), :], xs_hbm.at[pl.ds(0, 2 * tile), :], sems.at[s]).wait()

    def tile_load(step, s):
        rows = pl.ds(pl.multiple_of(step * tile, tile), tile)
        return _row_copy(m_hbm.at[rows, :], stage_ref.at[s], lsems.at[s])

    @pl.when(i == 0)
    def _():
        tile_load(0, 0).start()
        zero_ref[...] = jnp.zeros_like(zero_ref)

        def zero_block(j):
            return _row_copy(zero_ref, xs_hbm.at[pl.ds(pl.multiple_of(zs_ref[j], MOE_BM), MOE_BM), :], zsem)

        for j in range(2 * N_EXPERTS):
            @pl.when(zn_ref[j] > 0)
            def _():
                zero_block(j).start()
        for j in range(2 * N_EXPERTS):
            @pl.when(zn_ref[j] > 0)
            def _():
                zero_block(j).wait()

    nxt = (i + 1) % 3

    @pl.when(i >= 2)
    def _():
        tile_wait(nxt)

    @pl.when(i + 1 < n_tiles)
    def _():
        tile_load(i + 1, nxt).start()

    tile_load(i, slot).wait()
    base = i * tile
    for r in range(tile):
        src = stage_ref.at[slot, pl.ds(r, 1), :]
        _row_copy(src, xs_hbm.at[pl.ds(d1_ref[base + r], 1), :], sems.at[slot]).start()
        _row_copy(src, xs_hbm.at[pl.ds(d2_ref[base + r], 1), :], sems.at[slot]).start()

    @pl.when(i == n_tiles - 1)
    def _():
        if n_tiles > 1:
            tile_wait((i + 2) % 3)
        tile_wait(slot)


def _dispatch(d1, d2, zero_start, zero_n, m, n_slots, tile=BLOCK):
    t, d = m.shape
    n_tiles = t // tile
    kern = functools.partial(_dispatch_kernel, tile=tile, n_tiles=n_tiles)
    grid_spec = pltpu.PrefetchScalarGridSpec(
        num_scalar_prefetch=4,
        grid=(n_tiles,),
        in_specs=[pl.BlockSpec(memory_space=pl.ANY)],
        out_specs=pl.BlockSpec(memory_space=pl.ANY),
        scratch_shapes=[pltpu.VMEM((MOE_BM, d), F32), pltpu.VMEM((3, tile, d), F32),
                        pltpu.SemaphoreType.DMA((3,)), pltpu.SemaphoreType.DMA((3,)),
                        pltpu.SemaphoreType.DMA(())],
    )
    return pl.pallas_call(
        kern,
        grid_spec=grid_spec,
        out_shape=jax.ShapeDtypeStruct((n_slots, d), F32),
        compiler_params=pltpu.CompilerParams(dimension_semantics=("arbitrary",)),
        name="dispatch",
    )(d1, d2, zero_start, zero_n, m)


def _expert_kernel(be_ref, nu_ref, xs_ref, wg_ref, wu_ref, wd_ref, y_ref):
    i = pl.program_id(0)

    @pl.when(i < nu_ref[0])
    def _():
        xs = xs_ref[...]
        gt = _dot(xs, wg_ref[...])
        hb = gt * jax.nn.sigmoid(gt) * _dot(xs, wu_ref[...])
        y_ref[...] = _dot(hb, wd_ref[...])

    @pl.when(i >= nu_ref[0])
    def _():
        y_ref[...] = jnp.zeros_like(y_ref)


def _experts(block_e, n_used, xs, w_gate, w_up, w_down):
    n_slots, d = xs.shape
    ff = w_gate.shape[-1]
    last = lambda i, be, nu: (jnp.minimum(i, nu[0] - 1), 0)
    grid_spec = pltpu.PrefetchScalarGridSpec(
        num_scalar_prefetch=2,
        grid=(n_slots // MOE_BM,),
        in_specs=[pl.BlockSpec((MOE_BM, d), last),
                  pl.BlockSpec((None, d, ff), lambda i, be, nu: (be[i], 0, 0)),
                  pl.BlockSpec((None, d, ff), lambda i, be, nu: (be[i], 0, 0)),
                  pl.BlockSpec((None, ff, d), lambda i, be, nu: (be[i], 0, 0))],
        out_specs=pl.BlockSpec((MOE_BM, d), lambda i, be, nu: (i, 0)),
    )
    return pl.pallas_call(
        _expert_kernel,
        grid_spec=grid_spec,
        out_shape=jax.ShapeDtypeStruct((n_slots, d), F32),
        compiler_params=pltpu.CompilerParams(
            dimension_semantics=("arbitrary",), vmem_limit_bytes=40 * 1024 * 1024),
        name="experts",
    )(block_e, n_used, xs, w_gate, w_up, w_down)


def _combine_kernel(d1_ref, d2_ref, h2_ref, route_ref, ys_hbm, g_ref, o_ref, ya_ref, yb_ref, sems,
                    *, tile, n_tiles):
    i = pl.program_id(0)
    slot = i % 2

    def issue(step, s):
        base = step * tile
        for r in range(tile):
            _row_copy(ys_hbm.at[pl.ds(d1_ref[base + r], 1), :], ya_ref.at[s, pl.ds(r, 1), :], sems.at[s]).start()
            _row_copy(ys_hbm.at[pl.ds(d2_ref[base + r], 1), :], yb_ref.at[s, pl.ds(r, 1), :], sems.at[s]).start()

    @pl.when(i == 0)
    def _():
        issue(0, 0)

    @pl.when(i + 1 < n_tiles)
    def _():
        issue(i + 1, 1 - slot)

    _row_copy(ys_hbm.at[pl.ds(0, tile), :], ya_ref.at[slot], sems.at[slot]).wait()
    _row_copy(ys_hbm.at[pl.ds(0, tile), :], yb_ref.at[slot], sems.at[slot]).wait()
    rec = route_ref[...]
    g1 = rec[:, R_G1:R_G1 + 1]
    g2 = rec[:, R_G2:R_G2 + 1]
    h = h2_ref[...] + (g1 * ya_ref[slot] + g2 * yb_ref[slot])
    o_ref[...] = _rms(h) * g_ref[...]


def _combine(d1, d2, h2, route, ys, g_final, tile=BLOCK):
    t, d = h2.shape
    n_tiles = t // tile
    kern = functools.partial(_combine_kernel, tile=tile, n_tiles=n_tiles)
    grid_spec = pltpu.PrefetchScalarGridSpec(
        num_scalar_prefetch=2,
        grid=(n_tiles,),
        in_specs=[pl.BlockSpec((tile, d), lambda i, a, b: (i, 0)),
                  pl.BlockSpec((tile, LANES), lambda i, a, b: (i, 0)),
                  pl.BlockSpec(memory_space=pl.ANY),
                  pl.BlockSpec((1, d), lambda i, a, b: (0, 0))],
        out_specs=pl.BlockSpec((tile, d), lambda i, a, b: (i, 0)),
        scratch_shapes=[pltpu.VMEM((2, tile, d), F32), pltpu.VMEM((2, tile, d), F32),
                        pltpu.SemaphoreType.DMA((2,))],
    )
    return pl.pallas_call(
        kern,
        grid_spec=grid_spec,
        out_shape=jax.ShapeDtypeStruct((t, d), F32),
        compiler_params=pltpu.CompilerParams(dimension_semantics=("arbitrary",)),
        name="combine",
    )(d1, d2, h2, route, ys, g_final)


def _segment_layout(route, counts_row, n_tok):
    counts = counts_row[N_GROUPS:N_GROUPS + N_EXPERTS].astype(jnp.int32)
    n_slots = n_tok * TOP_K + N_EXPERTS * MOE_BM
    n_blocks = n_slots // MOE_BM
    padded = (counts + MOE_BM - 1) // MOE_BM * MOE_BM
    padded_end = jnp.cumsum(padded)
    padded_start = padded_end - padded
    e1 = route[:, R_E1].astype(jnp.int32)
    e2 = route[:, R_E2].astype(jnp.int32)
    onehot = jnp.arange(N_EXPERTS, dtype=jnp.int32)[None, :]
    start1 = jnp.sum(jnp.where(e1[:, None] == onehot, padded_start[None, :], 0), axis=1)
    start2 = jnp.sum(jnp.where(e2[:, None] == onehot, padded_start[None, :], 0), axis=1)
    d1 = start1 + route[:, R_RANK1].astype(jnp.int32)
    d2 = start2 + route[:, R_RANK2].astype(jnp.int32)
    block_e = jnp.sum((jnp.arange(n_blocks, dtype=jnp.int32)[:, None] * MOE_BM >= padded_end[None, :])
                      .astype(jnp.int32), axis=1)
    block_e = jnp.minimum(block_e, N_EXPERTS - 1)
    n_used = (padded_end[-1] // MOE_BM).reshape(1)
    trailing = n_used[0] + jnp.arange(N_EXPERTS, dtype=jnp.int32)
    zero_start = jnp.concatenate([jnp.maximum(padded_end - MOE_BM, 0),
                                  jnp.minimum(trailing, n_blocks - 1) * MOE_BM])
    zero_flag = jnp.concatenate([counts, (trailing < n_blocks).astype(jnp.int32)])
    return d1, d2, block_e, n_used, zero_start, zero_flag, n_slots


def kernel(x, meta_tokens, lb_logits, g_mix, w_in, sb_gain, hg_gain, w_out, g_ffn, w_router_group,
           b_router_group, w_router_expert, b_router_expert, w_expert_gate, w_expert_up, w_expert_down,
           g_final):
    b, seq, d = x.shape
    depth = w_in.shape[0]
    assert depth == 1, "single-layer block"
    assert seq % (Q_SUB * BLOCK) == 0
    t = b * seq
    tm = 512
    layer = 0

    xr = x.reshape(t, d)
    prefix = jnp.concatenate([jnp.zeros((LEAD_PAD, d), x.dtype), meta_tokens.astype(x.dtype)], axis=0)
    prefix_mask = (jnp.arange(BLOCK) >= LEAD_PAD).astype(F32)[:, None]
    lower_bounds = jnp.cumsum(jax.nn.softmax(lb_logits.astype(F32), axis=0), axis=0)
    lb = lower_bounds[layer][None, :]
    g_mix_l = g_mix[layer][None, :]
    w_in_l = w_in[layer].astype(BF16)

    q, k0, k1, v0, v1, hq, hk, hv, lf, hg = _proj(xr, jnp.ones((t, 1), F32), g_mix_l, w_in_l, lb, tm)
    _, mk0, mk1, mv0, mv1, _, mhk, mhv, mlf, _ = _proj(prefix, prefix_mask, g_mix_l, w_in_l, lb, BLOCK)

    o_sb = _sb_attention(q, (k0, k1, v0, v1), (mk0, mk1, mv0, mv1), sb_gain[layer][None, :], b, seq)
    o_hg = _hgrn2(hq, hk, hv, lf, hg, (mhk, mhv, mlf), hg_gain[layer][None, :], b, seq)

    w_r = jnp.zeros((d, LANES), F32)
    w_r = w_r.at[:, :N_GROUPS].set(w_router_group[layer])
    w_r = w_r.at[:, N_GROUPS:N_GROUPS + N_EXPERTS].set(w_router_expert[layer])
    b_r = jnp.zeros((1, LANES), F32)
    b_r = b_r.at[0, :N_GROUPS].set(b_router_group[layer])
    b_r = b_r.at[0, N_GROUPS:N_GROUPS + N_EXPERTS].set(b_router_expert[layer])
    h2, m, route, counts = _outproj(o_sb, o_hg, xr, w_out[layer].astype(BF16), g_ffn[layer][None, :],
                                    w_r, b_r, tm)

    d1, d2, block_e, n_used, zero_start, zero_n, n_slots = _segment_layout(route, counts[0], t)
    xs = _dispatch(d1, d2, zero_start, zero_n, m, n_slots)
    ys = _experts(block_e, n_used, xs, w_expert_gate[layer], w_expert_up[layer], w_expert_down[layer])
    out = _combine(d1, d2, h2, route, ys, g_final[None, :])
    return out.reshape(b, seq, d)
```

```python
import functools

import numpy as np
import jax
import jax.numpy as jnp
from jax import lax
from jax.experimental import pallas as pl
from jax.experimental.pallas import tpu as pltpu

BLOCK = 128
N_META = 16
LEAD_PAD = BLOCK - N_META
SB_HEAD_DIM = 64
HG_HEAD_DIM = 128
SUB = 16
Q_SUB = 4
DEAD_LOG_TAIL = -104.0
N_GROUPS = 4
EXPERTS_PER_GROUP = 8
N_EXPERTS = N_GROUPS * EXPERTS_PER_GROUP
TOP_K = 2
MOE_BM = 256
EPS = 1e-6
LANES = 128

F32 = jnp.float32
BF16 = jnp.bfloat16


def _dot(a, b):
    return jnp.dot(a, b, preferred_element_type=F32)


def _dot_nt(a, b):
    return lax.dot_general(a, b, (((1,), (1,)), ((), ())), preferred_element_type=F32)


def _dot_tn(a, b):
    return lax.dot_general(a, b, (((0,), (0,)), ((), ())), preferred_element_type=F32)


def _rms(x):
    return x * lax.rsqrt(jnp.mean(x * x, axis=-1, keepdims=True) + EPS)


def _proj_kernel(h_ref, mask_ref, g_ref, w_ref, lb_ref,
                 q_ref, k0_ref, k1_ref, v0_ref, v1_ref,
                 hq_ref, hk_ref, hv_ref, lf_ref, hg_ref, *, width):
    a = (_rms(h_ref[...]) * g_ref[...]).astype(BF16)

    def p(i):
        return _dot(a, w_ref[:, i * width:(i + 1) * width])

    lane = lax.broadcasted_iota(jnp.int32, (1, width), 1)
    head0 = (lane & (LANES - 1)) < SB_HEAD_DIM
    q_ref[...] = (p(0) * (SB_HEAD_DIM ** -0.5)).astype(BF16)
    k = p(1)
    k0_ref[...] = jnp.where(head0, k, 0.0).astype(BF16)
    k1_ref[...] = jnp.where(head0, 0.0, k).astype(BF16)
    v = p(2)
    v0_ref[...] = jnp.where(head0, v, 0.0).astype(BF16)
    v1_ref[...] = jnp.where(head0, 0.0, v).astype(BF16)
    hq = p(3)
    hq_ref[...] = hq * jax.nn.sigmoid(hq)
    f = p(4)
    lb = lb_ref[...]
    sig = jax.nn.sigmoid(f)
    lf_ref[...] = jnp.log(lb + (1.0 - lb) * sig)
    hk_ref[...] = mask_ref[...] * ((1.0 - lb) * jax.nn.sigmoid(-f))
    hv_ref[...] = p(5)
    g = p(6)
    hg_ref[...] = g * jax.nn.sigmoid(g)


def _proj(h, mask, g_mix, w_in, lb, tm):
    t, d = h.shape
    width = d // 2
    kern = functools.partial(_proj_kernel, width=width)
    row = lambda i: (i, 0)
    const = lambda i: (0, 0)
    outs = ([jax.ShapeDtypeStruct((t, width), BF16)] * 5
            + [jax.ShapeDtypeStruct((t, width), F32)] * 5)
    return pl.pallas_call(
        kern,
        grid=(t // tm,),
        in_specs=[pl.BlockSpec((tm, d), row), pl.BlockSpec((tm, 1), row),
                  pl.BlockSpec((1, d), const), pl.BlockSpec((d, 7 * width), const),
                  pl.BlockSpec((1, width), const)],
        out_specs=[pl.BlockSpec((tm, width), row)] * 10,
        out_shape=outs,
        compiler_params=pltpu.CompilerParams(
            dimension_semantics=("arbitrary",), vmem_limit_bytes=52 * 1024 * 1024),
        name="proj",
    )(h, mask, g_mix, w_in, lb)


def _sb_kernel(q_ref, k0_ref, k1_ref, v0_ref, v1_ref, mk0_ref, mk1_ref, mv0_ref, mv1_ref,
               tt_ref, gain_ref, o_ref, c_ref, acc_ref, zz_ref, w_ref):
    tq = Q_SUB * BLOCK
    base = pl.program_id(2) * Q_SUB
    c_ref[...] = jnp.zeros_like(c_ref)
    acc_ref[...] = jnp.zeros_like(acc_ref)
    row = lax.broadcasted_iota(jnp.int32, (tq, 2 * BLOCK), 0)
    col = lax.broadcasted_iota(jnp.int32, (tq, 2 * BLOCK), 1) & (BLOCK - 1)

    def k_real(kb):
        rows = pl.ds(pl.multiple_of(kb * BLOCK, BLOCK), BLOCK)
        return jnp.concatenate([k0_ref[rows, :], k1_ref[rows, :]], axis=0)

    def v_real(kb):
        rows = pl.ds(pl.multiple_of(kb * BLOCK, BLOCK), BLOCK)
        return jnp.concatenate([v0_ref[rows, :], v1_ref[rows, :]], axis=0)

    def k_real_or_prefix(kb):
        k_prefix = jnp.concatenate([mk0_ref[...], mk1_ref[...]], axis=0)
        return jnp.where(kb >= 0, k_real(jnp.maximum(kb, 0)), k_prefix)

    def step(mask, k_next, v_prev):
        zz = zz_ref[...]
        if k_next is not None:
            zz_ref[...] = _dot_nt(q_ref[...], k_next)
        if v_prev is not None:
            acc_ref[...] += _dot(w_ref[...], v_prev)
        sp = jnp.maximum(zz, 0.0) + jnp.log(1.0 + jnp.exp(-jnp.abs(zz)))
        spm = sp if mask is None else jnp.where(mask, sp, 0.0)
        hi = spm.astype(BF16)
        lo = (spm - hi.astype(F32)).astype(BF16)
        tt = tt_ref[...]
        tail = jnp.concatenate(
            [_dot(jnp.concatenate([hi[:, h * BLOCK:(h + 1) * BLOCK], lo[:, h * BLOCK:(h + 1) * BLOCK]], axis=1), tt)
             for h in range(2)], axis=1)
        c = c_ref[...]
        w = jnp.exp(zz - sp + tail + c)
        if mask is not None:
            w = jnp.where(mask, w, 0.0)
        w_ref[...] = w.astype(BF16)
        tot0 = jnp.sum(spm[:, :BLOCK], axis=-1, keepdims=True)
        tot1 = jnp.sum(spm[:, BLOCK:], axis=-1, keepdims=True)
        c_ref[...] = c - jnp.concatenate([jnp.broadcast_to(tot0, (tq, BLOCK)),
                                          jnp.broadcast_to(tot1, (tq, BLOCK))], axis=1)

    top = base + Q_SUB - 1
    zz_ref[...] = _dot_nt(q_ref[...], k_real(top))
    for jj in reversed(range(Q_SUB)):
        kb = base + jj
        step(col < row - jj * BLOCK,
             k_real(kb - 1) if jj > 0 else k_real_or_prefix(kb - 1),
             v_real(kb + 1) if jj < Q_SUB - 1 else None)

    def alive():
        return jnp.max(c_ref[...]) > DEAD_LOG_TAIL

    def cond(carry):
        kb, live = carry
        return jnp.logical_and(kb >= 0, live)

    def body(carry):
        kb, _ = carry
        step(None, k_real_or_prefix(kb - 1), v_real(kb + 1))
        return kb - 1, alive()

    kb, live = lax.while_loop(cond, body, (base - 1, alive()))
    acc_ref[...] += _dot(w_ref[...], v_real(kb + 1))

    @pl.when(jnp.logical_and(kb < 0, live))
    def _():
        step(col >= LEAD_PAD, None, None)
        acc_ref[...] += _dot(w_ref[...], jnp.concatenate([mv0_ref[...], mv1_ref[...]], axis=0))

    head0 = lax.broadcasted_iota(jnp.int32, (tq, LANES), 1) < SB_HEAD_DIM
    o = acc_ref[...]
    o2 = o * o
    s0 = jnp.sum(jnp.where(head0, o2, 0.0), axis=-1, keepdims=True) * (1.0 / SB_HEAD_DIM)
    s1 = jnp.sum(jnp.where(head0, 0.0, o2), axis=-1, keepdims=True) * (1.0 / SB_HEAD_DIM)
    r = jnp.where(head0, lax.rsqrt(s0 + EPS), lax.rsqrt(s1 + EPS))
    o_ref[...] = (o * r * gain_ref[...]).astype(o_ref.dtype)


def _sb_tail_matrix():
    j = np.arange(BLOCK)[:, None]
    s = np.arange(BLOCK)[None, :]
    tri = np.where(j > s, -1.0, 0.0)
    return jnp.asarray(np.concatenate([tri, tri], axis=0), dtype=BF16)


def _sb_attention(q, kv, kv_meta, gain, b, seq):
    t, width = q.shape
    nqt = seq // (Q_SUB * BLOCK)
    npair = width // LANES
    kv3 = [a.reshape(b, seq, width) for a in kv]
    tq = Q_SUB * BLOCK
    kv_spec = pl.BlockSpec((None, seq, LANES), lambda bi, hp, qt: (bi, 0, hp))
    meta_spec = pl.BlockSpec((BLOCK, LANES), lambda bi, hp, qt: (0, hp))
    return pl.pallas_call(
        _sb_kernel,
        grid=(b, npair, nqt),
        in_specs=[pl.BlockSpec((tq, LANES), lambda bi, hp, qt: (bi * nqt + qt, hp)),
                  kv_spec, kv_spec, kv_spec, kv_spec,
                  meta_spec, meta_spec, meta_spec, meta_spec,
                  pl.BlockSpec((2 * BLOCK, BLOCK), lambda bi, hp, qt: (0, 0)),
                  pl.BlockSpec((1, LANES), lambda bi, hp, qt: (0, hp))],
        out_specs=pl.BlockSpec((tq, LANES), lambda bi, hp, qt: (bi * nqt + qt, hp)),
        out_shape=jax.ShapeDtypeStruct((t, width), BF16),
        scratch_shapes=[pltpu.VMEM((tq, 2 * BLOCK), F32), pltpu.VMEM((tq, LANES), F32),
                        pltpu.VMEM((tq, 2 * BLOCK), F32), pltpu.VMEM((tq, 2 * BLOCK), BF16)],
        compiler_params=pltpu.CompilerParams(
            dimension_semantics=("arbitrary", "arbitrary", "arbitrary"),
            vmem_limit_bytes=40 * 1024 * 1024),
        name="sb_attn",
    )(q, *kv3, *kv_meta, _sb_tail_matrix(), gain)


def _hg_block(hq_ref, hk_ref, hv_ref, lf_ref, lmat_ref, st_ref, a_ref, qt_ref, kh_ref, oacc_ref, f_ref, n_heads,
              with_output):
    lf = lf_ref[...]
    h1 = lf.astype(BF16)
    r1 = lf - h1.astype(F32)
    h2 = r1.astype(BF16)
    h3 = (r1 - h2.astype(F32)).astype(BF16)
    lmat = lmat_ref[...]
    cs = _dot(lmat, h1) + _dot(lmat, h2) + _dot(lmat, h3)
    a = cs[:BLOCK]
    alast = cs[BLOCK:]
    a_ref[...] = alast
    kh_ref[...] = hk_ref[...] * jnp.exp(alast - a)
    if with_output:
        f_ref[...] = jnp.exp(lf)
        qt_ref[...] = hq_ref[...] * jnp.exp(a)
    ridx = lax.broadcasted_iota(jnp.int32, (SUB, 1), 0)

    for i in range(BLOCK // SUB):
        r0 = i * SUB
        rows = pl.ds(r0, SUB)
        for hd in range(n_heads):
            cols = slice(hd * HG_HEAD_DIM, (hd + 1) * HG_HEAD_DIM)
            v_i = hv_ref[rows, cols]
            st = st_ref[hd]
            if with_output:
                o_i = _dot_nt(qt_ref[rows, cols], st)
                e = hq_ref[rows, cols]
                for s in reversed(range(SUB)):
                    if s < SUB - 1:
                        e = jnp.where(ridx > s, e * f_ref[pl.ds(r0 + s + 1, 1), cols], e)
                    sc = jnp.sum(e * hk_ref[pl.ds(r0 + s, 1), cols], axis=-1, keepdims=True)
                    o_i = o_i + jnp.where(ridx >= s, sc, 0.0) * hv_ref[pl.ds(r0 + s, 1), cols]
                oacc_ref[rows, cols] = o_i
            decay = jnp.exp(a_ref[pl.ds(r0, 1), cols])
            st_ref[hd] = st * decay + _dot_tn(v_i, kh_ref[rows, cols])


def _hg_kernel(hq_ref, hk_ref, hv_ref, lf_ref, gate_ref, mk_ref, mv_ref, mlf_ref, gain_ref, lmat_ref, o_ref,
               st_ref, a_ref, qt_ref, kh_ref, oacc_ref, f_ref, *, n_heads):
    scratch = (st_ref, a_ref, qt_ref, kh_ref, oacc_ref, f_ref)

    @pl.when(pl.program_id(1) == 0)
    def _():
        st_ref[...] = jnp.zeros_like(st_ref)
        _hg_block(None, mk_ref, mv_ref, mlf_ref, lmat_ref, *scratch, n_heads, with_output=False)

    _hg_block(hq_ref, hk_ref, hv_ref, lf_ref, lmat_ref, *scratch, n_heads, with_output=True)
    for hd in range(n_heads):
        cols = slice(hd * HG_HEAD_DIM, (hd + 1) * HG_HEAD_DIM)
        o = _rms(oacc_ref[:, cols]) * gain_ref[:, cols] * gate_ref[:, cols]
        o_ref[:, cols] = o.astype(o_ref.dtype)


def _hg_cumsum_matrix():
    t = np.arange(BLOCK)[:, None]
    s = np.arange(BLOCK)[None, :]
    same = (t // SUB) == (s // SUB)
    incl = np.where(same & (s <= t), 1.0, 0.0)
    full = np.where(same, 1.0, 0.0)
    return jnp.asarray(np.concatenate([incl, full], axis=0), dtype=BF16)


def _hgrn2(hq, hk, hv, lf, gate, meta, gain, b, seq):
    t, width = hq.shape
    nc = seq // BLOCK
    n_heads = width // HG_HEAD_DIM
    blk = pl.BlockSpec((BLOCK, width), lambda bi, ci: (bi * nc + ci, 0))
    mblk = pl.BlockSpec((BLOCK, width), lambda bi, ci: (0, 0))
    kern = functools.partial(_hg_kernel, n_heads=n_heads)
    return pl.pallas_call(
        kern,
        grid=(b, nc),
        in_specs=[blk, blk, blk, blk, blk, mblk, mblk, mblk,
                  pl.BlockSpec((1, width), lambda bi, ci: (0, 0)),
                  pl.BlockSpec((2 * BLOCK, BLOCK), lambda bi, ci: (0, 0))],
        out_specs=blk,
        out_shape=jax.ShapeDtypeStruct((t, width), BF16),
        scratch_shapes=[pltpu.VMEM((n_heads, HG_HEAD_DIM, HG_HEAD_DIM), F32),
                        pltpu.VMEM((BLOCK, width), F32), pltpu.VMEM((BLOCK, width), F32),
                        pltpu.VMEM((BLOCK, width), F32), pltpu.VMEM((BLOCK, width), F32),
                        pltpu.VMEM((BLOCK, width), F32)],
        compiler_params=pltpu.CompilerParams(dimension_semantics=("arbitrary", "arbitrary")),
        name="hgrn2",
    )(hq, hk, hv, lf, gate, *meta, gain, _hg_cumsum_matrix())


R_E1, R_E2, R_RANK1, R_RANK2, R_G1, R_G2 = range(6)


def _outproj_kernel(osb_ref, ohg_ref, h_ref, w_ref, g_ref, wr_ref, br_ref, tri_ref,
                    h2_ref, m_ref, route_ref, cnt_ref, carry_ref, *, width):
    @pl.when(pl.program_id(0) == 0)
    def _():
        carry_ref[...] = jnp.zeros_like(carry_ref)

    h2 = h_ref[...] + _dot(osb_ref[...], w_ref[:width, :]) + _dot(ohg_ref[...], w_ref[width:, :])
    h2_ref[...] = h2
    m = _rms(h2) * g_ref[...]
    m_ref[...] = m
    lg = _dot(m, wr_ref[...]) + br_ref[...]
    tm = lg.shape[0]
    lane = lax.broadcasted_iota(jnp.int32, (tm, LANES), 1)
    neg = jnp.float32(-1e30)

    def first_argmax(vals):
        vmax = jnp.max(vals, axis=-1, keepdims=True)
        idx = jnp.min(jnp.where(vals == vmax, lane, LANES), axis=-1, keepdims=True)
        return vmax, idx

    is_grp = lane < N_GROUPS
    gl = jnp.where(is_grp, lg, neg)
    gmax, gidx = first_argmax(gl)
    p_grp = 1.0 / jnp.sum(jnp.where(is_grp, jnp.exp(gl - gmax), 0.0), axis=-1, keepdims=True)
    lo = N_GROUPS + gidx * EXPERTS_PER_GROUP
    el = jnp.where((lane >= lo) & (lane < lo + EXPERTS_PER_GROUP), lg, neg)
    v1, i1 = first_argmax(el)
    sel1 = lane == i1
    v2, i2 = first_argmax(jnp.where(sel1, neg, el))
    sel2 = lane == i2
    dlt = jnp.exp(v2 - v1)
    g1 = p_grp / (1.0 + dlt)
    g2 = g1 * dlt

    chosen = jnp.where(sel1 | sel2, 1.0, 0.0)
    carry = carry_ref[0:1, :]
    before = _dot(tri_ref[...], chosen.astype(BF16)) + carry
    r1 = jnp.sum(jnp.where(sel1, before, 0.0), axis=-1, keepdims=True)
    r2 = jnp.sum(jnp.where(sel2, before, 0.0), axis=-1, keepdims=True)
    carry = carry + jnp.sum(chosen, axis=0, keepdims=True)
    carry_ref[0:1, :] = carry
    cnt_ref[...] = jnp.broadcast_to(carry, cnt_ref.shape)

    rec = jnp.zeros((tm, LANES), F32)
    for ln, val in ((R_E1, (i1 - N_GROUPS).astype(F32)), (R_E2, (i2 - N_GROUPS).astype(F32)),
                    (R_RANK1, r1), (R_RANK2, r2), (R_G1, g1), (R_G2, g2)):
        rec = jnp.where(lane == ln, val, rec)
    route_ref[...] = rec


def _outproj(o_sb, o_hg, h, w_out, g_ffn, w_r, b_r, tm):
    t, d = h.shape
    width = o_sb.shape[1]
    row = lambda i: (i, 0)
    const = lambda i: (0, 0)
    tri = jnp.asarray(np.tril(np.ones((tm, tm), np.float32), -1), dtype=BF16)
    kern = functools.partial(_outproj_kernel, width=width)
    return pl.pallas_call(
        kern,
        grid=(t // tm,),
        in_specs=[pl.BlockSpec((tm, width), row), pl.BlockSpec((tm, width), row),
                  pl.BlockSpec((tm, d), row), pl.BlockSpec((2 * width, d), const),
                  pl.BlockSpec((1, d), const), pl.BlockSpec((d, LANES), const),
                  pl.BlockSpec((1, LANES), const), pl.BlockSpec((tm, tm), const)],
        out_specs=[pl.BlockSpec((tm, d), row), pl.BlockSpec((tm, d), row),
                   pl.BlockSpec((tm, LANES), row), pl.BlockSpec((8, LANES), const)],
        out_shape=[jax.ShapeDtypeStruct((t, d), F32), jax.ShapeDtypeStruct((t, d), F32),
                   jax.ShapeDtypeStruct((t, LANES), F32), jax.ShapeDtypeStruct((8, LANES), F32)],
        scratch_shapes=[pltpu.VMEM((8, LANES), F32)],
        compiler_params=pltpu.CompilerParams(
            dimension_semantics=("arbitrary",), vmem_limit_bytes=40 * 1024 * 1024),
        name="outproj",
    )(o_sb, o_hg, h, w_out, g_ffn, w_r, b_r, tri)


def _row_copy(src, dst, sem):
    return pltpu.make_async_copy(src, dst, sem)


def _dispatch_kernel(d1_ref, d2_ref, zs_ref, zn_ref, m_hbm, xs_hbm, zero_ref, stage_ref, sems, lsems, zsem,
                     *, tile, n_tiles):
    i = pl.program_id(0)
    slot = i % 3

    def tile_wait(s):
        _row_copy(m_hbm.at[pl.ds(0, 2 * tile), :], xs_hbm.at[pl.ds(0, 2 * tile), :], sems.at[s]).wait()

    def tile_load(step, s):
        rows = pl.ds(pl.multiple_of(step * tile, tile), tile)
        return _row_copy(m_hbm.at[rows, :], stage_ref.at[s], lsems.at[s])

    @pl.when(i == 0)
    def _():
        tile_load(0, 0).start()
        zero_ref[...] = jnp.zeros_like(zero_ref)

        def zero_block(j):
            return _row_copy(zero_ref, xs_hbm.at[pl.ds(pl.multiple_of(zs_ref[j], MOE_BM), MOE_BM), :], zsem)

        for j in range(2 * N_EXPERTS):
            @pl.when(zn_ref[j] > 0)
            def _():
                zero_block(j).start()
        for j in range(2 * N_EXPERTS):
            @pl.when(zn_ref[j] > 0)
            def _():
                zero_block(j).wait()

    nxt = (i + 1) % 3

    @pl.when(i >= 2)
    def _():
        tile_wait(nxt)

    @pl.when(i + 1 < n_tiles)
    def _():
        tile_load(i + 1, nxt).start()

    tile_load(i, slot).wait()
    base = i * tile
    for r in range(tile):
        src = stage_ref.at[slot, pl.ds(r, 1), :]
        _row_copy(src, xs_hbm.at[pl.ds(d1_ref[base + r], 1), :], sems.at[slot]).start()
        _row_copy(src, xs_hbm.at[pl.ds(d2_ref[base + r], 1), :], sems.at[slot]).start()

    @pl.when(i == n_tiles - 1)
    def _():
        if n_tiles > 1:
            tile_wait((i + 2) % 3)
        tile_wait(slot)


def _dispatch(d1, d2, zero_start, zero_n, m, n_slots, tile=BLOCK):
    t, d = m.shape
    n_tiles = t // tile
    kern = functools.partial(_dispatch_kernel, tile=tile, n_tiles=n_tiles)
    grid_spec = pltpu.PrefetchScalarGridSpec(
        num_scalar_prefetch=4,
        grid=(n_tiles,),
        in_specs=[pl.BlockSpec(memory_space=pl.ANY)],
        out_specs=pl.BlockSpec(memory_space=pl.ANY),
        scratch_shapes=[pltpu.VMEM((MOE_BM, d), F32), pltpu.VMEM((3, tile, d), F32),
                        pltpu.SemaphoreType.DMA((3,)), pltpu.SemaphoreType.DMA((3,)),
                        pltpu.SemaphoreType.DMA(())],
    )
    return pl.pallas_call(
        kern,
        grid_spec=grid_spec,
        out_shape=jax.ShapeDtypeStruct((n_slots, d), F32),
        compiler_params=pltpu.CompilerParams(dimension_semantics=("arbitrary",)),
        name="dispatch",
    )(d1, d2, zero_start, zero_n, m)


def _expert_kernel(be_ref, nu_ref, xs_ref, wg_ref, wu_ref, wd_ref, y_ref):
    i = pl.program_id(0)

    @pl.when(i < nu_ref[0])
    def _():
        xs = xs_ref[...]
        gt = _dot(xs, wg_ref[...])
        hb = gt * jax.nn.sigmoid(gt) * _dot(xs, wu_ref[...])
        y_ref[...] = _dot(hb, wd_ref[...])

    @pl.when(i >= nu_ref[0])
    def _():
        y_ref[...] = jnp.zeros_like(y_ref)


def _experts(block_e, n_used, xs, w_gate, w_up, w_down):
    n_slots, d = xs.shape
    ff = w_gate.shape[-1]
    last = lambda i, be, nu: (jnp.minimum(i, nu[0] - 1), 0)
    grid_spec = pltpu.PrefetchScalarGridSpec(
        num_scalar_prefetch=2,
        grid=(n_slots // MOE_BM,),
        in_specs=[pl.BlockSpec((MOE_BM, d), last),
                  pl.BlockSpec((None, d, ff), lambda i, be, nu: (be[i], 0, 0)),
                  pl.BlockSpec((None, d, ff), lambda i, be, nu: (be[i], 0, 0)),
                  pl.BlockSpec((None, ff, d), lambda i, be, nu: (be[i], 0, 0))],
        out_specs=pl.BlockSpec((MOE_BM, d), lambda i, be, nu: (i, 0)),
    )
    return pl.pallas_call(
        _expert_kernel,
        grid_spec=grid_spec,
        out_shape=jax.ShapeDtypeStruct((n_slots, d), F32),
        compiler_params=pltpu.CompilerParams(
            dimension_semantics=("arbitrary",), vmem_limit_bytes=40 * 1024 * 1024),
        name="experts",
    )(block_e, n_used, xs, w_gate, w_up, w_down)


def _combine_kernel(d1_ref, d2_ref, h2_ref, route_ref, ys_hbm, g_ref, o_ref, ya_ref, yb_ref, sems,
                    *, tile, n_tiles):
    i = pl.program_id(0)
    slot = i % 2

    def issue(step, s):
        base = step * tile
        for r in range(tile):
            _row_copy(ys_hbm.at[pl.ds(d1_ref[base + r], 1), :], ya_ref.at[s, pl.ds(r, 1), :], sems.at[s]).start()
            _row_copy(ys_hbm.at[pl.ds(d2_ref[base + r], 1), :], yb_ref.at[s, pl.ds(r, 1), :], sems.at[s]).start()

    @pl.when(i == 0)
    def _():
        issue(0, 0)

    @pl.when(i + 1 < n_tiles)
    def _():
        issue(i + 1, 1 - slot)

    _row_copy(ys_hbm.at[pl.ds(0, tile), :], ya_ref.at[slot], sems.at[slot]).wait()
    _row_copy(ys_hbm.at[pl.ds(0, tile), :], yb_ref.at[slot], sems.at[slot]).wait()
    rec = route_ref[...]
    g1 = rec[:, R_G1:R_G1 + 1]
    g2 = rec[:, R_G2:R_G2 + 1]
    h = h2_ref[...] + (g1 * ya_ref[slot] + g2 * yb_ref[slot])
    o_ref[...] = _rms(h) * g_ref[...]


def _combine(d1, d2, h2, route, ys, g_final, tile=BLOCK):
    t, d = h2.shape
    n_tiles = t // tile
    kern = functools.partial(_combine_kernel, tile=tile, n_tiles=n_tiles)
    grid_spec = pltpu.PrefetchScalarGridSpec(
        num_scalar_prefetch=2,
        grid=(n_tiles,),
        in_specs=[pl.BlockSpec((tile, d), lambda i, a, b: (i, 0)),
                  pl.BlockSpec((tile, LANES), lambda i, a, b: (i, 0)),
                  pl.BlockSpec(memory_space=pl.ANY),
                  pl.BlockSpec((1, d), lambda i, a, b: (0, 0))],
        out_specs=pl.BlockSpec((tile, d), lambda i, a, b: (i, 0)),
        scratch_shapes=[pltpu.VMEM((2, tile, d), F32), pltpu.VMEM((2, tile, d), F32),
                        pltpu.SemaphoreType.DMA((2,))],
    )
    return pl.pallas_call(
        kern,
        grid_spec=grid_spec,
        out_shape=jax.ShapeDtypeStruct((t, d), F32),
        compiler_params=pltpu.CompilerParams(dimension_semantics=("arbitrary",)),
        name="combine",
    )(d1, d2, h2, route, ys, g_final)


def _segment_layout(route, counts_row, n_tok):
    counts = counts_row[N_GROUPS:N_GROUPS + N_EXPERTS].astype(jnp.int32)
    n_slots = n_tok * TOP_K + N_EXPERTS * MOE_BM
    n_blocks = n_slots // MOE_BM
    padded = (counts + MOE_BM - 1) // MOE_BM * MOE_BM
    padded_end = jnp.cumsum(padded)
    padded_start = padded_end - padded
    e1 = route[:, R_E1].astype(jnp.int32)
    e2 = route[:, R_E2].astype(jnp.int32)
    onehot = jnp.arange(N_EXPERTS, dtype=jnp.int32)[None, :]
    start1 = jnp.sum(jnp.where(e1[:, None] == onehot, padded_start[None, :], 0), axis=1)
    start2 = jnp.sum(jnp.where(e2[:, None] == onehot, padded_start[None, :], 0), axis=1)
    d1 = start1 + route[:, R_RANK1].astype(jnp.int32)
    d2 = start2 + route[:, R_RANK2].astype(jnp.int32)
    block_e = jnp.sum((jnp.arange(n_blocks, dtype=jnp.int32)[:, None] * MOE_BM >= padded_end[None, :])
                      .astype(jnp.int32), axis=1)
    block_e = jnp.minimum(block_e, N_EXPERTS - 1)
    n_used = (padded_end[-1] // MOE_BM).reshape(1)
    trailing = n_used[0] + jnp.arange(N_EXPERTS, dtype=jnp.int32)
    zero_start = jnp.concatenate([jnp.maximum(padded_end - MOE_BM, 0),
                                  jnp.minimum(trailing, n_blocks - 1) * MOE_BM])
    zero_flag = jnp.concatenate([counts, (trailing < n_blocks).astype(jnp.int32)])
    return d1, d2, block_e, n_used, zero_start, zero_flag, n_slots


def kernel(x, meta_tokens, lb_logits, g_mix, w_in, sb_gain, hg_gain, w_out, g_ffn, w_router_group,
           b_router_group, w_router_expert, b_router_expert, w_expert_gate, w_expert_up, w_expert_down,
           g_final):
    b, seq, d = x.shape
    depth = w_in.shape[0]
    assert depth == 1, "single-layer block"
    assert seq % (Q_SUB * BLOCK) == 0
    t = b * seq
    tm = 512
    layer = 0

    xr = x.reshape(t, d)
    prefix = jnp.concatenate([jnp.zeros((LEAD_PAD, d), x.dtype), meta_tokens.astype(x.dtype)], axis=0)
    prefix_mask = (jnp.arange(BLOCK) >= LEAD_PAD).astype(F32)[:, None]
    lower_bounds = jnp.cumsum(jax.nn.softmax(lb_logits.astype(F32), axis=0), axis=0)
    lb = lower_bounds[layer][None, :]
    g_mix_l = g_mix[layer][None, :]
    w_in_l = w_in[layer].astype(BF16)

    q, k0, k1, v0, v1, hq, hk, hv, lf, hg = _proj(xr, jnp.ones((t, 1), F32), g_mix_l, w_in_l, lb, tm)
    _, mk0, mk1, mv0, mv1, _, mhk, mhv, mlf, _ = _proj(prefix, prefix_mask, g_mix_l, w_in_l, lb, BLOCK)

    o_sb = _sb_attention(q, (k0, k1, v0, v1), (mk0, mk1, mv0, mv1), sb_gain[layer][None, :], b, seq)
    o_hg = _hgrn2(hq, hk, hv, lf, hg, (mhk, mhv, mlf), hg_gain[layer][None, :], b, seq)

    w_r = jnp.zeros((d, LANES), F32)
    w_r = w_r.at[:, :N_GROUPS].set(w_router_group[layer])
    w_r = w_r.at[:, N_GROUPS:N_GROUPS + N_EXPERTS].set(w_router_expert[layer])
    b_r = jnp.zeros((1, LANES), F32)
    b_r = b_r.at[0, :N_GROUPS].set(b_router_group[layer])
    b_r = b_r.at[0, N_GROUPS:N_GROUPS + N_EXPERTS].set(b_router_expert[layer])
    h2, m, route, counts = _outproj(o_sb, o_hg, xr, w_out[layer].astype(BF16), g_ffn[layer][None, :],
                                    w_r, b_r, tm)

    d1, d2, block_e, n_used, zero_start, zero_n, n_slots = _segment_layout(route, counts[0], t)
    xs = _dispatch(d1, d2, zero_start, zero_n, m, n_slots)
    ys = _experts(block_e, n_used, xs, w_expert_gate[layer], w_expert_up[layer], w_expert_down[layer])
    out = _combine(d1, d2, h2, route, ys, g_final[None, :])
    return out.reshape(b, seq, d)
```

```python
import functools

import numpy as np
import jax
import jax.numpy as jnp
from jax import lax
from jax.experimental import pallas as pl
from jax.experimental.pallas import tpu as pltpu

BLOCK = 128
N_META = 16
LEAD_PAD = BLOCK - N_META
SB_HEAD_DIM = 64
HG_HEAD_DIM = 128
SUB = 16
Q_SUB = 2
DEAD_LOG_TAIL = -104.0
N_GROUPS = 4
EXPERTS_PER_GROUP = 8
N_EXPERTS = N_GROUPS * EXPERTS_PER_GROUP
TOP_K = 2
MOE_BM = 256
EPS = 1e-6
LANES = 128

F32 = jnp.float32
BF16 = jnp.bfloat16


def _dot(a, b):
    return jnp.dot(a, b, preferred_element_type=F32)


def _dot_nt(a, b):
    return lax.dot_general(a, b, (((1,), (1,)), ((), ())), preferred_element_type=F32)


def _dot_tn(a, b):
    return lax.dot_general(a, b, (((0,), (0,)), ((), ())), preferred_element_type=F32)


def _rms(x):
    return x * lax.rsqrt(jnp.mean(x * x, axis=-1, keepdims=True) + EPS)


def _proj_kernel(h_ref, mask_ref, g_ref, w_ref, lb_ref,
                 q_ref, k0_ref, k1_ref, v0_ref, v1_ref,
                 hq_ref, hk_ref, hv_ref, lf_ref, hg_ref, *, width):
    a = (_rms(h_ref[...]) * g_ref[...]).astype(BF16)

    def p(i):
        return _dot(a, w_ref[:, i * width:(i + 1) * width])

    lane = lax.broadcasted_iota(jnp.int32, (1, width), 1)
    head0 = (lane & (LANES - 1)) < SB_HEAD_DIM
    q_ref[...] = (p(0) * (SB_HEAD_DIM ** -0.5)).astype(BF16)
    k = p(1)
    k0_ref[...] = jnp.where(head0, k, 0.0).astype(BF16)
    k1_ref[...] = jnp.where(head0, 0.0, k).astype(BF16)
    v = p(2)
    v0_ref[...] = jnp.where(head0, v, 0.0).astype(BF16)
    v1_ref[...] = jnp.where(head0, 0.0, v).astype(BF16)
    hq = p(3)
    hq_ref[...] = hq * jax.nn.sigmoid(hq)
    f = p(4)
    lb = lb_ref[...]
    sig = jax.nn.sigmoid(f)
    lf_ref[...] = jnp.log(lb + (1.0 - lb) * sig)
    hk_ref[...] = mask_ref[...] * ((1.0 - lb) * jax.nn.sigmoid(-f))
    hv_ref[...] = p(5)
    g = p(6)
    hg_ref[...] = g * jax.nn.sigmoid(g)


def _proj(h, mask, g_mix, w_in, lb, tm):
    t, d = h.shape
    width = d // 2
    kern = functools.partial(_proj_kernel, width=width)
    row = lambda i: (i, 0)
    const = lambda i: (0, 0)
    outs = ([jax.ShapeDtypeStruct((t, width), BF16)] * 5
            + [jax.ShapeDtypeStruct((t, width), F32)] * 5)
    return pl.pallas_call(
        kern,
        grid=(t // tm,),
        in_specs=[pl.BlockSpec((tm, d), row), pl.BlockSpec((tm, 1), row),
                  pl.BlockSpec((1, d), const), pl.BlockSpec((d, 7 * width), const),
                  pl.BlockSpec((1, width), const)],
        out_specs=[pl.BlockSpec((tm, width), row)] * 10,
        out_shape=outs,
        compiler_params=pltpu.CompilerParams(
            dimension_semantics=("arbitrary",), vmem_limit_bytes=52 * 1024 * 1024),
        name="proj",
    )(h, mask, g_mix, w_in, lb)


def _sb_kernel(q_ref, k0_ref, k1_ref, v0_ref, v1_ref, mk0_ref, mk1_ref, mv0_ref, mv1_ref,
               tt_ref, gain_ref, o_ref, c_ref, acc_ref, zz_ref, w_ref):
    tq = Q_SUB * BLOCK
    base = pl.program_id(2) * Q_SUB
    c_ref[...] = jnp.zeros_like(c_ref)
    acc_ref[...] = jnp.zeros_like(acc_ref)

    def col_iota(rows):
        return lax.broadcasted_iota(jnp.int32, (rows, 2 * BLOCK), 1) & (BLOCK - 1)

    def k_real(kb):
        rows = pl.ds(pl.multiple_of(kb * BLOCK, BLOCK), BLOCK)
        return jnp.concatenate([k0_ref[rows, :], k1_ref[rows, :]], axis=0)

    def v_real(kb):
        rows = pl.ds(pl.multiple_of(kb * BLOCK, BLOCK), BLOCK)
        return jnp.concatenate([v0_ref[rows, :], v1_ref[rows, :]], axis=0)

    def k_real_or_prefix(kb):
        k_prefix = jnp.concatenate([mk0_ref[...], mk1_ref[...]], axis=0)
        return jnp.where(kb >= 0, k_real(jnp.maximum(kb, 0)), k_prefix)

    def step(mask, k_next, v_prev, lo=0, lo_next=0, lo_prev=0):
        zz = zz_ref[lo:, :]
        if k_next is not None:
            zz_ref[lo_next:, :] = _dot_nt(q_ref[lo_next:, :], k_next)
        if v_prev is not None:
            acc_ref[lo_prev:, :] += _dot(w_ref[lo_prev:, :], v_prev)
        sp = jnp.maximum(zz, 0.0) + jnp.log(1.0 + jnp.exp(-jnp.abs(zz)))
        spm = sp if mask is None else jnp.where(mask, sp, 0.0)
        hi = spm.astype(BF16)
        lo_part = (spm - hi.astype(F32)).astype(BF16)
        tt = tt_ref[...]
        tail = jnp.concatenate(
            [_dot(jnp.concatenate([hi[:, h * BLOCK:(h + 1) * BLOCK], lo_part[:, h * BLOCK:(h + 1) * BLOCK]],
                                  axis=1), tt) for h in range(2)], axis=1)
        c = c_ref[lo:, :]
        w = jnp.exp(zz - sp + tail + c)
        if mask is not None:
            w = jnp.where(mask, w, 0.0)
        w_ref[lo:, :] = w.astype(BF16)
        tot0 = jnp.sum(spm[:, :BLOCK], axis=-1, keepdims=True)
        tot1 = jnp.sum(spm[:, BLOCK:], axis=-1, keepdims=True)
        c_ref[lo:, :] = c - jnp.concatenate([jnp.broadcast_to(tot0, (tq - lo, BLOCK)),
                                             jnp.broadcast_to(tot1, (tq - lo, BLOCK))], axis=1)

    top = base + Q_SUB - 1
    zz_ref[(Q_SUB - 1) * BLOCK:, :] = _dot_nt(q_ref[(Q_SUB - 1) * BLOCK:, :], k_real(top))
    for jj in reversed(range(Q_SUB)):
        kb = base + jj
        lo = jj * BLOCK
        step(col_iota(tq - lo) < lax.broadcasted_iota(jnp.int32, (tq - lo, 2 * BLOCK), 0),
             k_real(kb - 1) if jj > 0 else k_real_or_prefix(kb - 1),
             v_real(kb + 1) if jj < Q_SUB - 1 else None,
             lo=lo, lo_next=max(lo - BLOCK, 0), lo_prev=lo + BLOCK)

    def alive():
        return jnp.max(c_ref[...]) > DEAD_LOG_TAIL

    def cond(carry):
        kb, live = carry
        return jnp.logical_and(kb >= 0, live)

    def body(carry):
        kb, _ = carry
        step(None, k_real_or_prefix(kb - 1), v_real(kb + 1))
        return kb - 1, alive()

    kb, live = lax.while_loop(cond, body, (base - 1, alive()))
    acc_ref[...] += _dot(w_ref[...], v_real(kb + 1))

    @pl.when(jnp.logical_and(kb < 0, live))
    def _():
        step(col_iota(tq) >= LEAD_PAD, None, None)
        acc_ref[...] += _dot(w_ref[...], jnp.concatenate([mv0_ref[...], mv1_ref[...]], axis=0))

    head0 = lax.broadcasted_iota(jnp.int32, (tq, LANES), 1) < SB_HEAD_DIM
    o = acc_ref[...]
    o2 = o * o
    s0 = jnp.sum(jnp.where(head0, o2, 0.0), axis=-1, keepdims=True) * (1.0 / SB_HEAD_DIM)
    s1 = jnp.sum(jnp.where(head0, 0.0, o2), axis=-1, keepdims=True) * (1.0 / SB_HEAD_DIM)
    r = jnp.where(head0, lax.rsqrt(s0 + EPS), lax.rsqrt(s1 + EPS))
    o_ref[...] = (o * r * gain_ref[...]).astype(o_ref.dtype)


def _sb_tail_matrix():
    j = np.arange(BLOCK)[:, None]
    s = np.arange(BLOCK)[None, :]
    tri = np.where(j > s, -1.0, 0.0)
    return jnp.asarray(np.concatenate([tri, tri], axis=0), dtype=BF16)


def _sb_attention(q, kv, kv_meta, gain, b, seq):
    t, width = q.shape
    nqt = seq // (Q_SUB * BLOCK)
    npair = width // LANES
    kv3 = [a.reshape(b, seq, width) for a in kv]
    tq = Q_SUB * BLOCK
    kv_spec = pl.BlockSpec((None, seq, LANES), lambda bi, hp, qt: (bi, 0, hp))
    meta_spec = pl.BlockSpec((BLOCK, LANES), lambda bi, hp, qt: (0, hp))
    return pl.pallas_call(
        _sb_kernel,
        grid=(b, npair, nqt),
        in_specs=[pl.BlockSpec((tq, LANES), lambda bi, hp, qt: (bi * nqt + qt, hp)),
                  kv_spec, kv_spec, kv_spec, kv_spec,
                  meta_spec, meta_spec, meta_spec, meta_spec,
                  pl.BlockSpec((2 * BLOCK, BLOCK), lambda bi, hp, qt: (0, 0)),
                  pl.BlockSpec((1, LANES), lambda bi, hp, qt: (0, hp))],
        out_specs=pl.BlockSpec((tq, LANES), lambda bi, hp, qt: (bi * nqt + qt, hp)),
        out_shape=jax.ShapeDtypeStruct((t, width), BF16),
        scratch_shapes=[pltpu.VMEM((tq, 2 * BLOCK), F32), pltpu.VMEM((tq, LANES), F32),
                        pltpu.VMEM((tq, 2 * BLOCK), F32), pltpu.VMEM((tq, 2 * BLOCK), BF16)],
        compiler_params=pltpu.CompilerParams(
            dimension_semantics=("arbitrary", "arbitrary", "arbitrary"),
            vmem_limit_bytes=40 * 1024 * 1024),
        name="sb_attn",
    )(q, *kv3, *kv_meta, _sb_tail_matrix(), gain)


def _hg_block(hq_ref, hk_ref, hv_ref, lf_ref, lmat_ref, st_ref, a_ref, qt_ref, kh_ref, oacc_ref, f_ref, n_heads,
              with_output):
    lf = lf_ref[...]
    h1 = lf.astype(BF16)
    r1 = lf - h1.astype(F32)
    h2 = r1.astype(BF16)
    h3 = (r1 - h2.astype(F32)).astype(BF16)
    lmat = lmat_ref[...]
    cs = _dot(lmat, h1) + _dot(lmat, h2) + _dot(lmat, h3)
    a = cs[:BLOCK]
    alast = cs[BLOCK:]
    a_ref[...] = alast
    kh_ref[...] = hk_ref[...] * jnp.exp(alast - a)
    if with_output:
        f_ref[...] = jnp.exp(lf)
        qt_ref[...] = hq_ref[...] * jnp.exp(a)
    ridx = lax.broadcasted_iota(jnp.int32, (SUB, 1), 0)

    for i in range(BLOCK // SUB):
        r0 = i * SUB
        rows = pl.ds(r0, SUB)
        for hd in range(n_heads):
            cols = slice(hd * HG_HEAD_DIM, (hd + 1) * HG_HEAD_DIM)
            v_i = hv_ref[rows, cols]
            st = st_ref[hd]
            if with_output:
                o_i = _dot_nt(qt_ref[rows, cols], st)
                e = hq_ref[rows, cols]
                for s in reversed(range(SUB)):
                    if s < SUB - 1:
                        e = jnp.where(ridx > s, e * f_ref[pl.ds(r0 + s + 1, 1), cols], e)
                    sc = jnp.sum(e * hk_ref[pl.ds(r0 + s, 1), cols], axis=-1, keepdims=True)
                    o_i = o_i + jnp.where(ridx >= s, sc, 0.0) * hv_ref[pl.ds(r0 + s, 1), cols]
                oacc_ref[rows, cols] = o_i
            decay = jnp.exp(a_ref[pl.ds(r0, 1), cols])
            st_ref[hd] = st * decay + _dot_tn(v_i, kh_ref[rows, cols])


def _hg_kernel(hq_ref, hk_ref, hv_ref, lf_ref, gate_ref, mk_ref, mv_ref, mlf_ref, gain_ref, lmat_ref, o_ref,
               st_ref, a_ref, qt_ref, kh_ref, oacc_ref, f_ref, *, n_heads):
    scratch = (st_ref, a_ref, qt_ref, kh_ref, oacc_ref, f_ref)

    @pl.when(pl.program_id(1) == 0)
    def _():
        st_ref[...] = jnp.zeros_like(st_ref)
        _hg_block(None, mk_ref, mv_ref, mlf_ref, lmat_ref, *scratch, n_heads, with_output=False)

    _hg_block(hq_ref, hk_ref, hv_ref, lf_ref, lmat_ref, *scratch, n_heads, with_output=True)
    for hd in range(n_heads):
        cols = slice(hd * HG_HEAD_DIM, (hd + 1) * HG_HEAD_DIM)
        o = _rms(oacc_ref[:, cols]) * gain_ref[:, cols] * gate_ref[:, cols]
        o_ref[:, cols] = o.astype(o_ref.dtype)


def _hg_cumsum_matrix():
    t = np.arange(BLOCK)[:, None]
    s = np.arange(BLOCK)[None, :]
    same = (t // SUB) == (s // SUB)
    incl = np.where(same & (s <= t), 1.0, 0.0)
    full = np.where(same, 1.0, 0.0)
    return jnp.asarray(np.concatenate([incl, full], axis=0), dtype=BF16)


def _hgrn2(hq, hk, hv, lf, gate, meta, gain, b, seq):
    t, width = hq.shape
    nc = seq // BLOCK
    n_heads = width // HG_HEAD_DIM
    blk = pl.BlockSpec((BLOCK, width), lambda bi, ci: (bi * nc + ci, 0))
    mblk = pl.BlockSpec((BLOCK, width), lambda bi, ci: (0, 0))
    kern = functools.partial(_hg_kernel, n_heads=n_heads)
    return pl.pallas_call(
        kern,
        grid=(b, nc),
        in_specs=[blk, blk, blk, blk, blk, mblk, mblk, mblk,
                  pl.BlockSpec((1, width), lambda bi, ci: (0, 0)),
                  pl.BlockSpec((2 * BLOCK, BLOCK), lambda bi, ci: (0, 0))],
        out_specs=blk,
        out_shape=jax.ShapeDtypeStruct((t, width), BF16),
        scratch_shapes=[pltpu.VMEM((n_heads, HG_HEAD_DIM, HG_HEAD_DIM), F32),
                        pltpu.VMEM((BLOCK, width), F32), pltpu.VMEM((BLOCK, width), F32),
                        pltpu.VMEM((BLOCK, width), F32), pltpu.VMEM((BLOCK, width), F32),
                        pltpu.VMEM((BLOCK, width), F32)],
        compiler_params=pltpu.CompilerParams(dimension_semantics=("arbitrary", "arbitrary")),
        name="hgrn2",
    )(hq, hk, hv, lf, gate, *meta, gain, _hg_cumsum_matrix())


R_E1, R_E2, R_RANK1, R_RANK2, R_G1, R_G2 = range(6)


def _outproj_kernel(osb_ref, ohg_ref, h_ref, w_ref, g_ref, wr_ref, br_ref, tri_ref,
                    h2_ref, m_ref, route_ref, cnt_ref, carry_ref, *, width):
    @pl.when(pl.program_id(0) == 0)
    def _():
        carry_ref[...] = jnp.zeros_like(carry_ref)

    h2 = h_ref[...] + _dot(osb_ref[...], w_ref[:width, :]) + _dot(ohg_ref[...], w_ref[width:, :])
    h2_ref[...] = h2
    m = _rms(h2) * g_ref[...]
    m_ref[...] = m
    lg = _dot(m, wr_ref[...]) + br_ref[...]
    tm = lg.shape[0]
    lane = lax.broadcasted_iota(jnp.int32, (tm, LANES), 1)
    neg = jnp.float32(-1e30)

    def first_argmax(vals):
        vmax = jnp.max(vals, axis=-1, keepdims=True)
        idx = jnp.min(jnp.where(vals == vmax, lane, LANES), axis=-1, keepdims=True)
        return vmax, idx

    is_grp = lane < N_GROUPS
    gl = jnp.where(is_grp, lg, neg)
    gmax, gidx = first_argmax(gl)
    p_grp = 1.0 / jnp.sum(jnp.where(is_grp, jnp.exp(gl - gmax), 0.0), axis=-1, keepdims=True)
    lo = N_GROUPS + gidx * EXPERTS_PER_GROUP
    el = jnp.where((lane >= lo) & (lane < lo + EXPERTS_PER_GROUP), lg, neg)
    v1, i1 = first_argmax(el)
    sel1 = lane == i1
    v2, i2 = first_argmax(jnp.where(sel1, neg, el))
    sel2 = lane == i2
    dlt = jnp.exp(v2 - v1)
    g1 = p_grp / (1.0 + dlt)
    g2 = g1 * dlt

    chosen = jnp.where(sel1 | sel2, 1.0, 0.0)
    carry = carry_ref[0:1, :]
    before = _dot(tri_ref[...], chosen.astype(BF16)) + carry
    r1 = jnp.sum(jnp.where(sel1, before, 0.0), axis=-1, keepdims=True)
    r2 = jnp.sum(jnp.where(sel2, before, 0.0), axis=-1, keepdims=True)
    carry = carry + jnp.sum(chosen, axis=0, keepdims=True)
    carry_ref[0:1, :] = carry
    cnt_ref[...] = jnp.broadcast_to(carry, cnt_ref.shape)

    rec = jnp.zeros((tm, LANES), F32)
    for ln, val in ((R_E1, (i1 - N_GROUPS).astype(F32)), (R_E2, (i2 - N_GROUPS).astype(F32)),
                    (R_RANK1, r1), (R_RANK2, r2), (R_G1, g1), (R_G2, g2)):
        rec = jnp.where(lane == ln, val, rec)
    route_ref[...] = rec


def _outproj(o_sb, o_hg, h, w_out, g_ffn, w_r, b_r, tm):
    t, d = h.shape
    width = o_sb.shape[1]
    row = lambda i: (i, 0)
    const = lambda i: (0, 0)
    tri = jnp.asarray(np.tril(np.ones((tm, tm), np.float32), -1), dtype=BF16)
    kern = functools.partial(_outproj_kernel, width=width)
    return pl.pallas_call(
        kern,
        grid=(t // tm,),
        in_specs=[pl.BlockSpec((tm, width), row), pl.BlockSpec((tm, width), row),
                  pl.BlockSpec((tm, d), row), pl.BlockSpec((2 * width, d), const),
                  pl.BlockSpec((1, d), const), pl.BlockSpec((d, LANES), const),
                  pl.BlockSpec((1, LANES), const), pl.BlockSpec((tm, tm), const)],
        out_specs=[pl.BlockSpec((tm, d), row), pl.BlockSpec((tm, d), row),
                   pl.BlockSpec((tm, LANES), row), pl.BlockSpec((8, LANES), const)],
        out_shape=[jax.ShapeDtypeStruct((t, d), F32), jax.ShapeDtypeStruct((t, d), F32),
                   jax.ShapeDtypeStruct((t, LANES), F32), jax.ShapeDtypeStruct((8, LANES), F32)],
        scratch_shapes=[pltpu.VMEM((8, LANES), F32)],
        compiler_params=pltpu.CompilerParams(
            dimension_semantics=("arbitrary",), vmem_limit_bytes=40 * 1024 * 1024),
        name="outproj",
    )(o_sb, o_hg, h, w_out, g_ffn, w_r, b_r, tri)


def _row_copy(src, dst, sem):
    return pltpu.make_async_copy(src, dst, sem)


def _dispatch_kernel(d1_ref, d2_ref, zs_ref, zn_ref, m_hbm, xs_hbm, zero_ref, stage_ref, sems, lsems, zsem,
                     *, tile, n_tiles):
    i = pl.program_id(0)
    slot = i % 3

    def tile_wait(s):
        _row_copy(m_hbm.at[pl.ds(0, 2 * tile), :], xs_hbm.at[pl.ds(0, 2 * tile), :], sems.at[s]).wait()

    def tile_load(step, s):
        rows = pl.ds(pl.multiple_of(step * tile, tile), tile)
        return _row_copy(m_hbm.at[rows, :], stage_ref.at[s], lsems.at[s])

    @pl.when(i == 0)
    def _():
        tile_load(0, 0).start()
        zero_ref[...] = jnp.zeros_like(zero_ref)

        def zero_block(j):
            return _row_copy(zero_ref, xs_hbm.at[pl.ds(pl.multiple_of(zs_ref[j], MOE_BM), MOE_BM), :], zsem)

        for j in range(2 * N_EXPERTS):
            @pl.when(zn_ref[j] > 0)
            def _():
                zero_block(j).start()
        for j in range(2 * N_EXPERTS):
            @pl.when(zn_ref[j] > 0)
            def _():
                zero_block(j).wait()

    nxt = (i + 1) % 3

    @pl.when(i >= 2)
    def _():
        tile_wait(nxt)

    @pl.when(i + 1 < n_tiles)
    def _():
        tile_load(i + 1, nxt).start()

    tile_load(i, slot).wait()
    base = i * tile
    for r in range(tile):
        src = stage_ref.at[slot, pl.ds(r, 1), :]
        _row_copy(src, xs_hbm.at[pl.ds(d1_ref[base + r], 1), :], sems.at[slot]).start()
        _row_copy(src, xs_hbm.at[pl.ds(d2_ref[base + r], 1), :], sems.at[slot]).start()

    @pl.when(i == n_tiles - 1)
    def _():
        if n_tiles > 1:
            tile_wait((i + 2) % 3)
        tile_wait(slot)


def _dispatch(d1, d2, zero_start, zero_n, m, n_slots, tile=BLOCK):
    t, d = m.shape
    n_tiles = t // tile
    kern = functools.partial(_dispatch_kernel, tile=tile, n_tiles=n_tiles)
    grid_spec = pltpu.PrefetchScalarGridSpec(
        num_scalar_prefetch=4,
        grid=(n_tiles,),
        in_specs=[pl.BlockSpec(memory_space=pl.ANY)],
        out_specs=pl.BlockSpec(memory_space=pl.ANY),
        scratch_shapes=[pltpu.VMEM((MOE_BM, d), F32), pltpu.VMEM((3, tile, d), F32),
                        pltpu.SemaphoreType.DMA((3,)), pltpu.SemaphoreType.DMA((3,)),
                        pltpu.SemaphoreType.DMA(())],
    )
    return pl.pallas_call(
        kern,
        grid_spec=grid_spec,
        out_shape=jax.ShapeDtypeStruct((n_slots, d), F32),
        compiler_params=pltpu.CompilerParams(dimension_semantics=("arbitrary",)),
        name="dispatch",
    )(d1, d2, zero_start, zero_n, m)


def _expert_kernel(be_ref, nu_ref, xs_ref, wg_ref, wu_ref, wd_ref, y_ref):
    i = pl.program_id(0)

    @pl.when(i < nu_ref[0])
    def _():
        xs = xs_ref[...]
        gt = _dot(xs, wg_ref[...])
        hb = gt * jax.nn.sigmoid(gt) * _dot(xs, wu_ref[...])
        y_ref[...] = _dot(hb, wd_ref[...])

    @pl.when(i >= nu_ref[0])
    def _():
        y_ref[...] = jnp.zeros_like(y_ref)


def _experts(block_e, n_used, xs, w_gate, w_up, w_down):
    n_slots, d = xs.shape
    ff = w_gate.shape[-1]
    last = lambda i, be, nu: (jnp.minimum(i, nu[0] - 1), 0)
    grid_spec = pltpu.PrefetchScalarGridSpec(
        num_scalar_prefetch=2,
        grid=(n_slots // MOE_BM,),
        in_specs=[pl.BlockSpec((MOE_BM, d), last),
                  pl.BlockSpec((None, d, ff), lambda i, be, nu: (be[i], 0, 0)),
                  pl.BlockSpec((None, d, ff), lambda i, be, nu: (be[i], 0, 0)),
                  pl.BlockSpec((None, ff, d), lambda i, be, nu: (be[i], 0, 0))],
        out_specs=pl.BlockSpec((MOE_BM, d), lambda i, be, nu: (i, 0)),
    )
    return pl.pallas_call(
        _expert_kernel,
        grid_spec=grid_spec,
        out_shape=jax.ShapeDtypeStruct((n_slots, d), F32),
        compiler_params=pltpu.CompilerParams(
            dimension_semantics=("arbitrary",), vmem_limit_bytes=40 * 1024 * 1024),
        name="experts",
    )(block_e, n_used, xs, w_gate, w_up, w_down)


def _combine_kernel(d1_ref, d2_ref, h2_ref, route_ref, ys_hbm, g_ref, o_ref, ya_ref, yb_ref, sems,
                    *, tile, n_tiles):
    i = pl.program_id(0)
    slot = i % 2

    def issue(step, s):
        base = step * tile
        for r in range(tile):
            _row_copy(ys_hbm.at[pl.ds(d1_ref[base + r], 1), :], ya_ref.at[s, pl.ds(r, 1), :], sems.at[s]).start()
            _row_copy(ys_hbm.at[pl.ds(d2_ref[base + r], 1), :], yb_ref.at[s, pl.ds(r, 1), :], sems.at[s]).start()

    @pl.when(i == 0)
    def _():
        issue(0, 0)

    @pl.when(i + 1 < n_tiles)
    def _():
        issue(i + 1, 1 - slot)

    _row_copy(ys_hbm.at[pl.ds(0, tile), :], ya_ref.at[slot], sems.at[slot]).wait()
    _row_copy(ys_hbm.at[pl.ds(0, tile), :], yb_ref.at[slot], sems.at[slot]).wait()
    rec = route_ref[...]
    g1 = rec[:, R_G1:R_G1 + 1]
    g2 = rec[:, R_G2:R_G2 + 1]
    h = h2_ref[...] + (g1 * ya_ref[slot] + g2 * yb_ref[slot])
    o_ref[...] = _rms(h) * g_ref[...]


def _combine(d1, d2, h2, route, ys, g_final, tile=BLOCK):
    t, d = h2.shape
    n_tiles = t // tile
    kern = functools.partial(_combine_kernel, tile=tile, n_tiles=n_tiles)
    grid_spec = pltpu.PrefetchScalarGridSpec(
        num_scalar_prefetch=2,
        grid=(n_tiles,),
        in_specs=[pl.BlockSpec((tile, d), lambda i, a, b: (i, 0)),
                  pl.BlockSpec((tile, LANES), lambda i, a, b: (i, 0)),
                  pl.BlockSpec(memory_space=pl.ANY),
                  pl.BlockSpec((1, d), lambda i, a, b: (0, 0))],
        out_specs=pl.BlockSpec((tile, d), lambda i, a, b: (i, 0)),
        scratch_shapes=[pltpu.VMEM((2, tile, d), F32), pltpu.VMEM((2, tile, d), F32),
                        pltpu.SemaphoreType.DMA((2,))],
    )
    return pl.pallas_call(
        kern,
        grid_spec=grid_spec,
        out_shape=jax.ShapeDtypeStruct((t, d), F32),
        compiler_params=pltpu.CompilerParams(dimension_semantics=("arbitrary",)),
        name="combine",
    )(d1, d2, h2, route, ys, g_final)


def _segment_layout(route, counts_row, n_tok):
    counts = counts_row[N_GROUPS:N_GROUPS + N_EXPERTS].astype(jnp.int32)
    n_slots = n_tok * TOP_K + N_EXPERTS * MOE_BM
    n_blocks = n_slots // MOE_BM
    padded = (counts + MOE_BM - 1) // MOE_BM * MOE_BM
    padded_end = jnp.cumsum(padded)
    padded_start = padded_end - padded
    e1 = route[:, R_E1].astype(jnp.int32)
    e2 = route[:, R_E2].astype(jnp.int32)
    onehot = jnp.arange(N_EXPERTS, dtype=jnp.int32)[None, :]
    start1 = jnp.sum(jnp.where(e1[:, None] == onehot, padded_start[None, :], 0), axis=1)
    start2 = jnp.sum(jnp.where(e2[:, None] == onehot, padded_start[None, :], 0), axis=1)
    d1 = start1 + route[:, R_RANK1].astype(jnp.int32)
    d2 = start2 + route[:, R_RANK2].astype(jnp.int32)
    block_e = jnp.sum((jnp.arange(n_blocks, dtype=jnp.int32)[:, None] * MOE_BM >= padded_end[None, :])
                      .astype(jnp.int32), axis=1)
    block_e = jnp.minimum(block_e, N_EXPERTS - 1)
    n_used = (padded_end[-1] // MOE_BM).reshape(1)
    trailing = n_used[0] + jnp.arange(N_EXPERTS, dtype=jnp.int32)
    zero_start = jnp.concatenate([jnp.maximum(padded_end - MOE_BM, 0),
                                  jnp.minimum(trailing, n_blocks - 1) * MOE_BM])
    zero_flag = jnp.concatenate([counts, (trailing < n_blocks).astype(jnp.int32)])
    return d1, d2, block_e, n_used, zero_start, zero_flag, n_slots


def kernel(x, meta_tokens, lb_logits, g_mix, w_in, sb_gain, hg_gain, w_out, g_ffn, w_router_group,
           b_router_group, w_router_expert, b_router_expert, w_expert_gate, w_expert_up, w_expert_down,
           g_final):
    b, seq, d = x.shape
    depth = w_in.shape[0]
    assert depth == 1, "single-layer block"
    assert seq % (Q_SUB * BLOCK) == 0
    t = b * seq
    tm = 512
    layer = 0

    xr = x.reshape(t, d)
    prefix = jnp.concatenate([jnp.zeros((LEAD_PAD, d), x.dtype), meta_tokens.astype(x.dtype)], axis=0)
    prefix_mask = (jnp.arange(BLOCK) >= LEAD_PAD).astype(F32)[:, None]
    lower_bounds = jnp.cumsum(jax.nn.softmax(lb_logits.astype(F32), axis=0), axis=0)
    lb = lower_bounds[layer][None, :]
    g_mix_l = g_mix[layer][None, :]
    w_in_l = w_in[layer].astype(BF16)

    q, k0, k1, v0, v1, hq, hk, hv, lf, hg = _proj(xr, jnp.ones((t, 1), F32), g_mix_l, w_in_l, lb, tm)
    _, mk0, mk1, mv0, mv1, _, mhk, mhv, mlf, _ = _proj(prefix, prefix_mask, g_mix_l, w_in_l, lb, BLOCK)

    o_sb = _sb_attention(q, (k0, k1, v0, v1), (mk0, mk1, mv0, mv1), sb_gain[layer][None, :], b, seq)
    o_hg = _hgrn2(hq, hk, hv, lf, hg, (mhk, mhv, mlf), hg_gain[layer][None, :], b, seq)

    w_r = jnp.zeros((d, LANES), F32)
    w_r = w_r.at[:, :N_GROUPS].set(w_router_group[layer])
    w_r = w_r.at[:, N_GROUPS:N_GROUPS + N_EXPERTS].set(w_router_expert[layer])
    b_r = jnp.zeros((1, LANES), F32)
    b_r = b_r.at[0, :N_GROUPS].set(b_router_group[layer])
    b_r = b_r.at[0, N_GROUPS:N_GROUPS + N_EXPERTS].set(b_router_expert[layer])
    h2, m, route, counts = _outproj(o_sb, o_hg, xr, w_out[layer].astype(BF16), g_ffn[layer][None, :],
                                    w_r, b_r, tm)

    d1, d2, block_e, n_used, zero_start, zero_n, n_slots = _segment_layout(route, counts[0], t)
    xs = _dispatch(d1, d2, zero_start, zero_n, m, n_slots)
    ys = _experts(block_e, n_used, xs, w_expert_gate[layer], w_expert_up[layer], w_expert_down[layer])
    out = _combine(d1, d2, h2, route, ys, g_final[None, :])
    return out.reshape(b, seq, d)
```

```python
import functools

import numpy as np
import jax
import jax.numpy as jnp
from jax import lax
from jax.experimental import pallas as pl
from jax.experimental.pallas import tpu as pltpu

BLOCK = 128
N_META = 16
LEAD_PAD = BLOCK - N_META
SB_HEAD_DIM = 64
HG_HEAD_DIM = 128
SUB = 16
Q_SUB = 4
PAIRS = 1
DEAD_LOG_TAIL = -104.0
N_GROUPS = 4
EXPERTS_PER_GROUP = 8
N_EXPERTS = N_GROUPS * EXPERTS_PER_GROUP
TOP_K = 2
MOE_BM = 256
EPS = 1e-6
LANES = 128

F32 = jnp.float32
BF16 = jnp.bfloat16


def _dot(a, b):
    return jnp.dot(a, b, preferred_element_type=F32)


def _dot_nt(a, b):
    return lax.dot_general(a, b, (((1,), (1,)), ((), ())), preferred_element_type=F32)


def _dot_tn(a, b):
    return lax.dot_general(a, b, (((0,), (0,)), ((), ())), preferred_element_type=F32)


def _rms(x):
    return x * lax.rsqrt(jnp.mean(x * x, axis=-1, keepdims=True) + EPS)


def _proj_kernel(h_ref, mask_ref, g_ref, w_ref, lb_ref,
                 q_ref, k0_ref, k1_ref, v0_ref, v1_ref,
                 hq_ref, hk_ref, hv_ref, lf_ref, hg_ref, *, width):
    a = (_rms(h_ref[...]) * g_ref[...]).astype(BF16)

    def p(i):
        return _dot(a, w_ref[:, i * width:(i + 1) * width])

    lane = lax.broadcasted_iota(jnp.int32, (1, width), 1)
    head0 = (lane & (LANES - 1)) < SB_HEAD_DIM
    q_ref[...] = (p(0) * (SB_HEAD_DIM ** -0.5)).astype(BF16)
    k = p(1)
    k0_ref[...] = jnp.where(head0, k, 0.0).astype(BF16)
    k1_ref[...] = jnp.where(head0, 0.0, k).astype(BF16)
    v = p(2)
    v0_ref[...] = jnp.where(head0, v, 0.0).astype(BF16)
    v1_ref[...] = jnp.where(head0, 0.0, v).astype(BF16)
    hq = p(3)
    hq_ref[...] = hq * jax.nn.sigmoid(hq)
    f = p(4)
    lb = lb_ref[...]
    sig = jax.nn.sigmoid(f)
    lf_ref[...] = jnp.log(lb + (1.0 - lb) * sig)
    hk_ref[...] = mask_ref[...] * ((1.0 - lb) * jax.nn.sigmoid(-f))
    hv_ref[...] = p(5)
    g = p(6)
    hg_ref[...] = g * jax.nn.sigmoid(g)


def _proj(h, mask, g_mix, w_in, lb, tm):
    t, d = h.shape
    width = d // 2
    kern = functools.partial(_proj_kernel, width=width)
    row = lambda i: (i, 0)
    const = lambda i: (0, 0)
    outs = ([jax.ShapeDtypeStruct((t, width), BF16)] * 5
            + [jax.ShapeDtypeStruct((t, width), F32)] * 5)
    return pl.pallas_call(
        kern,
        grid=(t // tm,),
        in_specs=[pl.BlockSpec((tm, d), row), pl.BlockSpec((tm, 1), row),
                  pl.BlockSpec((1, d), const), pl.BlockSpec((d, 7 * width), const),
                  pl.BlockSpec((1, width), const)],
        out_specs=[pl.BlockSpec((tm, width), row)] * 10,
        out_shape=outs,
        compiler_params=pltpu.CompilerParams(
            dimension_semantics=("arbitrary",), vmem_limit_bytes=52 * 1024 * 1024),
        name="proj",
    )(h, mask, g_mix, w_in, lb)


def _sb_kernel(q_ref, k0_ref, k1_ref, v0_ref, v1_ref, mk0_ref, mk1_ref, mv0_ref, mv1_ref,
               tt_ref, gain_ref, o_ref, c_ref, acc_ref, zz_ref, w_ref):
    tq = Q_SUB * BLOCK
    base = pl.program_id(2) * Q_SUB
    c_ref[...] = jnp.zeros_like(c_ref)
    acc_ref[...] = jnp.zeros_like(acc_ref)

    def col_iota(rows):
        return lax.broadcasted_iota(jnp.int32, (rows, 2 * BLOCK), 1) & (BLOCK - 1)

    def lanes(p):
        return slice(p * LANES, (p + 1) * LANES)

    def k_real(p, kb):
        rows = pl.ds(pl.multiple_of(kb * BLOCK, BLOCK), BLOCK)
        return jnp.concatenate([k0_ref[rows, lanes(p)], k1_ref[rows, lanes(p)]], axis=0)

    def v_real(p, kb):
        rows = pl.ds(pl.multiple_of(kb * BLOCK, BLOCK), BLOCK)
        return jnp.concatenate([v0_ref[rows, lanes(p)], v1_ref[rows, lanes(p)]], axis=0)

    def k_real_or_prefix(p, kb):
        k_prefix = jnp.concatenate([mk0_ref[:, lanes(p)], mk1_ref[:, lanes(p)]], axis=0)
        return jnp.where(kb >= 0, k_real(p, jnp.maximum(kb, 0)), k_prefix)

    def v_prefix(p):
        return jnp.concatenate([mv0_ref[:, lanes(p)], mv1_ref[:, lanes(p)]], axis=0)

    def scores(p, k2, lo):
        zz_ref[p, lo:, :] = _dot_nt(q_ref[lo:, lanes(p)], k2)

    def values(p, v2, lo):
        acc_ref[p, lo:, :] += _dot(w_ref[p, lo:, :], v2)

    def step(mask, k_next, v_prev, lo=0, lo_next=0, lo_prev=0):
        for p in range(PAIRS):
            zz = zz_ref[p, lo:, :]
            if k_next is not None:
                scores(p, k_next(p), lo_next)
            if v_prev is not None:
                values(p, v_prev(p), lo_prev)
            neg_abs = lax.bitcast_convert_type(
                lax.bitcast_convert_type(zz, jnp.uint32) | jnp.uint32(0x80000000), F32)
            sp = jnp.maximum(zz, 0.0) + jnp.log(1.0 + jnp.exp(neg_abs))
            spm = sp if mask is None else jnp.where(mask, sp, 0.0)
            spb = spm.astype(BF16)
            tt = tt_ref[...]
            tail = jnp.concatenate(
                [_dot(spb[:, h * BLOCK:(h + 1) * BLOCK], tt) for h in range(2)], axis=1)
            c = c_ref[p, lo:, :]
            w = jnp.exp(zz - sp + tail + c)
            if mask is not None:
                w = jnp.where(mask, w, 0.0)
            w_ref[p, lo:, :] = w.astype(BF16)
            tot0 = jnp.sum(spm[:, :BLOCK], axis=-1, keepdims=True)
            tot1 = jnp.sum(spm[:, BLOCK:], axis=-1, keepdims=True)
            c_ref[p, lo:, :] = c - jnp.concatenate([jnp.broadcast_to(tot0, (tq - lo, BLOCK)),
                                                    jnp.broadcast_to(tot1, (tq - lo, BLOCK))], axis=1)

    top = base + Q_SUB - 1
    for p in range(PAIRS):
        scores(p, k_real(p, top), (Q_SUB - 1) * BLOCK)
    for jj in reversed(range(Q_SUB)):
        kb = base + jj
        lo = jj * BLOCK
        step(col_iota(tq - lo) < lax.broadcasted_iota(jnp.int32, (tq - lo, 2 * BLOCK), 0),
             (lambda p, kb=kb: k_real(p, kb - 1)) if jj > 0 else (lambda p, kb=kb: k_real_or_prefix(p, kb - 1)),
             (lambda p, kb=kb: v_real(p, kb + 1)) if jj < Q_SUB - 1 else None,
             lo=lo, lo_next=max(lo - BLOCK, 0), lo_prev=lo + BLOCK)

    def alive():
        return jnp.max(c_ref[...]) > DEAD_LOG_TAIL

    def cond(carry):
        kb, live = carry
        return jnp.logical_and(kb >= 0, live)

    def body(carry):
        kb, _ = carry
        step(None, lambda p: k_real_or_prefix(p, kb - 1), lambda p: v_real(p, kb + 1))
        return kb - 1, alive()

    kb, live = lax.while_loop(cond, body, (base - 1, alive()))
    for p in range(PAIRS):
        values(p, v_real(p, kb + 1), 0)

    @pl.when(jnp.logical_and(kb < 0, live))
    def _():
        step(col_iota(tq) >= LEAD_PAD, None, None)
        for p in range(PAIRS):
            values(p, v_prefix(p), 0)

    head0 = lax.broadcasted_iota(jnp.int32, (tq, LANES), 1) < SB_HEAD_DIM
    for p in range(PAIRS):
        o = acc_ref[p]
        o2 = o * o
        s0 = jnp.sum(jnp.where(head0, o2, 0.0), axis=-1, keepdims=True) * (1.0 / SB_HEAD_DIM)
        s1 = jnp.sum(jnp.where(head0, 0.0, o2), axis=-1, keepdims=True) * (1.0 / SB_HEAD_DIM)
        r = jnp.where(head0, lax.rsqrt(s0 + EPS), lax.rsqrt(s1 + EPS))
        o_ref[:, lanes(p)] = (o * r * gain_ref[:, lanes(p)]).astype(o_ref.dtype)


def _sb_tail_matrix():
    j = np.arange(BLOCK)[:, None]
    s = np.arange(BLOCK)[None, :]
    return jnp.asarray(np.where(j > s, -1.0, 0.0), dtype=BF16)


def _sb_attention(q, kv, kv_meta, gain, b, seq):
    t, width = q.shape
    nqt = seq // (Q_SUB * BLOCK)
    gw = PAIRS * LANES
    ngroup = width // gw
    kv3 = [a.reshape(b, seq, width) for a in kv]
    tq = Q_SUB * BLOCK
    kv_spec = pl.BlockSpec((None, seq, gw), lambda bi, hp, qt: (bi, 0, hp))
    meta_spec = pl.BlockSpec((BLOCK, gw), lambda bi, hp, qt: (0, hp))
    return pl.pallas_call(
        _sb_kernel,
        grid=(b, ngroup, nqt),
        in_specs=[pl.BlockSpec((tq, gw), lambda bi, hp, qt: (bi * nqt + qt, hp)),
                  kv_spec, kv_spec, kv_spec, kv_spec,
                  meta_spec, meta_spec, meta_spec, meta_spec,
                  pl.BlockSpec((BLOCK, BLOCK), lambda bi, hp, qt: (0, 0)),
                  pl.BlockSpec((1, gw), lambda bi, hp, qt: (0, hp))],
        out_specs=pl.BlockSpec((tq, gw), lambda bi, hp, qt: (bi * nqt + qt, hp)),
        out_shape=jax.ShapeDtypeStruct((t, width), BF16),
        scratch_shapes=[pltpu.VMEM((PAIRS, tq, 2 * BLOCK), F32), pltpu.VMEM((PAIRS, tq, LANES), F32),
                        pltpu.VMEM((PAIRS, tq, 2 * BLOCK), F32), pltpu.VMEM((PAIRS, tq, 2 * BLOCK), BF16)],
        compiler_params=pltpu.CompilerParams(
            dimension_semantics=("arbitrary", "arbitrary", "arbitrary"),
            vmem_limit_bytes=40 * 1024 * 1024),
        name="sb_attn",
    )(q, *kv3, *kv_meta, _sb_tail_matrix(), gain)


def _hg_block(hq_ref, hk_ref, hv_ref, lf_ref, lmat_ref, st_ref, a_ref, qt_ref, kh_ref, oacc_ref, f_ref, n_heads,
              with_output):
    lf = lf_ref[...]
    h1 = lf.astype(BF16)
    r1 = lf - h1.astype(F32)
    h2 = r1.astype(BF16)
    h3 = (r1 - h2.astype(F32)).astype(BF16)
    lmat = lmat_ref[...]
    cs = _dot(lmat, h1) + _dot(lmat, h2) + _dot(lmat, h3)
    a = cs[:BLOCK]
    alast = cs[BLOCK:]
    a_ref[...] = alast
    kh_ref[...] = hk_ref[...] * jnp.exp(alast - a)
    if with_output:
        f_ref[...] = jnp.exp(lf)
        qt_ref[...] = hq_ref[...] * jnp.exp(a)
    ridx = lax.broadcasted_iota(jnp.int32, (SUB, 1), 0)

    for i in range(BLOCK // SUB):
        r0 = i * SUB
        rows = pl.ds(r0, SUB)
        for hd in range(n_heads):
            cols = slice(hd * HG_HEAD_DIM, (hd + 1) * HG_HEAD_DIM)
            v_i = hv_ref[rows, cols]
            st = st_ref[hd]
            if with_output:
                o_i = _dot_nt(qt_ref[rows, cols], st)
                e = hq_ref[rows, cols]
                for s in reversed(range(SUB)):
                    if s < SUB - 1:
                        e = jnp.where(ridx > s, e * f_ref[pl.ds(r0 + s + 1, 1), cols], e)
                    sc = jnp.sum(e * hk_ref[pl.ds(r0 + s, 1), cols], axis=-1, keepdims=True)
                    o_i = o_i + jnp.where(ridx >= s, sc, 0.0) * hv_ref[pl.ds(r0 + s, 1), cols]
                oacc_ref[rows, cols] = o_i
            decay = jnp.exp(a_ref[pl.ds(r0, 1), cols])
            st_ref[hd] = st * decay + _dot_tn(v_i, kh_ref[rows, cols])


def _hg_kernel(hq_ref, hk_ref, hv_ref, lf_ref, gate_ref, mk_ref, mv_ref, mlf_ref, gain_ref, lmat_ref, o_ref,
               st_ref, a_ref, qt_ref, kh_ref, oacc_ref, f_ref, *, n_heads):
    scratch = (st_ref, a_ref, qt_ref, kh_ref, oacc_ref, f_ref)

    @pl.when(pl.program_id(1) == 0)
    def _():
        st_ref[...] = jnp.zeros_like(st_ref)
        _hg_block(None, mk_ref, mv_ref, mlf_ref, lmat_ref, *scratch, n_heads, with_output=False)

    _hg_block(hq_ref, hk_ref, hv_ref, lf_ref, lmat_ref, *scratch, n_heads, with_output=True)
    for hd in range(n_heads):
        cols = slice(hd * HG_HEAD_DIM, (hd + 1) * HG_HEAD_DIM)
        o = _rms(oacc_ref[:, cols]) * gain_ref[:, cols] * gate_ref[:, cols]
        o_ref[:, cols] = o.astype(o_ref.dtype)


def _hg_cumsum_matrix():
    t = np.arange(BLOCK)[:, None]
    s = np.arange(BLOCK)[None, :]
    same = (t // SUB) == (s // SUB)
    incl = np.where(same & (s <= t), 1.0, 0.0)
    full = np.where(same, 1.0, 0.0)
    return jnp.asarray(np.concatenate([incl, full], axis=0), dtype=BF16)


def _hgrn2(hq, hk, hv, lf, gate, meta, gain, b, seq):
    t, width = hq.shape
    nc = seq // BLOCK
    n_heads = width // HG_HEAD_DIM
    blk = pl.BlockSpec((BLOCK, width), lambda bi, ci: (bi * nc + ci, 0))
    mblk = pl.BlockSpec((BLOCK, width), lambda bi, ci: (0, 0))
    kern = functools.partial(_hg_kernel, n_heads=n_heads)
    return pl.pallas_call(
        kern,
        grid=(b, nc),
        in_specs=[blk, blk, blk, blk, blk, mblk, mblk, mblk,
                  pl.BlockSpec((1, width), lambda bi, ci: (0, 0)),
                  pl.BlockSpec((2 * BLOCK, BLOCK), lambda bi, ci: (0, 0))],
        out_specs=blk,
        out_shape=jax.ShapeDtypeStruct((t, width), BF16),
        scratch_shapes=[pltpu.VMEM((n_heads, HG_HEAD_DIM, HG_HEAD_DIM), F32),
                        pltpu.VMEM((BLOCK, width), F32), pltpu.VMEM((BLOCK, width), F32),
                        pltpu.VMEM((BLOCK, width), F32), pltpu.VMEM((BLOCK, width), F32),
                        pltpu.VMEM((BLOCK, width), F32)],
        compiler_params=pltpu.CompilerParams(dimension_semantics=("arbitrary", "arbitrary")),
        name="hgrn2",
    )(hq, hk, hv, lf, gate, *meta, gain, _hg_cumsum_matrix())


R_E1, R_E2, R_RANK1, R_RANK2, R_G1, R_G2 = range(6)


def _outproj_kernel(osb_ref, ohg_ref, h_ref, w_ref, g_ref, wr_ref, br_ref, tri_ref,
                    h2_ref, m_ref, route_ref, cnt_ref, carry_ref, *, width):
    @pl.when(pl.program_id(0) == 0)
    def _():
        carry_ref[...] = jnp.zeros_like(carry_ref)

    h2 = h_ref[...] + _dot(osb_ref[...], w_ref[:width, :]) + _dot(ohg_ref[...], w_ref[width:, :])
    h2_ref[...] = h2
    m = _rms(h2) * g_ref[...]
    m_ref[...] = m
    lg = _dot(m, wr_ref[...]) + br_ref[...]
    tm = lg.shape[0]
    lane = lax.broadcasted_iota(jnp.int32, (tm, LANES), 1)
    neg = jnp.float32(-1e30)

    def first_argmax(vals):
        vmax = jnp.max(vals, axis=-1, keepdims=True)
        idx = jnp.min(jnp.where(vals == vmax, lane, LANES), axis=-1, keepdims=True)
        return vmax, idx

    is_grp = lane < N_GROUPS
    gl = jnp.where(is_grp, lg, neg)
    gmax, gidx = first_argmax(gl)
    p_grp = 1.0 / jnp.sum(jnp.where(is_grp, jnp.exp(gl - gmax), 0.0), axis=-1, keepdims=True)
    lo = N_GROUPS + gidx * EXPERTS_PER_GROUP
    el = jnp.where((lane >= lo) & (lane < lo + EXPERTS_PER_GROUP), lg, neg)
    v1, i1 = first_argmax(el)
    sel1 = lane == i1
    v2, i2 = first_argmax(jnp.where(sel1, neg, el))
    sel2 = lane == i2
    dlt = jnp.exp(v2 - v1)
    g1 = p_grp / (1.0 + dlt)
    g2 = g1 * dlt

    chosen = jnp.where(sel1 | sel2, 1.0, 0.0)
    carry = carry_ref[0:1, :]
    before = _dot(tri_ref[...], chosen.astype(BF16)) + carry
    r1 = jnp.sum(jnp.where(sel1, before, 0.0), axis=-1, keepdims=True)
    r2 = jnp.sum(jnp.where(sel2, before, 0.0), axis=-1, keepdims=True)
    carry = carry + jnp.sum(chosen, axis=0, keepdims=True)
    carry_ref[0:1, :] = carry
    cnt_ref[...] = jnp.broadcast_to(carry, cnt_ref.shape)

    rec = jnp.zeros((tm, LANES), F32)
    for ln, val in ((R_E1, (i1 - N_GROUPS).astype(F32)), (R_E2, (i2 - N_GROUPS).astype(F32)),
                    (R_RANK1, r1), (R_RANK2, r2), (R_G1, g1), (R_G2, g2)):
        rec = jnp.where(lane == ln, val, rec)
    route_ref[...] = rec


def _outproj(o_sb, o_hg, h, w_out, g_ffn, w_r, b_r, tm):
    t, d = h.shape
    width = o_sb.shape[1]
    row = lambda i: (i, 0)
    const = lambda i: (0, 0)
    tri = jnp.asarray(np.tril(np.ones((tm, tm), np.float32), -1), dtype=BF16)
    kern = functools.partial(_outproj_kernel, width=width)
    return pl.pallas_call(
        kern,
        grid=(t // tm,),
        in_specs=[pl.BlockSpec((tm, width), row), pl.BlockSpec((tm, width), row),
                  pl.BlockSpec((tm, d), row), pl.BlockSpec((2 * width, d), const),
                  pl.BlockSpec((1, d), const), pl.BlockSpec((d, LANES), const),
                  pl.BlockSpec((1, LANES), const), pl.BlockSpec((tm, tm), const)],
        out_specs=[pl.BlockSpec((tm, d), row), pl.BlockSpec((tm, d), row),
                   pl.BlockSpec((tm, LANES), row), pl.BlockSpec((8, LANES), const)],
        out_shape=[jax.ShapeDtypeStruct((t, d), F32), jax.ShapeDtypeStruct((t, d), F32),
                   jax.ShapeDtypeStruct((t, LANES), F32), jax.ShapeDtypeStruct((8, LANES), F32)],
        scratch_shapes=[pltpu.VMEM((8, LANES), F32)],
        compiler_params=pltpu.CompilerParams(
            dimension_semantics=("arbitrary",), vmem_limit_bytes=40 * 1024 * 1024),
        name="outproj",
    )(o_sb, o_hg, h, w_out, g_ffn, w_r, b_r, tri)


def _row_copy(src, dst, sem):
    return pltpu.make_async_copy(src, dst, sem)


def _dispatch_kernel(d1_ref, d2_ref, zs_ref, zn_ref, m_hbm, xs_hbm, zero_ref, stage_ref, sems, lsems, zsem,
                     *, tile, n_tiles):
    i = pl.program_id(0)
    slot = i % 3

    def tile_wait(s):
        _row_copy(m_hbm.at[pl.ds(0, 2 * tile), :], xs_hbm.at[pl.ds(0, 2 * tile), :], sems.at[s]).wait()

    def tile_load(step, s):
        rows = pl.ds(pl.multiple_of(step * tile, tile), tile)
        return _row_copy(m_hbm.at[rows, :], stage_ref.at[s], lsems.at[s])

    @pl.when(i == 0)
    def _():
        tile_load(0, 0).start()
        zero_ref[...] = jnp.zeros_like(zero_ref)

        def zero_block(j):
            return _row_copy(zero_ref, xs_hbm.at[pl.ds(pl.multiple_of(zs_ref[j], MOE_BM), MOE_BM), :], zsem)

        for j in range(2 * N_EXPERTS):
            @pl.when(zn_ref[j] > 0)
            def _():
                zero_block(j).start()
        for j in range(2 * N_EXPERTS):
            @pl.when(zn_ref[j] > 0)
            def _():
                zero_block(j).wait()

    nxt = (i + 1) % 3

    @pl.when(i >= 2)
    def _():
        tile_wait(nxt)

    @pl.when(i + 1 < n_tiles)
    def _():
        tile_load(i + 1, nxt).start()

    tile_load(i, slot).wait()
    base = i * tile
    for r in range(tile):
        src = stage_ref.at[slot, pl.ds(r, 1), :]
        _row_copy(src, xs_hbm.at[pl.ds(d1_ref[base + r], 1), :], sems.at[slot]).start()
        _row_copy(src, xs_hbm.at[pl.ds(d2_ref[base + r], 1), :], sems.at[slot]).start()

    @pl.when(i == n_tiles - 1)
    def _():
        if n_tiles > 1:
            tile_wait((i + 2) % 3)
        tile_wait(slot)


def _dispatch(d1, d2, zero_start, zero_n, m, n_slots, tile=BLOCK):
    t, d = m.shape
    n_tiles = t // tile
    kern = functools.partial(_dispatch_kernel, tile=tile, n_tiles=n_tiles)
    grid_spec = pltpu.PrefetchScalarGridSpec(
        num_scalar_prefetch=4,
        grid=(n_tiles,),
        in_specs=[pl.BlockSpec(memory_space=pl.ANY)],
        out_specs=pl.BlockSpec(memory_space=pl.ANY),
        scratch_shapes=[pltpu.VMEM((MOE_BM, d), F32), pltpu.VMEM((3, tile, d), F32),
                        pltpu.SemaphoreType.DMA((3,)), pltpu.SemaphoreType.DMA((3,)),
                        pltpu.SemaphoreType.DMA(())],
    )
    return pl.pallas_call(
        kern,
        grid_spec=grid_spec,
        out_shape=jax.ShapeDtypeStruct((n_slots, d), F32),
        compiler_params=pltpu.CompilerParams(dimension_semantics=("arbitrary",)),
        name="dispatch",
    )(d1, d2, zero_start, zero_n, m)


def _expert_kernel(be_ref, nu_ref, xs_ref, wg_ref, wu_ref, wd_ref, y_ref):
    i = pl.program_id(0)

    @pl.when(i < nu_ref[0])
    def _():
        xs = xs_ref[...]
        gt = _dot(xs, wg_ref[...])
        hb = gt * jax.nn.sigmoid(gt) * _dot(xs, wu_ref[...])
        y_ref[...] = _dot(hb, wd_ref[...])

    @pl.when(i >= nu_ref[0])
    def _():
        y_ref[...] = jnp.zeros_like(y_ref)


def _experts(block_e, n_used, xs, w_gate, w_up, w_down):
    n_slots, d = xs.shape
    ff = w_gate.shape[-1]
    last = lambda i, be, nu: (jnp.minimum(i, nu[0] - 1), 0)
    grid_spec = pltpu.PrefetchScalarGridSpec(
        num_scalar_prefetch=2,
        grid=(n_slots // MOE_BM,),
        in_specs=[pl.BlockSpec((MOE_BM, d), last),
                  pl.BlockSpec((None, d, ff), lambda i, be, nu: (be[i], 0, 0)),
                  pl.BlockSpec((None, d, ff), lambda i, be, nu: (be[i], 0, 0)),
                  pl.BlockSpec((None, ff, d), lambda i, be, nu: (be[i], 0, 0))],
        out_specs=pl.BlockSpec((MOE_BM, d), lambda i, be, nu: (i, 0)),
    )
    return pl.pallas_call(
        _expert_kernel,
        grid_spec=grid_spec,
        out_shape=jax.ShapeDtypeStruct((n_slots, d), F32),
        compiler_params=pltpu.CompilerParams(
            dimension_semantics=("arbitrary",), vmem_limit_bytes=40 * 1024 * 1024),
        name="experts",
    )(block_e, n_used, xs, w_gate, w_up, w_down)


def _combine_kernel(d1_ref, d2_ref, h2_ref, route_ref, ys_hbm, g_ref, o_ref, ya_ref, yb_ref, sems,
                    *, tile, n_tiles):
    i = pl.program_id(0)
    slot = i % 2

    def issue(step, s):
        base = step * tile
        for r in range(tile):
            _row_copy(ys_hbm.at[pl.ds(d1_ref[base + r], 1), :], ya_ref.at[s, pl.ds(r, 1), :], sems.at[s]).start()
            _row_copy(ys_hbm.at[pl.ds(d2_ref[base + r], 1), :], yb_ref.at[s, pl.ds(r, 1), :], sems.at[s]).start()

    @pl.when(i == 0)
    def _():
        issue(0, 0)

    @pl.when(i + 1 < n_tiles)
    def _():
        issue(i + 1, 1 - slot)

    _row_copy(ys_hbm.at[pl.ds(0, tile), :], ya_ref.at[slot], sems.at[slot]).wait()
    _row_copy(ys_hbm.at[pl.ds(0, tile), :], yb_ref.at[slot], sems.at[slot]).wait()
    rec = route_ref[...]
    g1 = rec[:, R_G1:R_G1 + 1]
    g2 = rec[:, R_G2:R_G2 + 1]
    h = h2_ref[...] + (g1 * ya_ref[slot] + g2 * yb_ref[slot])
    o_ref[...] = _rms(h) * g_ref[...]


def _combine(d1, d2, h2, route, ys, g_final, tile=BLOCK):
    t, d = h2.shape
    n_tiles = t // tile
    kern = functools.partial(_combine_kernel, tile=tile, n_tiles=n_tiles)
    grid_spec = pltpu.PrefetchScalarGridSpec(
        num_scalar_prefetch=2,
        grid=(n_tiles,),
        in_specs=[pl.BlockSpec((tile, d), lambda i, a, b: (i, 0)),
                  pl.BlockSpec((tile, LANES), lambda i, a, b: (i, 0)),
                  pl.BlockSpec(memory_space=pl.ANY),
                  pl.BlockSpec((1, d), lambda i, a, b: (0, 0))],
        out_specs=pl.BlockSpec((tile, d), lambda i, a, b: (i, 0)),
        scratch_shapes=[pltpu.VMEM((2, tile, d), F32), pltpu.VMEM((2, tile, d), F32),
                        pltpu.SemaphoreType.DMA((2,))],
    )
    return pl.pallas_call(
        kern,
        grid_spec=grid_spec,
        out_shape=jax.ShapeDtypeStruct((t, d), F32),
        compiler_params=pltpu.CompilerParams(dimension_semantics=("arbitrary",)),
        name="combine",
    )(d1, d2, h2, route, ys, g_final)


def _segment_layout(route, counts_row, n_tok):
    counts = counts_row[N_GROUPS:N_GROUPS + N_EXPERTS].astype(jnp.int32)
    n_slots = n_tok * TOP_K + N_EXPERTS * MOE_BM
    n_blocks = n_slots // MOE_BM
    padded = (counts + MOE_BM - 1) // MOE_BM * MOE_BM
    padded_end = jnp.cumsum(padded)
    padded_start = padded_end - padded
    e1 = route[:, R_E1].astype(jnp.int32)
    e2 = route[:, R_E2].astype(jnp.int32)
    onehot = jnp.arange(N_EXPERTS, dtype=jnp.int32)[None, :]
    start1 = jnp.sum(jnp.where(e1[:, None] == onehot, padded_start[None, :], 0), axis=1)
    start2 = jnp.sum(jnp.where(e2[:, None] == onehot, padded_start[None, :], 0), axis=1)
    d1 = start1 + route[:, R_RANK1].astype(jnp.int32)
    d2 = start2 + route[:, R_RANK2].astype(jnp.int32)
    block_e = jnp.sum((jnp.arange(n_blocks, dtype=jnp.int32)[:, None] * MOE_BM >= padded_end[None, :])
                      .astype(jnp.int32), axis=1)
    block_e = jnp.minimum(block_e, N_EXPERTS - 1)
    n_used = (padded_end[-1] // MOE_BM).reshape(1)
    trailing = n_used[0] + jnp.arange(N_EXPERTS, dtype=jnp.int32)
    zero_start = jnp.concatenate([jnp.maximum(padded_end - MOE_BM, 0),
                                  jnp.minimum(trailing, n_blocks - 1) * MOE_BM])
    zero_flag = jnp.concatenate([counts, (trailing < n_blocks).astype(jnp.int32)])
    return d1, d2, block_e, n_used, zero_start, zero_flag, n_slots


def kernel(x, meta_tokens, lb_logits, g_mix, w_in, sb_gain, hg_gain, w_out, g_ffn, w_router_group,
           b_router_group, w_router_expert, b_router_expert, w_expert_gate, w_expert_up, w_expert_down,
           g_final):
    b, seq, d = x.shape
    depth = w_in.shape[0]
    assert depth == 1, "single-layer block"
    assert seq % (Q_SUB * BLOCK) == 0
    t = b * seq
    tm = 512
    layer = 0

    xr = x.reshape(t, d)
    prefix = jnp.concatenate([jnp.zeros((LEAD_PAD, d), x.dtype), meta_tokens.astype(x.dtype)], axis=0)
    prefix_mask = (jnp.arange(BLOCK) >= LEAD_PAD).astype(F32)[:, None]
    lower_bounds = jnp.cumsum(jax.nn.softmax(lb_logits.astype(F32), axis=0), axis=0)
    lb = lower_bounds[layer][None, :]
    g_mix_l = g_mix[layer][None, :]
    w_in_l = w_in[layer].astype(BF16)

    q, k0, k1, v0, v1, hq, hk, hv, lf, hg = _proj(xr, jnp.ones((t, 1), F32), g_mix_l, w_in_l, lb, tm)
    _, mk0, mk1, mv0, mv1, _, mhk, mhv, mlf, _ = _proj(prefix, prefix_mask, g_mix_l, w_in_l, lb, BLOCK)

    o_sb = _sb_attention(q, (k0, k1, v0, v1), (mk0, mk1, mv0, mv1), sb_gain[layer][None, :], b, seq)
    o_hg = _hgrn2(hq, hk, hv, lf, hg, (mhk, mhv, mlf), hg_gain[layer][None, :], b, seq)

    w_r = jnp.zeros((d, LANES), F32)
    w_r = w_r.at[:, :N_GROUPS].set(w_router_group[layer])
    w_r = w_r.at[:, N_GROUPS:N_GROUPS + N_EXPERTS].set(w_router_expert[layer])
    b_r = jnp.zeros((1, LANES), F32)
    b_r = b_r.at[0, :N_GROUPS].set(b_router_group[layer])
    b_r = b_r.at[0, N_GROUPS:N_GROUPS + N_EXPERTS].set(b_router_expert[layer])
    h2, m, route, counts = _outproj(o_sb, o_hg, xr, w_out[layer].astype(BF16), g_ffn[layer][None, :],
                                    w_r, b_r, tm)

    d1, d2, block_e, n_used, zero_start, zero_n, n_slots = _segment_layout(route, counts[0], t)
    xs = _dispatch(d1, d2, zero_start, zero_n, m, n_slots)
    ys = _experts(block_e, n_used, xs, w_expert_gate[layer], w_expert_up[layer], w_expert_down[layer])
    out = _combine(d1, d2, h2, route, ys, g_final[None, :])
    return out.reshape(b, seq, d)
```

```python
import functools

import numpy as np
import jax
import jax.numpy as jnp
from jax import lax
from jax.experimental import pallas as pl
from jax.experimental.pallas import tpu as pltpu

BLOCK = 128
N_META = 16
LEAD_PAD = BLOCK - N_META
SB_HEAD_DIM = 64
HG_HEAD_DIM = 128
SUB = 16
Q_SUB = 4
PAIRS = 1
DEAD_LOG_TAIL = -104.0
N_GROUPS = 4
EXPERTS_PER_GROUP = 8
N_EXPERTS = N_GROUPS * EXPERTS_PER_GROUP
TOP_K = 2
MOE_BM = 256
EPS = 1e-6
LANES = 128

F32 = jnp.float32
BF16 = jnp.bfloat16


def _dot(a, b):
    return jnp.dot(a, b, preferred_element_type=F32)


def _dot_nt(a, b):
    return lax.dot_general(a, b, (((1,), (1,)), ((), ())), preferred_element_type=F32)


def _dot_tn(a, b):
    return lax.dot_general(a, b, (((0,), (0,)), ((), ())), preferred_element_type=F32)


def _rms(x):
    return x * lax.rsqrt(jnp.mean(x * x, axis=-1, keepdims=True) + EPS)


def _proj_kernel(h_ref, mask_ref, g_ref, w_ref, lb_ref,
                 q_ref, k0_ref, k1_ref, v0_ref, v1_ref,
                 hq_ref, hk_ref, hv_ref, lf_ref, hg_ref, *, width):
    a = (_rms(h_ref[...]) * g_ref[...]).astype(BF16)

    def p(i):
        return _dot(a, w_ref[:, i * width:(i + 1) * width])

    lane = lax.broadcasted_iota(jnp.int32, (1, width), 1)
    head0 = (lane & (LANES - 1)) < SB_HEAD_DIM
    q_ref[...] = (p(0) * (SB_HEAD_DIM ** -0.5)).astype(BF16)
    k = p(1)
    k0_ref[...] = jnp.where(head0, k, 0.0).astype(BF16)
    k1_ref[...] = jnp.where(head0, 0.0, k).astype(BF16)
    v = p(2)
    v0_ref[...] = jnp.where(head0, v, 0.0).astype(BF16)
    v1_ref[...] = jnp.where(head0, 0.0, v).astype(BF16)
    hq = p(3)
    hq_ref[...] = hq * jax.nn.sigmoid(hq)
    f = p(4)
    lb = lb_ref[...]
    sig = jax.nn.sigmoid(f)
    lf_ref[...] = jnp.log(lb + (1.0 - lb) * sig)
    hk_ref[...] = mask_ref[...] * ((1.0 - lb) * jax.nn.sigmoid(-f))
    hv_ref[...] = p(5)
    g = p(6)
    hg_ref[...] = g * jax.nn.sigmoid(g)


def _proj(h, mask, g_mix, w_in, lb, tm):
    t, d = h.shape
    width = d // 2
    kern = functools.partial(_proj_kernel, width=width)
    row = lambda i: (i, 0)
    const = lambda i: (0, 0)
    outs = ([jax.ShapeDtypeStruct((t, width), BF16)] * 5
            + [jax.ShapeDtypeStruct((t, width), F32)] * 5)
    return pl.pallas_call(
        kern,
        grid=(t // tm,),
        in_specs=[pl.BlockSpec((tm, d), row), pl.BlockSpec((tm, 1), row),
                  pl.BlockSpec((1, d), const), pl.BlockSpec((d, 7 * width), const),
                  pl.BlockSpec((1, width), const)],
        out_specs=[pl.BlockSpec((tm, width), row)] * 10,
        out_shape=outs,
        compiler_params=pltpu.CompilerParams(
            dimension_semantics=("arbitrary",), vmem_limit_bytes=52 * 1024 * 1024),
        name="proj",
    )(h, mask, g_mix, w_in, lb)


def _sb_kernel(q_ref, k0_ref, k1_ref, v0_ref, v1_ref, mk0_ref, mk1_ref, mv0_ref, mv1_ref,
               tt_ref, gain_ref, o_ref, c_ref, acc_ref, zz_ref, w_ref):
    tq = Q_SUB * BLOCK
    base = pl.program_id(2) * Q_SUB
    c_ref[...] = jnp.zeros_like(c_ref)
    acc_ref[...] = jnp.zeros_like(acc_ref)

    def col_iota(rows):
        return lax.broadcasted_iota(jnp.int32, (rows, 2 * BLOCK), 1) & (BLOCK - 1)

    def lanes(p):
        return slice(p * LANES, (p + 1) * LANES)

    def k_real(p, kb):
        rows = pl.ds(pl.multiple_of(kb * BLOCK, BLOCK), BLOCK)
        return jnp.concatenate([k0_ref[rows, lanes(p)], k1_ref[rows, lanes(p)]], axis=0)

    def v_real(p, kb):
        rows = pl.ds(pl.multiple_of(kb * BLOCK, BLOCK), BLOCK)
        return jnp.concatenate([v0_ref[rows, lanes(p)], v1_ref[rows, lanes(p)]], axis=0)

    def k_real_or_prefix(p, kb):
        k_prefix = jnp.concatenate([mk0_ref[:, lanes(p)], mk1_ref[:, lanes(p)]], axis=0)
        return jnp.where(kb >= 0, k_real(p, jnp.maximum(kb, 0)), k_prefix)

    def v_prefix(p):
        return jnp.concatenate([mv0_ref[:, lanes(p)], mv1_ref[:, lanes(p)]], axis=0)

    def scores(p, k2, lo):
        zz_ref[p, lo:, :] = _dot_nt(q_ref[lo:, lanes(p)], k2)

    def values(p, v2, lo):
        acc_ref[p, lo:, :] += _dot(w_ref[p, lo:, :], v2)

    def step(mask, k_next, v_prev, lo=0, lo_next=0, lo_prev=0):
        for p in range(PAIRS):
            zz = zz_ref[p, lo:, :]
            if k_next is not None:
                scores(p, k_next(p), lo_next)
            if v_prev is not None:
                values(p, v_prev(p), lo_prev)
            neg_abs = lax.bitcast_convert_type(
                lax.bitcast_convert_type(zz, jnp.uint32) | jnp.uint32(0x80000000), F32)
            sp = jnp.maximum(zz, 0.0) + jnp.log(1.0 + jnp.exp(neg_abs))
            spm = sp if mask is None else jnp.where(mask, sp, 0.0)
            spb = spm.astype(BF16)
            tt = tt_ref[...]
            tail = jnp.concatenate(
                [_dot(spb[:, h * BLOCK:(h + 1) * BLOCK], tt) for h in range(2)], axis=1)
            c = c_ref[p, lo:, :]
            w = jnp.exp(zz - sp + tail + c)
            if mask is not None:
                w = jnp.where(mask, w, 0.0)
            w_ref[p, lo:, :] = w.astype(BF16)
            tot0 = jnp.sum(spm[:, :BLOCK], axis=-1, keepdims=True)
            tot1 = jnp.sum(spm[:, BLOCK:], axis=-1, keepdims=True)
            c_ref[p, lo:, :] = c - jnp.concatenate([jnp.broadcast_to(tot0, (tq - lo, BLOCK)),
                                                    jnp.broadcast_to(tot1, (tq - lo, BLOCK))], axis=1)

    top = base + Q_SUB - 1
    for p in range(PAIRS):
        scores(p, k_real(p, top), (Q_SUB - 1) * BLOCK)
    for jj in reversed(range(Q_SUB)):
        kb = base + jj
        lo = jj * BLOCK
        step(col_iota(tq - lo) < lax.broadcasted_iota(jnp.int32, (tq - lo, 2 * BLOCK), 0),
             (lambda p, kb=kb: k_real(p, kb - 1)) if jj > 0 else (lambda p, kb=kb: k_real_or_prefix(p, kb - 1)),
             (lambda p, kb=kb: v_real(p, kb + 1)) if jj < Q_SUB - 1 else None,
             lo=lo, lo_next=max(lo - BLOCK, 0), lo_prev=lo + BLOCK)

    def alive():
        return jnp.max(c_ref[...]) > DEAD_LOG_TAIL

    def cond(carry):
        kb, live = carry
        return jnp.logical_and(kb >= 0, live)

    def body(carry):
        kb, _ = carry
        step(None, lambda p: k_real_or_prefix(p, kb - 1), lambda p: v_real(p, kb + 1))
        return kb - 1, alive()

    kb, live = lax.while_loop(cond, body, (base - 1, alive()))
    for p in range(PAIRS):
        values(p, v_real(p, kb + 1), 0)

    @pl.when(jnp.logical_and(kb < 0, live))
    def _():
        step(col_iota(tq) >= LEAD_PAD, None, None)
        for p in range(PAIRS):
            values(p, v_prefix(p), 0)

    head0 = lax.broadcasted_iota(jnp.int32, (tq, LANES), 1) < SB_HEAD_DIM
    for p in range(PAIRS):
        o = acc_ref[p]
        o2 = o * o
        s0 = jnp.sum(jnp.where(head0, o2, 0.0), axis=-1, keepdims=True) * (1.0 / SB_HEAD_DIM)
        s1 = jnp.sum(jnp.where(head0, 0.0, o2), axis=-1, keepdims=True) * (1.0 / SB_HEAD_DIM)
        r = jnp.where(head0, lax.rsqrt(s0 + EPS), lax.rsqrt(s1 + EPS))
        o_ref[:, lanes(p)] = (o * r * gain_ref[:, lanes(p)]).astype(o_ref.dtype)


def _sb_tail_matrix():
    j = np.arange(BLOCK)[:, None]
    s = np.arange(BLOCK)[None, :]
    return jnp.asarray(np.where(j > s, -1.0, 0.0), dtype=BF16)


def _sb_attention(q, kv, kv_meta, gain, b, seq):
    t, width = q.shape
    nqt = seq // (Q_SUB * BLOCK)
    gw = PAIRS * LANES
    ngroup = width // gw
    kv3 = [a.reshape(b, seq, width) for a in kv]
    tq = Q_SUB * BLOCK
    kv_spec = pl.BlockSpec((None, seq, gw), lambda bi, hp, qt: (bi, 0, hp))
    meta_spec = pl.BlockSpec((BLOCK, gw), lambda bi, hp, qt: (0, hp))
    return pl.pallas_call(
        _sb_kernel,
        grid=(b, ngroup, nqt),
        in_specs=[pl.BlockSpec((tq, gw), lambda bi, hp, qt: (bi * nqt + qt, hp)),
                  kv_spec, kv_spec, kv_spec, kv_spec,
                  meta_spec, meta_spec, meta_spec, meta_spec,
                  pl.BlockSpec((BLOCK, BLOCK), lambda bi, hp, qt: (0, 0)),
                  pl.BlockSpec((1, gw), lambda bi, hp, qt: (0, hp))],
        out_specs=pl.BlockSpec((tq, gw), lambda bi, hp, qt: (bi * nqt + qt, hp)),
        out_shape=jax.ShapeDtypeStruct((t, width), BF16),
        scratch_shapes=[pltpu.VMEM((PAIRS, tq, 2 * BLOCK), F32), pltpu.VMEM((PAIRS, tq, LANES), F32),
                        pltpu.VMEM((PAIRS, tq, 2 * BLOCK), F32), pltpu.VMEM((PAIRS, tq, 2 * BLOCK), BF16)],
        compiler_params=pltpu.CompilerParams(
            dimension_semantics=("arbitrary", "arbitrary", "arbitrary"),
            vmem_limit_bytes=40 * 1024 * 1024),
        name="sb_attn",
    )(q, *kv3, *kv_meta, _sb_tail_matrix(), gain)


def _hg_block(hq_ref, hk_ref, hv_ref, lf_ref, lmat_ref, st_ref, a_ref, qt_ref, kh_ref, oacc_ref, f_ref, n_heads,
              with_output):
    lf = lf_ref[...]
    h1 = lf.astype(BF16)
    r1 = lf - h1.astype(F32)
    h2 = r1.astype(BF16)
    h3 = (r1 - h2.astype(F32)).astype(BF16)
    lmat = lmat_ref[...]
    cs = _dot(lmat, h1) + _dot(lmat, h2) + _dot(lmat, h3)
    a = cs[:BLOCK]
    alast = cs[BLOCK:]
    a_ref[...] = alast
    kh_ref[...] = hk_ref[...] * jnp.exp(alast - a)
    if with_output:
        f_ref[...] = jnp.exp(lf)
        qt_ref[...] = hq_ref[...] * jnp.exp(a)
    ridx = lax.broadcasted_iota(jnp.int32, (SUB, 1), 0)

    for i in range(BLOCK // SUB):
        r0 = i * SUB
        rows = pl.ds(r0, SUB)
        for hd in range(n_heads):
            cols = slice(hd * HG_HEAD_DIM, (hd + 1) * HG_HEAD_DIM)
            v_i = hv_ref[rows, cols]
            st = st_ref[hd]
            if with_output:
                o_i = _dot_nt(qt_ref[rows, cols], st)
                e = hq_ref[rows, cols]
                for s in reversed(range(SUB)):
                    if s < SUB - 1:
                        e = jnp.where(ridx > s, e * f_ref[pl.ds(r0 + s + 1, 1), cols], e)
                    sc = jnp.sum(e * hk_ref[pl.ds(r0 + s, 1), cols], axis=-1, keepdims=True)
                    o_i = o_i + jnp.where(ridx >= s, sc, 0.0) * hv_ref[pl.ds(r0 + s, 1), cols]
                oacc_ref[rows, cols] = o_i
            decay = jnp.exp(a_ref[pl.ds(r0, 1), cols])
            st_ref[hd] = st * decay + _dot_tn(v_i, kh_ref[rows, cols])


def _hg_kernel(hq_ref, hk_ref, hv_ref, lf_ref, gate_ref, mk_ref, mv_ref, mlf_ref, gain_ref, lmat_ref, o_ref,
               st_ref, a_ref, qt_ref, kh_ref, oacc_ref, f_ref, *, n_heads):
    scratch = (st_ref, a_ref, qt_ref, kh_ref, oacc_ref, f_ref)

    @pl.when(pl.program_id(1) == 0)
    def _():
        st_ref[...] = jnp.zeros_like(st_ref)
        _hg_block(None, mk_ref, mv_ref, mlf_ref, lmat_ref, *scratch, n_heads, with_output=False)

    _hg_block(hq_ref, hk_ref, hv_ref, lf_ref, lmat_ref, *scratch, n_heads, with_output=True)
    for hd in range(n_heads):
        cols = slice(hd * HG_HEAD_DIM, (hd + 1) * HG_HEAD_DIM)
        o = _rms(oacc_ref[:, cols]) * gain_ref[:, cols] * gate_ref[:, cols]
        o_ref[:, cols] = o.astype(o_ref.dtype)


def _hg_cumsum_matrix():
    t = np.arange(BLOCK)[:, None]
    s = np.arange(BLOCK)[None, :]
    same = (t // SUB) == (s // SUB)
    incl = np.where(same & (s <= t), 1.0, 0.0)
    full = np.where(same, 1.0, 0.0)
    return jnp.asarray(np.concatenate([incl, full], axis=0), dtype=BF16)


def _hgrn2(hq, hk, hv, lf, gate, meta, gain, b, seq):
    t, width = hq.shape
    nc = seq // BLOCK
    n_heads = width // HG_HEAD_DIM
    blk = pl.BlockSpec((BLOCK, width), lambda bi, ci: (bi * nc + ci, 0))
    mblk = pl.BlockSpec((BLOCK, width), lambda bi, ci: (0, 0))
    kern = functools.partial(_hg_kernel, n_heads=n_heads)
    return pl.pallas_call(
        kern,
        grid=(b, nc),
        in_specs=[blk, blk, blk, blk, blk, mblk, mblk, mblk,
                  pl.BlockSpec((1, width), lambda bi, ci: (0, 0)),
                  pl.BlockSpec((2 * BLOCK, BLOCK), lambda bi, ci: (0, 0))],
        out_specs=blk,
        out_shape=jax.ShapeDtypeStruct((t, width), BF16),
        scratch_shapes=[pltpu.VMEM((n_heads, HG_HEAD_DIM, HG_HEAD_DIM), F32),
                        pltpu.VMEM((BLOCK, width), F32), pltpu.VMEM((BLOCK, width), F32),
                        pltpu.VMEM((BLOCK, width), F32), pltpu.VMEM((BLOCK, width), F32),
                        pltpu.VMEM((BLOCK, width), F32)],
        compiler_params=pltpu.CompilerParams(dimension_semantics=("arbitrary", "arbitrary")),
        name="hgrn2",
    )(hq, hk, hv, lf, gate, *meta, gain, _hg_cumsum_matrix())


R_E1, R_E2, R_RANK1, R_RANK2, R_G1, R_G2 = range(6)


def _outproj_kernel(osb_ref, ohg_ref, h_ref, w_ref, g_ref, wr_ref, br_ref, tri_ref,
                    h2_ref, route_ref, cnt_ref, carry_ref, *, width):
    @pl.when(pl.program_id(0) == 0)
    def _():
        carry_ref[...] = jnp.zeros_like(carry_ref)

    h2 = h_ref[...] + _dot(osb_ref[...], w_ref[:width, :]) + _dot(ohg_ref[...], w_ref[width:, :])
    h2_ref[...] = h2
    m = _rms(h2) * g_ref[...]
    lg = _dot(m, wr_ref[...]) + br_ref[...]
    tm = lg.shape[0]
    lane = lax.broadcasted_iota(jnp.int32, (tm, LANES), 1)
    neg = jnp.float32(-1e30)

    def first_argmax(vals):
        vmax = jnp.max(vals, axis=-1, keepdims=True)
        idx = jnp.min(jnp.where(vals == vmax, lane, LANES), axis=-1, keepdims=True)
        return vmax, idx

    is_grp = lane < N_GROUPS
    gl = jnp.where(is_grp, lg, neg)
    gmax, gidx = first_argmax(gl)
    p_grp = 1.0 / jnp.sum(jnp.where(is_grp, jnp.exp(gl - gmax), 0.0), axis=-1, keepdims=True)
    lo = N_GROUPS + gidx * EXPERTS_PER_GROUP
    el = jnp.where((lane >= lo) & (lane < lo + EXPERTS_PER_GROUP), lg, neg)
    v1, i1 = first_argmax(el)
    sel1 = lane == i1
    v2, i2 = first_argmax(jnp.where(sel1, neg, el))
    sel2 = lane == i2
    dlt = jnp.exp(v2 - v1)
    g1 = p_grp / (1.0 + dlt)
    g2 = g1 * dlt

    chosen = jnp.where(sel1 | sel2, 1.0, 0.0)
    carry = carry_ref[0:1, :]
    before = _dot(tri_ref[...], chosen.astype(BF16)) + carry
    r1 = jnp.sum(jnp.where(sel1, before, 0.0), axis=-1, keepdims=True)
    r2 = jnp.sum(jnp.where(sel2, before, 0.0), axis=-1, keepdims=True)
    carry = carry + jnp.sum(chosen, axis=0, keepdims=True)
    carry_ref[0:1, :] = carry
    cnt_ref[...] = jnp.broadcast_to(carry, cnt_ref.shape)

    rec = jnp.zeros((tm, LANES), F32)
    for ln, val in ((R_E1, (i1 - N_GROUPS).astype(F32)), (R_E2, (i2 - N_GROUPS).astype(F32)),
                    (R_RANK1, r1), (R_RANK2, r2), (R_G1, g1), (R_G2, g2)):
        rec = jnp.where(lane == ln, val, rec)
    route_ref[...] = rec


def _outproj(o_sb, o_hg, h, w_out, g_ffn, w_r, b_r, tm):
    t, d = h.shape
    width = o_sb.shape[1]
    row = lambda i: (i, 0)
    const = lambda i: (0, 0)
    tri = jnp.asarray(np.tril(np.ones((tm, tm), np.float32), -1), dtype=BF16)
    kern = functools.partial(_outproj_kernel, width=width)
    return pl.pallas_call(
        kern,
        grid=(t // tm,),
        in_specs=[pl.BlockSpec((tm, width), row), pl.BlockSpec((tm, width), row),
                  pl.BlockSpec((tm, d), row), pl.BlockSpec((2 * width, d), const),
                  pl.BlockSpec((1, d), const), pl.BlockSpec((d, LANES), const),
                  pl.BlockSpec((1, LANES), const), pl.BlockSpec((tm, tm), const)],
        out_specs=[pl.BlockSpec((tm, d), row),
                   pl.BlockSpec((tm, LANES), row), pl.BlockSpec((8, LANES), const)],
        out_shape=[jax.ShapeDtypeStruct((t, d), F32),
                   jax.ShapeDtypeStruct((t, LANES), F32), jax.ShapeDtypeStruct((8, LANES), F32)],
        scratch_shapes=[pltpu.VMEM((8, LANES), F32)],
        compiler_params=pltpu.CompilerParams(
            dimension_semantics=("arbitrary",), vmem_limit_bytes=40 * 1024 * 1024),
        name="outproj",
    )(o_sb, o_hg, h, w_out, g_ffn, w_r, b_r, tri)


def _row_copy(src, dst, sem):
    return pltpu.make_async_copy(src, dst, sem)


def _dispatch_kernel(d1_ref, d2_ref, zs_ref, zn_ref, m_hbm, g_ref, xs_hbm, zero_ref, stage_ref, sems, lsems, zsem,
                     *, tile, n_tiles):
    i = pl.program_id(0)
    slot = i % 3

    def tile_wait(s):
        _row_copy(m_hbm.at[pl.ds(0, 2 * tile), :], xs_hbm.at[pl.ds(0, 2 * tile), :], sems.at[s]).wait()

    def tile_load(step, s):
        rows = pl.ds(pl.multiple_of(step * tile, tile), tile)
        return _row_copy(m_hbm.at[rows, :], stage_ref.at[s], lsems.at[s])

    @pl.when(i == 0)
    def _():
        tile_load(0, 0).start()
        zero_ref[...] = jnp.zeros_like(zero_ref)

        def zero_block(j):
            return _row_copy(zero_ref, xs_hbm.at[pl.ds(pl.multiple_of(zs_ref[j], MOE_BM), MOE_BM), :], zsem)

        for j in range(2 * N_EXPERTS):
            @pl.when(zn_ref[j] > 0)
            def _():
                zero_block(j).start()
        for j in range(2 * N_EXPERTS):
            @pl.when(zn_ref[j] > 0)
            def _():
                zero_block(j).wait()

    nxt = (i + 1) % 3

    @pl.when(i >= 2)
    def _():
        tile_wait(nxt)

    @pl.when(i + 1 < n_tiles)
    def _():
        tile_load(i + 1, nxt).start()

    tile_load(i, slot).wait()
    stage_ref[slot] = _rms(stage_ref[slot]) * g_ref[...]
    base = i * tile
    for r in range(tile):
        src = stage_ref.at[slot, pl.ds(r, 1), :]
        _row_copy(src, xs_hbm.at[pl.ds(d1_ref[base + r], 1), :], sems.at[slot]).start(priority=0)
        _row_copy(src, xs_hbm.at[pl.ds(d2_ref[base + r], 1), :], sems.at[slot]).start(priority=1)

    @pl.when(i == n_tiles - 1)
    def _():
        if n_tiles > 1:
            tile_wait((i + 2) % 3)
        tile_wait(slot)


def _dispatch(d1, d2, zero_start, zero_n, h2, g_ffn, n_slots, tile=BLOCK):
    t, d = h2.shape
    n_tiles = t // tile
    kern = functools.partial(_dispatch_kernel, tile=tile, n_tiles=n_tiles)
    grid_spec = pltpu.PrefetchScalarGridSpec(
        num_scalar_prefetch=4,
        grid=(n_tiles,),
        in_specs=[pl.BlockSpec(memory_space=pl.ANY),
                  pl.BlockSpec((1, d), lambda i, a, b, c, e: (0, 0))],
        out_specs=pl.BlockSpec(memory_space=pl.ANY),
        scratch_shapes=[pltpu.VMEM((MOE_BM, d), F32), pltpu.VMEM((3, tile, d), F32),
                        pltpu.SemaphoreType.DMA((3,)), pltpu.SemaphoreType.DMA((3,)),
                        pltpu.SemaphoreType.DMA(())],
    )
    return pl.pallas_call(
        kern,
        grid_spec=grid_spec,
        out_shape=jax.ShapeDtypeStruct((n_slots, d), F32),
        compiler_params=pltpu.CompilerParams(dimension_semantics=("arbitrary",)),
        name="dispatch",
    )(d1, d2, zero_start, zero_n, h2, g_ffn)


def _expert_kernel(be_ref, nu_ref, xs_ref, wg_ref, wu_ref, wd_ref, y_ref):
    i = pl.program_id(0)

    @pl.when(i < nu_ref[0])
    def _():
        xs = xs_ref[...]
        gt = _dot(xs, wg_ref[...])
        hb = gt * jax.nn.sigmoid(gt) * _dot(xs, wu_ref[...])
        y_ref[...] = _dot(hb, wd_ref[...])

    @pl.when(i >= nu_ref[0])
    def _():
        y_ref[...] = jnp.zeros_like(y_ref)


def _experts(block_e, n_used, xs, w_gate, w_up, w_down):
    n_slots, d = xs.shape
    ff = w_gate.shape[-1]
    last = lambda i, be, nu: (jnp.minimum(i, nu[0] - 1), 0)
    grid_spec = pltpu.PrefetchScalarGridSpec(
        num_scalar_prefetch=2,
        grid=(n_slots // MOE_BM,),
        in_specs=[pl.BlockSpec((MOE_BM, d), last),
                  pl.BlockSpec((None, d, ff), lambda i, be, nu: (be[i], 0, 0)),
                  pl.BlockSpec((None, d, ff), lambda i, be, nu: (be[i], 0, 0)),
                  pl.BlockSpec((None, ff, d), lambda i, be, nu: (be[i], 0, 0))],
        out_specs=pl.BlockSpec((MOE_BM, d), lambda i, be, nu: (i, 0)),
    )
    return pl.pallas_call(
        _expert_kernel,
        grid_spec=grid_spec,
        out_shape=jax.ShapeDtypeStruct((n_slots, d), F32),
        compiler_params=pltpu.CompilerParams(
            dimension_semantics=("arbitrary",), vmem_limit_bytes=40 * 1024 * 1024),
        name="experts",
    )(block_e, n_used, xs, w_gate, w_up, w_down)


def _combine_kernel(d1_ref, d2_ref, h2_ref, route_ref, ys_hbm, g_ref, o_ref, ya_ref, yb_ref, sems,
                    *, tile, n_tiles):
    i = pl.program_id(0)
    slot = i % 2

    def issue(step, s):
        base = step * tile
        for r in range(tile):
            _row_copy(ys_hbm.at[pl.ds(d1_ref[base + r], 1), :], ya_ref.at[s, pl.ds(r, 1), :],
                      sems.at[s]).start(priority=0)
            _row_copy(ys_hbm.at[pl.ds(d2_ref[base + r], 1), :], yb_ref.at[s, pl.ds(r, 1), :],
                      sems.at[s]).start(priority=1)

    @pl.when(i == 0)
    def _():
        issue(0, 0)

    @pl.when(i + 1 < n_tiles)
    def _():
        issue(i + 1, 1 - slot)

    _row_copy(ys_hbm.at[pl.ds(0, tile), :], ya_ref.at[slot], sems.at[slot]).wait()
    _row_copy(ys_hbm.at[pl.ds(0, tile), :], yb_ref.at[slot], sems.at[slot]).wait()
    rec = route_ref[...]
    g1 = rec[:, R_G1:R_G1 + 1]
    g2 = rec[:, R_G2:R_G2 + 1]
    h = h2_ref[...] + (g1 * ya_ref[slot] + g2 * yb_ref[slot])
    o_ref[...] = _rms(h) * g_ref[...]


def _combine(d1, d2, h2, route, ys, g_final, tile=BLOCK):
    t, d = h2.shape
    n_tiles = t // tile
    kern = functools.partial(_combine_kernel, tile=tile, n_tiles=n_tiles)
    grid_spec = pltpu.PrefetchScalarGridSpec(
        num_scalar_prefetch=2,
        grid=(n_tiles,),
        in_specs=[pl.BlockSpec((tile, d), lambda i, a, b: (i, 0)),
                  pl.BlockSpec((tile, LANES), lambda i, a, b: (i, 0)),
                  pl.BlockSpec(memory_space=pl.ANY),
                  pl.BlockSpec((1, d), lambda i, a, b: (0, 0))],
        out_specs=pl.BlockSpec((tile, d), lambda i, a, b: (i, 0)),
        scratch_shapes=[pltpu.VMEM((2, tile, d), F32), pltpu.VMEM((2, tile, d), F32),
                        pltpu.SemaphoreType.DMA((2,))],
    )
    return pl.pallas_call(
        kern,
        grid_spec=grid_spec,
        out_shape=jax.ShapeDtypeStruct((t, d), F32),
        compiler_params=pltpu.CompilerParams(dimension_semantics=("arbitrary",)),
        name="combine",
    )(d1, d2, h2, route, ys, g_final)


def _segment_layout(route, counts_row, n_tok):
    counts = counts_row[N_GROUPS:N_GROUPS + N_EXPERTS].astype(jnp.int32)
    n_slots = n_tok * TOP_K + N_EXPERTS * MOE_BM
    n_blocks = n_slots // MOE_BM
    padded = (counts + MOE_BM - 1) // MOE_BM * MOE_BM
    padded_end = jnp.cumsum(padded)
    padded_start = padded_end - padded
    e1 = route[:, R_E1].astype(jnp.int32)
    e2 = route[:, R_E2].astype(jnp.int32)
    d1 = jnp.take(padded_start, e1) + route[:, R_RANK1].astype(jnp.int32)
    d2 = jnp.take(padded_start, e2) + route[:, R_RANK2].astype(jnp.int32)
    block_e = jnp.sum((jnp.arange(n_blocks, dtype=jnp.int32)[:, None] * MOE_BM >= padded_end[None, :])
                      .astype(jnp.int32), axis=1)
    block_e = jnp.minimum(block_e, N_EXPERTS - 1)
    n_used = (padded_end[-1] // MOE_BM).reshape(1)
    trailing = n_used[0] + jnp.arange(N_EXPERTS, dtype=jnp.int32)
    zero_start = jnp.concatenate([jnp.maximum(padded_end - MOE_BM, 0),
                                  jnp.minimum(trailing, n_blocks - 1) * MOE_BM])
    zero_flag = jnp.concatenate([counts, (trailing < n_blocks).astype(jnp.int32)])
    return d1, d2, block_e, n_used, zero_start, zero_flag, n_slots


def kernel(x, meta_tokens, lb_logits, g_mix, w_in, sb_gain, hg_gain, w_out, g_ffn, w_router_group,
           b_router_group, w_router_expert, b_router_expert, w_expert_gate, w_expert_up, w_expert_down,
           g_final):
    b, seq, d = x.shape
    depth = w_in.shape[0]
    assert depth == 1, "single-layer block"
    assert seq % (Q_SUB * BLOCK) == 0
    t = b * seq
    tm = 512
    layer = 0

    xr = x.reshape(t, d)
    prefix = jnp.concatenate([jnp.zeros((LEAD_PAD, d), x.dtype), meta_tokens.astype(x.dtype)], axis=0)
    prefix_mask = (jnp.arange(BLOCK) >= LEAD_PAD).astype(F32)[:, None]
    lower_bounds = jnp.cumsum(jax.nn.softmax(lb_logits.astype(F32), axis=0), axis=0)
    lb = lower_bounds[layer][None, :]
    g_mix_l = g_mix[layer][None, :]
    w_in_l = w_in[layer].astype(BF16)

    q, k0, k1, v0, v1, hq, hk, hv, lf, hg = _proj(xr, jnp.ones((t, 1), F32), g_mix_l, w_in_l, lb, tm)
    _, mk0, mk1, mv0, mv1, _, mhk, mhv, mlf, _ = _proj(prefix, prefix_mask, g_mix_l, w_in_l, lb, BLOCK)

    o_sb = _sb_attention(q, (k0, k1, v0, v1), (mk0, mk1, mv0, mv1), sb_gain[layer][None, :], b, seq)
    o_hg = _hgrn2(hq, hk, hv, lf, hg, (mhk, mhv, mlf), hg_gain[layer][None, :], b, seq)

    w_r = jnp.zeros((d, LANES), F32)
    w_r = w_r.at[:, :N_GROUPS].set(w_router_group[layer])
    w_r = w_r.at[:, N_GROUPS:N_GROUPS + N_EXPERTS].set(w_router_expert[layer])
    b_r = jnp.zeros((1, LANES), F32)
    b_r = b_r.at[0, :N_GROUPS].set(b_router_group[layer])
    b_r = b_r.at[0, N_GROUPS:N_GROUPS + N_EXPERTS].set(b_router_expert[layer])
    g_ffn_l = g_ffn[layer][None, :]
    h2, route, counts = _outproj(o_sb, o_hg, xr, w_out[layer].astype(BF16), g_ffn_l, w_r, b_r, tm)

    d1, d2, block_e, n_used, zero_start, zero_n, n_slots = _segment_layout(route, counts[0], t)
    xs = _dispatch(d1, d2, zero_start, zero_n, h2, g_ffn_l, n_slots)
    ys = _experts(block_e, n_used, xs, w_expert_gate[layer], w_expert_up[layer], w_expert_down[layer])
    out = _combine(d1, d2, h2, route, ys, g_final[None, :])
    return out.reshape(b, seq, d)
```

```python
import functools

import numpy as np
import jax
import jax.numpy as jnp
from jax import lax
from jax.experimental import pallas as pl
from jax.experimental.pallas import tpu as pltpu

BLOCK = 128
N_META = 16
LEAD_PAD = BLOCK - N_META
SB_HEAD_DIM = 64
HG_HEAD_DIM = 128
SUB = 16
Q_SUB = 4
PAIRS = 1
DEAD_LOG_TAIL = -104.0
N_GROUPS = 4
EXPERTS_PER_GROUP = 8
N_EXPERTS = N_GROUPS * EXPERTS_PER_GROUP
TOP_K = 2
MOE_BM = 256
EPS = 1e-6
LANES = 128

F32 = jnp.float32
BF16 = jnp.bfloat16


def _dot(a, b):
    return jnp.dot(a, b, preferred_element_type=F32)


def _dot_nt(a, b):
    return lax.dot_general(a, b, (((1,), (1,)), ((), ())), preferred_element_type=F32)


def _dot_tn(a, b):
    return lax.dot_general(a, b, (((0,), (0,)), ((), ())), preferred_element_type=F32)


def _rms(x):
    return x * lax.rsqrt(jnp.mean(x * x, axis=-1, keepdims=True) + EPS)


def _tiles_load(ref, n, chunks):
    return jnp.concatenate([ref[pl.ds(c, n, stride=chunks), :] for c in range(chunks)], axis=1)


def _tiles_store(ref, x, chunks):
    n = x.shape[0]
    for c in range(chunks):
        ref[pl.ds(c, n, stride=chunks), :] = x[:, c * LANES:(c + 1) * LANES]


def _proj_kernel(h_ref, mask_ref, g_ref, w_ref, lb_ref,
                 q_ref, k0_ref, k1_ref, v0_ref, v1_ref,
                 hq_ref, hk_ref, hv_ref, lf_ref, hg_ref, *, width):
    a = (_rms(h_ref[...]) * g_ref[...]).astype(BF16)

    def p(i):
        return _dot(a, w_ref[:, i * width:(i + 1) * width])

    lane = lax.broadcasted_iota(jnp.int32, (1, width), 1)
    head0 = (lane & (LANES - 1)) < SB_HEAD_DIM
    q_ref[...] = (p(0) * (SB_HEAD_DIM ** -0.5)).astype(BF16)
    k = p(1)
    k0_ref[...] = jnp.where(head0, k, 0.0).astype(BF16)
    k1_ref[...] = jnp.where(head0, 0.0, k).astype(BF16)
    v = p(2)
    v0_ref[...] = jnp.where(head0, v, 0.0).astype(BF16)
    v1_ref[...] = jnp.where(head0, 0.0, v).astype(BF16)
    hq = p(3)
    hq_ref[...] = hq * jax.nn.sigmoid(hq)
    f = p(4)
    lb = lb_ref[...]
    sig = jax.nn.sigmoid(f)
    lf_ref[...] = jnp.log(lb + (1.0 - lb) * sig)
    hk_ref[...] = mask_ref[...] * ((1.0 - lb) * jax.nn.sigmoid(-f))
    hv_ref[...] = p(5)
    g = p(6)
    hg_ref[...] = g * jax.nn.sigmoid(g)


def _proj(h, mask, g_mix, w_in, lb, tm):
    t, d = h.shape
    width = d // 2
    kern = functools.partial(_proj_kernel, width=width)
    row = lambda i: (i, 0)
    const = lambda i: (0, 0)
    outs = ([jax.ShapeDtypeStruct((t, width), BF16)] * 5
            + [jax.ShapeDtypeStruct((t, width), F32)] * 5)
    return pl.pallas_call(
        kern,
        grid=(t // tm,),
        in_specs=[pl.BlockSpec((tm, d), row), pl.BlockSpec((tm, 1), row),
                  pl.BlockSpec((1, d), const), pl.BlockSpec((d, 7 * width), const),
                  pl.BlockSpec((1, width), const)],
        out_specs=[pl.BlockSpec((tm, width), row)] * 10,
        out_shape=outs,
        compiler_params=pltpu.CompilerParams(
            dimension_semantics=("arbitrary",), vmem_limit_bytes=52 * 1024 * 1024),
        name="proj",
    )(h, mask, g_mix, w_in, lb)


def _sb_kernel(q_ref, k0_ref, k1_ref, v0_ref, v1_ref, mk0_ref, mk1_ref, mv0_ref, mv1_ref,
               tt_ref, gain_ref, o_ref, c_ref, acc_ref, zz_ref, w_ref):
    tq = Q_SUB * BLOCK
    base = pl.program_id(2) * Q_SUB
    c_ref[...] = jnp.zeros_like(c_ref)
    acc_ref[...] = jnp.zeros_like(acc_ref)

    def col_iota(rows):
        return lax.broadcasted_iota(jnp.int32, (rows, 2 * BLOCK), 1) & (BLOCK - 1)

    def lanes(p):
        return slice(p * LANES, (p + 1) * LANES)

    def k_real(p, kb):
        rows = pl.ds(pl.multiple_of(kb * BLOCK, BLOCK), BLOCK)
        return jnp.concatenate([k0_ref[rows, lanes(p)], k1_ref[rows, lanes(p)]], axis=0)

    def v_real(p, kb):
        rows = pl.ds(pl.multiple_of(kb * BLOCK, BLOCK), BLOCK)
        return jnp.concatenate([v0_ref[rows, lanes(p)], v1_ref[rows, lanes(p)]], axis=0)

    def k_real_or_prefix(p, kb):
        k_prefix = jnp.concatenate([mk0_ref[:, lanes(p)], mk1_ref[:, lanes(p)]], axis=0)
        return jnp.where(kb >= 0, k_real(p, jnp.maximum(kb, 0)), k_prefix)

    def v_prefix(p):
        return jnp.concatenate([mv0_ref[:, lanes(p)], mv1_ref[:, lanes(p)]], axis=0)

    def scores(p, k2, lo):
        zz_ref[p, lo:, :] = _dot_nt(q_ref[lo:, lanes(p)], k2)

    def values(p, v2, lo):
        acc_ref[p, lo:, :] += _dot(w_ref[p, lo:, :], v2)

    def step(mask, k_next, v_prev, lo=0, lo_next=0, lo_prev=0):
        for p in range(PAIRS):
            zz = zz_ref[p, lo:, :]
            if k_next is not None:
                scores(p, k_next(p), lo_next)
            if v_prev is not None:
                values(p, v_prev(p), lo_prev)
            neg_abs = lax.bitcast_convert_type(
                lax.bitcast_convert_type(zz, jnp.uint32) | jnp.uint32(0x80000000), F32)
            sp = jnp.maximum(zz, 0.0) + jnp.log(1.0 + jnp.exp(neg_abs))
            spm = sp if mask is None else jnp.where(mask, sp, 0.0)
            spb = spm.astype(BF16)
            tt = tt_ref[...]
            tail = jnp.concatenate(
                [_dot(spb[:, h * BLOCK:(h + 1) * BLOCK], tt) for h in range(2)], axis=1)
            c = c_ref[p, lo:, :]
            w = jnp.exp(zz - sp + tail + c)
            if mask is not None:
                w = jnp.where(mask, w, 0.0)
            w_ref[p, lo:, :] = w.astype(BF16)
            tot0 = jnp.sum(spm[:, :BLOCK], axis=-1, keepdims=True)
            tot1 = jnp.sum(spm[:, BLOCK:], axis=-1, keepdims=True)
            c_ref[p, lo:, :] = c - jnp.concatenate([jnp.broadcast_to(tot0, (tq - lo, BLOCK)),
                                                    jnp.broadcast_to(tot1, (tq - lo, BLOCK))], axis=1)

    top = base + Q_SUB - 1
    for p in range(PAIRS):
        scores(p, k_real(p, top), (Q_SUB - 1) * BLOCK)
    for jj in reversed(range(Q_SUB)):
        kb = base + jj
        lo = jj * BLOCK
        step(col_iota(tq - lo) < lax.broadcasted_iota(jnp.int32, (tq - lo, 2 * BLOCK), 0),
             (lambda p, kb=kb: k_real(p, kb - 1)) if jj > 0 else (lambda p, kb=kb: k_real_or_prefix(p, kb - 1)),
             (lambda p, kb=kb: v_real(p, kb + 1)) if jj < Q_SUB - 1 else None,
             lo=lo, lo_next=max(lo - BLOCK, 0), lo_prev=lo + BLOCK)

    def alive():
        return jnp.max(c_ref[...]) > DEAD_LOG_TAIL

    def cond(carry):
        kb, live = carry
        return jnp.logical_and(kb >= 0, live)

    def body(carry):
        kb, _ = carry
        step(None, lambda p: k_real_or_prefix(p, kb - 1), lambda p: v_real(p, kb + 1))
        return kb - 1, alive()

    kb, live = lax.while_loop(cond, body, (base - 1, alive()))
    for p in range(PAIRS):
        values(p, v_real(p, kb + 1), 0)

    @pl.when(jnp.logical_and(kb < 0, live))
    def _():
        step(col_iota(tq) >= LEAD_PAD, None, None)
        for p in range(PAIRS):
            values(p, v_prefix(p), 0)

    head0 = lax.broadcasted_iota(jnp.int32, (tq, LANES), 1) < SB_HEAD_DIM
    for p in range(PAIRS):
        o = acc_ref[p]
        o2 = o * o
        s0 = jnp.sum(jnp.where(head0, o2, 0.0), axis=-1, keepdims=True) * (1.0 / SB_HEAD_DIM)
        s1 = jnp.sum(jnp.where(head0, 0.0, o2), axis=-1, keepdims=True) * (1.0 / SB_HEAD_DIM)
        r = jnp.where(head0, lax.rsqrt(s0 + EPS), lax.rsqrt(s1 + EPS))
        o_ref[:, lanes(p)] = (o * r * gain_ref[:, lanes(p)]).astype(o_ref.dtype)


def _sb_tail_matrix():
    j = np.arange(BLOCK)[:, None]
    s = np.arange(BLOCK)[None, :]
    return jnp.asarray(np.where(j > s, -1.0, 0.0), dtype=BF16)


def _sb_attention(q, kv, kv_meta, gain, b, seq):
    t, width = q.shape
    nqt = seq // (Q_SUB * BLOCK)
    gw = PAIRS * LANES
    ngroup = width // gw
    kv3 = [a.reshape(b, seq, width) for a in kv]
    tq = Q_SUB * BLOCK
    kv_spec = pl.BlockSpec((None, seq, gw), lambda bi, hp, qt: (bi, 0, hp))
    meta_spec = pl.BlockSpec((BLOCK, gw), lambda bi, hp, qt: (0, hp))
    return pl.pallas_call(
        _sb_kernel,
        grid=(b, ngroup, nqt),
        in_specs=[pl.BlockSpec((tq, gw), lambda bi, hp, qt: (bi * nqt + qt, hp)),
                  kv_spec, kv_spec, kv_spec, kv_spec,
                  meta_spec, meta_spec, meta_spec, meta_spec,
                  pl.BlockSpec((BLOCK, BLOCK), lambda bi, hp, qt: (0, 0)),
                  pl.BlockSpec((1, gw), lambda bi, hp, qt: (0, hp))],
        out_specs=pl.BlockSpec((tq, gw), lambda bi, hp, qt: (bi * nqt + qt, hp)),
        out_shape=jax.ShapeDtypeStruct((t, width), BF16),
        scratch_shapes=[pltpu.VMEM((PAIRS, tq, 2 * BLOCK), F32), pltpu.VMEM((PAIRS, tq, LANES), F32),
                        pltpu.VMEM((PAIRS, tq, 2 * BLOCK), F32), pltpu.VMEM((PAIRS, tq, 2 * BLOCK), BF16)],
        compiler_params=pltpu.CompilerParams(
            dimension_semantics=("arbitrary", "arbitrary", "arbitrary"),
            vmem_limit_bytes=40 * 1024 * 1024),
        name="sb_attn",
    )(q, *kv3, *kv_meta, _sb_tail_matrix(), gain)


def _hg_block(hq_ref, hk_ref, hv_ref, lf_ref, lmat_ref, st_ref, a_ref, qt_ref, kh_ref, oacc_ref, f_ref, n_heads,
              with_output):
    lf = lf_ref[...]
    h1 = lf.astype(BF16)
    r1 = lf - h1.astype(F32)
    h2 = r1.astype(BF16)
    h3 = (r1 - h2.astype(F32)).astype(BF16)
    lmat = lmat_ref[...]
    cs = _dot(lmat, h1) + _dot(lmat, h2) + _dot(lmat, h3)
    a = cs[:BLOCK]
    alast = cs[BLOCK:]
    a_ref[...] = alast
    kh_ref[...] = hk_ref[...] * jnp.exp(alast - a)
    if with_output:
        f_ref[...] = jnp.exp(lf)
        qt_ref[...] = hq_ref[...] * jnp.exp(a)
    ridx = lax.broadcasted_iota(jnp.int32, (SUB, 1), 0)

    for i in range(BLOCK // SUB):
        r0 = i * SUB
        rows = pl.ds(r0, SUB)
        for hd in range(n_heads):
            cols = slice(hd * HG_HEAD_DIM, (hd + 1) * HG_HEAD_DIM)
            v_i = hv_ref[rows, cols]
            st = st_ref[hd]
            if with_output:
                o_i = _dot_nt(qt_ref[rows, cols], st)
                e = hq_ref[rows, cols]
                for s in reversed(range(SUB)):
                    if s < SUB - 1:
                        e = jnp.where(ridx > s, e * f_ref[pl.ds(r0 + s + 1, 1), cols], e)
                    sc = jnp.sum(e * hk_ref[pl.ds(r0 + s, 1), cols], axis=-1, keepdims=True)
                    o_i = o_i + jnp.where(ridx >= s, sc, 0.0) * hv_ref[pl.ds(r0 + s, 1), cols]
                oacc_ref[rows, cols] = o_i
            decay = jnp.exp(a_ref[pl.ds(r0, 1), cols])
            st_ref[hd] = st * decay + _dot_tn(v_i, kh_ref[rows, cols])


def _hg_kernel(hq_ref, hk_ref, hv_ref, lf_ref, gate_ref, mk_ref, mv_ref, mlf_ref, gain_ref, lmat_ref, o_ref,
               st_ref, a_ref, qt_ref, kh_ref, oacc_ref, f_ref, *, n_heads):
    scratch = (st_ref, a_ref, qt_ref, kh_ref, oacc_ref, f_ref)

    @pl.when(pl.program_id(1) == 0)
    def _():
        st_ref[...] = jnp.zeros_like(st_ref)
        _hg_block(None, mk_ref, mv_ref, mlf_ref, lmat_ref, *scratch, n_heads, with_output=False)

    _hg_block(hq_ref, hk_ref, hv_ref, lf_ref, lmat_ref, *scratch, n_heads, with_output=True)
    for hd in range(n_heads):
        cols = slice(hd * HG_HEAD_DIM, (hd + 1) * HG_HEAD_DIM)
        o = _rms(oacc_ref[:, cols]) * gain_ref[:, cols] * gate_ref[:, cols]
        o_ref[:, cols] = o.astype(o_ref.dtype)


def _hg_cumsum_matrix():
    t = np.arange(BLOCK)[:, None]
    s = np.arange(BLOCK)[None, :]
    same = (t // SUB) == (s // SUB)
    incl = np.where(same & (s <= t), 1.0, 0.0)
    full = np.where(same, 1.0, 0.0)
    return jnp.asarray(np.concatenate([incl, full], axis=0), dtype=BF16)


def _hgrn2(hq, hk, hv, lf, gate, meta, gain, b, seq):
    t, width = hq.shape
    nc = seq // BLOCK
    n_heads = width // HG_HEAD_DIM
    blk = pl.BlockSpec((BLOCK, width), lambda bi, ci: (bi * nc + ci, 0))
    mblk = pl.BlockSpec((BLOCK, width), lambda bi, ci: (0, 0))
    kern = functools.partial(_hg_kernel, n_heads=n_heads)
    return pl.pallas_call(
        kern,
        grid=(b, nc),
        in_specs=[blk, blk, blk, blk, blk, mblk, mblk, mblk,
                  pl.BlockSpec((1, width), lambda bi, ci: (0, 0)),
                  pl.BlockSpec((2 * BLOCK, BLOCK), lambda bi, ci: (0, 0))],
        out_specs=blk,
        out_shape=jax.ShapeDtypeStruct((t, width), BF16),
        scratch_shapes=[pltpu.VMEM((n_heads, HG_HEAD_DIM, HG_HEAD_DIM), F32),
                        pltpu.VMEM((BLOCK, width), F32), pltpu.VMEM((BLOCK, width), F32),
                        pltpu.VMEM((BLOCK, width), F32), pltpu.VMEM((BLOCK, width), F32),
                        pltpu.VMEM((BLOCK, width), F32)],
        compiler_params=pltpu.CompilerParams(dimension_semantics=("arbitrary", "arbitrary")),
        name="hgrn2",
    )(hq, hk, hv, lf, gate, *meta, gain, _hg_cumsum_matrix())


R_E1, R_E2, R_RANK1, R_RANK2, R_G1, R_G2 = range(6)


def _outproj_kernel(osb_ref, ohg_ref, h_ref, w_ref, g_ref, wr_ref, br_ref, tri_ref,
                    h2_ref, m_ref, route_ref, cnt_ref, carry_ref, *, width):
    @pl.when(pl.program_id(0) == 0)
    def _():
        carry_ref[...] = jnp.zeros_like(carry_ref)

    h2 = h_ref[...] + _dot(osb_ref[...], w_ref[:width, :]) + _dot(ohg_ref[...], w_ref[width:, :])
    h2_ref[...] = h2
    m = _rms(h2) * g_ref[...]
    _tiles_store(m_ref, m, m.shape[1] // LANES)
    lg = _dot(m, wr_ref[...]) + br_ref[...]
    tm = lg.shape[0]
    lane = lax.broadcasted_iota(jnp.int32, (tm, LANES), 1)
    neg = jnp.float32(-1e30)

    def first_argmax(vals):
        vmax = jnp.max(vals, axis=-1, keepdims=True)
        idx = jnp.min(jnp.where(vals == vmax, lane, LANES), axis=-1, keepdims=True)
        return vmax, idx

    is_grp = lane < N_GROUPS
    gl = jnp.where(is_grp, lg, neg)
    gmax, gidx = first_argmax(gl)
    p_grp = 1.0 / jnp.sum(jnp.where(is_grp, jnp.exp(gl - gmax), 0.0), axis=-1, keepdims=True)
    lo = N_GROUPS + gidx * EXPERTS_PER_GROUP
    el = jnp.where((lane >= lo) & (lane < lo + EXPERTS_PER_GROUP), lg, neg)
    v1, i1 = first_argmax(el)
    sel1 = lane == i1
    v2, i2 = first_argmax(jnp.where(sel1, neg, el))
    sel2 = lane == i2
    dlt = jnp.exp(v2 - v1)
    g1 = p_grp / (1.0 + dlt)
    g2 = g1 * dlt

    chosen = jnp.where(sel1 | sel2, 1.0, 0.0)
    carry = carry_ref[0:1, :]
    before = _dot(tri_ref[...], chosen.astype(BF16)) + carry
    r1 = jnp.sum(jnp.where(sel1, before, 0.0), axis=-1, keepdims=True)
    r2 = jnp.sum(jnp.where(sel2, before, 0.0), axis=-1, keepdims=True)
    carry = carry + jnp.sum(chosen, axis=0, keepdims=True)
    carry_ref[0:1, :] = carry
    cnt_ref[...] = jnp.broadcast_to(carry, cnt_ref.shape)

    rec = jnp.zeros((tm, LANES), F32)
    for ln, val in ((R_E1, (i1 - N_GROUPS).astype(F32)), (R_E2, (i2 - N_GROUPS).astype(F32)),
                    (R_RANK1, r1), (R_RANK2, r2), (R_G1, g1), (R_G2, g2)):
        rec = jnp.where(lane == ln, val, rec)
    route_ref[...] = rec


def _outproj(o_sb, o_hg, h, w_out, g_ffn, w_r, b_r, tm):
    t, d = h.shape
    width = o_sb.shape[1]
    row = lambda i: (i, 0)
    const = lambda i: (0, 0)
    tri = jnp.asarray(np.tril(np.ones((tm, tm), np.float32), -1), dtype=BF16)
    kern = functools.partial(_outproj_kernel, width=width)
    return pl.pallas_call(
        kern,
        grid=(t // tm,),
        in_specs=[pl.BlockSpec((tm, width), row), pl.BlockSpec((tm, width), row),
                  pl.BlockSpec((tm, d), row), pl.BlockSpec((2 * width, d), const),
                  pl.BlockSpec((1, d), const), pl.BlockSpec((d, LANES), const),
                  pl.BlockSpec((1, LANES), const), pl.BlockSpec((tm, tm), const)],
        out_specs=[pl.BlockSpec((tm, d), row), pl.BlockSpec((tm * (d // LANES), LANES), row),
                   pl.BlockSpec((tm, LANES), row), pl.BlockSpec((8, LANES), const)],
        out_shape=[jax.ShapeDtypeStruct((t, d), F32), jax.ShapeDtypeStruct((t * (d // LANES), LANES), F32),
                   jax.ShapeDtypeStruct((t, LANES), F32), jax.ShapeDtypeStruct((8, LANES), F32)],
        scratch_shapes=[pltpu.VMEM((8, LANES), F32)],
        compiler_params=pltpu.CompilerParams(
            dimension_semantics=("arbitrary",), vmem_limit_bytes=40 * 1024 * 1024),
        name="outproj",
    )(o_sb, o_hg, h, w_out, g_ffn, w_r, b_r, tri)


def _row_copy(src, dst, sem):
    return pltpu.make_async_copy(src, dst, sem)


def _dispatch_kernel(d1_ref, d2_ref, zs_ref, zn_ref, m_hbm, xs_hbm, zero_ref, stage_ref, sems, lsems, zsem,
                     *, tile, n_tiles, ch):
    i = pl.program_id(0)
    slot = i % 3
    lines = tile * ch

    def tile_wait(s):
        _row_copy(m_hbm.at[pl.ds(0, 2 * lines), :], xs_hbm.at[pl.ds(0, 2 * lines), :], sems.at[s]).wait()

    def tile_load(step, s):
        rows = pl.ds(pl.multiple_of(step * lines, lines), lines)
        return _row_copy(m_hbm.at[rows, :], stage_ref.at[s], lsems.at[s])

    @pl.when(i == 0)
    def _():
        tile_load(0, 0).start()
        zero_ref[...] = jnp.zeros_like(zero_ref)

        def zero_block(j):
            rows = pl.ds(pl.multiple_of(zs_ref[j], MOE_BM * ch), MOE_BM * ch)
            return _row_copy(zero_ref, xs_hbm.at[rows, :], zsem)

        for j in range(2 * N_EXPERTS):
            @pl.when(zn_ref[j] > 0)
            def _():
                zero_block(j).start()
        for j in range(2 * N_EXPERTS):
            @pl.when(zn_ref[j] > 0)
            def _():
                zero_block(j).wait()

    nxt = (i + 1) % 3

    @pl.when(i >= 2)
    def _():
        tile_wait(nxt)

    @pl.when(i + 1 < n_tiles)
    def _():
        tile_load(i + 1, nxt).start()

    tile_load(i, slot).wait()
    base = i * tile
    for r in range(tile):
        src = stage_ref.at[slot, pl.ds(r * ch, ch), :]
        for prio, d_ref in enumerate((d1_ref, d2_ref)):
            dst = xs_hbm.at[pl.ds(pl.multiple_of(d_ref[base + r], ch), ch), :]
            _row_copy(src, dst, sems.at[slot]).start(priority=prio)

    @pl.when(i == n_tiles - 1)
    def _():
        if n_tiles > 1:
            tile_wait((i + 2) % 3)
        tile_wait(slot)


def _dispatch(d1, d2, zero_start, zero_n, m_tiles, n_tok, n_slots, tile=BLOCK):
    ch = m_tiles.shape[0] // n_tok
    n_tiles = n_tok // tile
    kern = functools.partial(_dispatch_kernel, tile=tile, n_tiles=n_tiles, ch=ch)
    grid_spec = pltpu.PrefetchScalarGridSpec(
        num_scalar_prefetch=4,
        grid=(n_tiles,),
        in_specs=[pl.BlockSpec(memory_space=pl.ANY)],
        out_specs=pl.BlockSpec(memory_space=pl.ANY),
        scratch_shapes=[pltpu.VMEM((MOE_BM * ch, LANES), F32), pltpu.VMEM((3, tile * ch, LANES), F32),
                        pltpu.SemaphoreType.DMA((3,)), pltpu.SemaphoreType.DMA((3,)),
                        pltpu.SemaphoreType.DMA(())],
    )
    return pl.pallas_call(
        kern,
        grid_spec=grid_spec,
        out_shape=jax.ShapeDtypeStruct((n_slots * ch, LANES), F32),
        compiler_params=pltpu.CompilerParams(dimension_semantics=("arbitrary",)),
        name="dispatch",
    )(d1, d2, zero_start, zero_n, m_tiles)


def _expert_kernel(be_ref, nu_ref, xs_ref, wg_ref, wu_ref, wd_ref, y_ref, *, ch):
    i = pl.program_id(0)

    @pl.when(i < nu_ref[0])
    def _():
        xs = _tiles_load(xs_ref, MOE_BM, ch)
        gt = _dot(xs, wg_ref[...])
        hb = gt * jax.nn.sigmoid(gt) * _dot(xs, wu_ref[...])
        _tiles_store(y_ref, _dot(hb, wd_ref[...]), ch)

    @pl.when(i >= nu_ref[0])
    def _():
        y_ref[...] = jnp.zeros_like(y_ref)


def _experts(block_e, n_used, xs_tiles, n_slots, w_gate, w_up, w_down):
    d, ff = w_gate.shape[-2:]
    ch = d // LANES
    last = lambda i, be, nu: (jnp.minimum(i, nu[0] - 1), 0)
    grid_spec = pltpu.PrefetchScalarGridSpec(
        num_scalar_prefetch=2,
        grid=(n_slots // MOE_BM,),
        in_specs=[pl.BlockSpec((MOE_BM * ch, LANES), last),
                  pl.BlockSpec((None, d, ff), lambda i, be, nu: (be[i], 0, 0)),
                  pl.BlockSpec((None, d, ff), lambda i, be, nu: (be[i], 0, 0)),
                  pl.BlockSpec((None, ff, d), lambda i, be, nu: (be[i], 0, 0))],
        out_specs=pl.BlockSpec((MOE_BM * ch, LANES), lambda i, be, nu: (i, 0)),
    )
    return pl.pallas_call(
        functools.partial(_expert_kernel, ch=ch),
        grid_spec=grid_spec,
        out_shape=jax.ShapeDtypeStruct((n_slots * ch, LANES), F32),
        compiler_params=pltpu.CompilerParams(
            dimension_semantics=("arbitrary",), vmem_limit_bytes=40 * 1024 * 1024),
        name="experts",
    )(block_e, n_used, xs_tiles, w_gate, w_up, w_down)


def _combine_kernel(d1_ref, d2_ref, h2_ref, route_ref, ys_hbm, g_ref, o_ref, ya_ref, yb_ref, sems,
                    *, tile, n_tiles, ch):
    i = pl.program_id(0)
    slot = i % 2
    lines = tile * ch

    def issue(step, s):
        base = step * tile
        for r in range(tile):
            for prio, (d_ref, y_ref) in enumerate(((d1_ref, ya_ref), (d2_ref, yb_ref))):
                src = ys_hbm.at[pl.ds(pl.multiple_of(d_ref[base + r], ch), ch), :]
                _row_copy(src, y_ref.at[s, pl.ds(r * ch, ch), :], sems.at[s]).start(priority=prio)

    @pl.when(i == 0)
    def _():
        issue(0, 0)

    @pl.when(i + 1 < n_tiles)
    def _():
        issue(i + 1, 1 - slot)

    _row_copy(ys_hbm.at[pl.ds(0, lines), :], ya_ref.at[slot], sems.at[slot]).wait()
    _row_copy(ys_hbm.at[pl.ds(0, lines), :], yb_ref.at[slot], sems.at[slot]).wait()
    rec = route_ref[...]
    g1 = rec[:, R_G1:R_G1 + 1]
    g2 = rec[:, R_G2:R_G2 + 1]
    h = h2_ref[...] + (g1 * _tiles_load(ya_ref.at[slot], tile, ch) + g2 * _tiles_load(yb_ref.at[slot], tile, ch))
    o_ref[...] = _rms(h) * g_ref[...]


def _combine(d1, d2, h2, route, ys, g_final, tile=BLOCK):
    t, d = h2.shape
    ch = d // LANES
    n_tiles = t // tile
    kern = functools.partial(_combine_kernel, tile=tile, n_tiles=n_tiles, ch=ch)
    grid_spec = pltpu.PrefetchScalarGridSpec(
        num_scalar_prefetch=2,
        grid=(n_tiles,),
        in_specs=[pl.BlockSpec((tile, d), lambda i, a, b: (i, 0)),
                  pl.BlockSpec((tile, LANES), lambda i, a, b: (i, 0)),
                  pl.BlockSpec(memory_space=pl.ANY),
                  pl.BlockSpec((1, d), lambda i, a, b: (0, 0))],
        out_specs=pl.BlockSpec((tile, d), lambda i, a, b: (i, 0)),
        scratch_shapes=[pltpu.VMEM((2, tile * ch, LANES), F32), pltpu.VMEM((2, tile * ch, LANES), F32),
                        pltpu.SemaphoreType.DMA((2,))],
    )
    return pl.pallas_call(
        kern,
        grid_spec=grid_spec,
        out_shape=jax.ShapeDtypeStruct((t, d), F32),
        compiler_params=pltpu.CompilerParams(dimension_semantics=("arbitrary",)),
        name="combine",
    )(d1, d2, h2, route, ys, g_final)


def _segment_layout(route, counts_row, n_tok, ch):
    counts = counts_row[N_GROUPS:N_GROUPS + N_EXPERTS].astype(jnp.int32)
    n_slots = n_tok * TOP_K + N_EXPERTS * MOE_BM
    n_blocks = n_slots // MOE_BM
    padded = (counts + MOE_BM - 1) // MOE_BM * MOE_BM
    padded_end = jnp.cumsum(padded)
    padded_start = padded_end - padded
    e1 = route[:, R_E1].astype(jnp.int32)
    e2 = route[:, R_E2].astype(jnp.int32)
    onehot = jnp.arange(N_EXPERTS, dtype=jnp.int32)[None, :]
    start1 = jnp.sum(jnp.where(e1[:, None] == onehot, padded_start[None, :], 0), axis=1)
    start2 = jnp.sum(jnp.where(e2[:, None] == onehot, padded_start[None, :], 0), axis=1)
    d1 = (start1 + route[:, R_RANK1].astype(jnp.int32)) * ch
    d2 = (start2 + route[:, R_RANK2].astype(jnp.int32)) * ch
    block_e = jnp.sum((jnp.arange(n_blocks, dtype=jnp.int32)[:, None] * MOE_BM >= padded_end[None, :])
                      .astype(jnp.int32), axis=1)
    block_e = jnp.minimum(block_e, N_EXPERTS - 1)
    n_used = (padded_end[-1] // MOE_BM).reshape(1)
    trailing = n_used[0] + jnp.arange(N_EXPERTS, dtype=jnp.int32)
    zero_start = jnp.concatenate([jnp.maximum(padded_end - MOE_BM, 0),
                                  jnp.minimum(trailing, n_blocks - 1) * MOE_BM]) * ch
    zero_flag = jnp.concatenate([counts, (trailing < n_blocks).astype(jnp.int32)])
    return d1, d2, block_e, n_used, zero_start, zero_flag, n_slots


def kernel(x, meta_tokens, lb_logits, g_mix, w_in, sb_gain, hg_gain, w_out, g_ffn, w_router_group,
           b_router_group, w_router_expert, b_router_expert, w_expert_gate, w_expert_up, w_expert_down,
           g_final):
    b, seq, d = x.shape
    depth = w_in.shape[0]
    assert depth == 1, "single-layer block"
    assert seq % (Q_SUB * BLOCK) == 0
    t = b * seq
    tm = 512
    layer = 0

    xr = x.reshape(t, d)
    prefix = jnp.concatenate([jnp.zeros((LEAD_PAD, d), x.dtype), meta_tokens.astype(x.dtype)], axis=0)
    prefix_mask = (jnp.arange(BLOCK) >= LEAD_PAD).astype(F32)[:, None]
    lower_bounds = jnp.cumsum(jax.nn.softmax(lb_logits.astype(F32), axis=0), axis=0)
    lb = lower_bounds[layer][None, :]
    g_mix_l = g_mix[layer][None, :]
    w_in_l = w_in[layer].astype(BF16)

    q, k0, k1, v0, v1, hq, hk, hv, lf, hg = _proj(xr, jnp.ones((t, 1), F32), g_mix_l, w_in_l, lb, tm)
    _, mk0, mk1, mv0, mv1, _, mhk, mhv, mlf, _ = _proj(prefix, prefix_mask, g_mix_l, w_in_l, lb, BLOCK)

    o_sb = _sb_attention(q, (k0, k1, v0, v1), (mk0, mk1, mv0, mv1), sb_gain[layer][None, :], b, seq)
    o_hg = _hgrn2(hq, hk, hv, lf, hg, (mhk, mhv, mlf), hg_gain[layer][None, :], b, seq)

    w_r = jnp.zeros((d, LANES), F32)
    w_r = w_r.at[:, :N_GROUPS].set(w_router_group[layer])
    w_r = w_r.at[:, N_GROUPS:N_GROUPS + N_EXPERTS].set(w_router_expert[layer])
    b_r = jnp.zeros((1, LANES), F32)
    b_r = b_r.at[0, :N_GROUPS].set(b_router_group[layer])
    b_r = b_r.at[0, N_GROUPS:N_GROUPS + N_EXPERTS].set(b_router_expert[layer])
    h2, m_tiles, route, counts = _outproj(o_sb, o_hg, xr, w_out[layer].astype(BF16), g_ffn[layer][None, :],
                                          w_r, b_r, tm)

    d1, d2, block_e, n_used, zero_start, zero_n, n_slots = _segment_layout(route, counts[0], t, d // LANES)
    xs_tiles = _dispatch(d1, d2, zero_start, zero_n, m_tiles, t, n_slots)
    ys_tiles = _experts(block_e, n_used, xs_tiles, n_slots,
                        w_expert_gate[layer], w_expert_up[layer], w_expert_down[layer])
    out = _combine(d1, d2, h2, route, ys_tiles, g_final[None, :])
    return out.reshape(b, seq, d)
```

```python
import functools

import numpy as np
import jax
import jax.numpy as jnp
from jax import lax
from jax.experimental import pallas as pl
from jax.experimental.pallas import tpu as pltpu

BLOCK = 128
N_META = 16
LEAD_PAD = BLOCK - N_META
SB_HEAD_DIM = 64
HG_HEAD_DIM = 128
SUB = 16
Q_SUB = 4
PAIRS = 2
DEAD_LOG_TAIL = -104.0
N_GROUPS = 4
EXPERTS_PER_GROUP = 8
N_EXPERTS = N_GROUPS * EXPERTS_PER_GROUP
TOP_K = 2
MOE_BM = 256
EPS = 1e-6
LANES = 128

F32 = jnp.float32
BF16 = jnp.bfloat16


def _dot(a, b):
    return jnp.dot(a, b, preferred_element_type=F32)


def _dot_nt(a, b):
    return lax.dot_general(a, b, (((1,), (1,)), ((), ())), preferred_element_type=F32)


def _dot_tn(a, b):
    return lax.dot_general(a, b, (((0,), (0,)), ((), ())), preferred_element_type=F32)


def _rms(x):
    return x * lax.rsqrt(jnp.mean(x * x, axis=-1, keepdims=True) + EPS)


def _tiles_load(ref, n, chunks):
    return jnp.concatenate([ref[pl.ds(c, n, stride=chunks), :] for c in range(chunks)], axis=1)


def _tiles_store(ref, x, chunks):
    n = x.shape[0]
    for c in range(chunks):
        ref[pl.ds(c, n, stride=chunks), :] = x[:, c * LANES:(c + 1) * LANES]


def _proj_kernel(h_ref, mask_ref, g_ref, w_ref, lb_ref,
                 q_ref, k0_ref, k1_ref, v0_ref, v1_ref,
                 hq_ref, hk_ref, hv_ref, lf_ref, hg_ref, *, width):
    a = (_rms(h_ref[...]) * g_ref[...]).astype(BF16)

    def p(i):
        return _dot(a, w_ref[:, i * width:(i + 1) * width])

    lane = lax.broadcasted_iota(jnp.int32, (1, width), 1)
    head0 = (lane & (LANES - 1)) < SB_HEAD_DIM
    q_ref[...] = (p(0) * (SB_HEAD_DIM ** -0.5)).astype(BF16)
    k = p(1)
    k0_ref[...] = jnp.where(head0, k, 0.0).astype(BF16)
    k1_ref[...] = jnp.where(head0, 0.0, k).astype(BF16)
    v = p(2)
    v0_ref[...] = jnp.where(head0, v, 0.0).astype(BF16)
    v1_ref[...] = jnp.where(head0, 0.0, v).astype(BF16)
    hq = p(3)
    hq_ref[...] = hq * jax.nn.sigmoid(hq)
    f = p(4)
    lb = lb_ref[...]
    sig = jax.nn.sigmoid(f)
    lf_ref[...] = jnp.log(lb + (1.0 - lb) * sig)
    hk_ref[...] = mask_ref[...] * ((1.0 - lb) * jax.nn.sigmoid(-f))
    hv_ref[...] = p(5)
    g = p(6)
    hg_ref[...] = g * jax.nn.sigmoid(g)


def _proj(h, mask, g_mix, w_in, lb, tm):
    t, d = h.shape
    width = d // 2
    kern = functools.partial(_proj_kernel, width=width)
    row = lambda i: (i, 0)
    const = lambda i: (0, 0)
    outs = ([jax.ShapeDtypeStruct((t, width), BF16)] * 5
            + [jax.ShapeDtypeStruct((t, width), F32)] * 5)
    return pl.pallas_call(
        kern,
        grid=(t // tm,),
        in_specs=[pl.BlockSpec((tm, d), row), pl.BlockSpec((tm, 1), row),
                  pl.BlockSpec((1, d), const), pl.BlockSpec((d, 7 * width), const),
                  pl.BlockSpec((1, width), const)],
        out_specs=[pl.BlockSpec((tm, width), row)] * 10,
        out_shape=outs,
        compiler_params=pltpu.CompilerParams(
            dimension_semantics=("arbitrary",), vmem_limit_bytes=52 * 1024 * 1024),
        name="proj",
    )(h, mask, g_mix, w_in, lb)


def _sb_kernel(q_ref, k0_ref, k1_ref, v0_ref, v1_ref, mk0_ref, mk1_ref, mv0_ref, mv1_ref,
               tt_ref, gain_ref, o_ref, c_ref, acc_ref, zz_ref, w_ref):
    tq = Q_SUB * BLOCK
    base = pl.program_id(2) * Q_SUB
    c_ref[...] = jnp.zeros_like(c_ref)
    acc_ref[...] = jnp.zeros_like(acc_ref)

    def col_iota(rows):
        return lax.broadcasted_iota(jnp.int32, (rows, 2 * BLOCK), 1) & (BLOCK - 1)

    def lanes(p):
        return slice(p * LANES, (p + 1) * LANES)

    def k_real(p, kb):
        rows = pl.ds(pl.multiple_of(kb * BLOCK, BLOCK), BLOCK)
        return jnp.concatenate([k0_ref[rows, lanes(p)], k1_ref[rows, lanes(p)]], axis=0)

    def v_real(p, kb):
        rows = pl.ds(pl.multiple_of(kb * BLOCK, BLOCK), BLOCK)
        return jnp.concatenate([v0_ref[rows, lanes(p)], v1_ref[rows, lanes(p)]], axis=0)

    def k_real_or_prefix(p, kb):
        k_prefix = jnp.concatenate([mk0_ref[:, lanes(p)], mk1_ref[:, lanes(p)]], axis=0)
        return jnp.where(kb >= 0, k_real(p, jnp.maximum(kb, 0)), k_prefix)

    def v_prefix(p):
        return jnp.concatenate([mv0_ref[:, lanes(p)], mv1_ref[:, lanes(p)]], axis=0)

    def scores(p, k2, lo):
        zz_ref[p, lo:, :] = _dot_nt(q_ref[lo:, lanes(p)], k2)

    def values(p, v2, lo):
        acc_ref[p, lo:, :] += _dot(w_ref[p, lo:, :], v2)

    def step(mask, k_next, v_prev, lo=0, lo_next=0, lo_prev=0):
        for p in range(PAIRS):
            zz = zz_ref[p, lo:, :]
            if k_next is not None:
                scores(p, k_next(p), lo_next)
            if v_prev is not None:
                values(p, v_prev(p), lo_prev)
            neg_abs = lax.bitcast_convert_type(
                lax.bitcast_convert_type(zz, jnp.uint32) | jnp.uint32(0x80000000), F32)
            sp = jnp.maximum(zz, 0.0) + jnp.log(1.0 + jnp.exp(neg_abs))
            spm = sp if mask is None else jnp.where(mask, sp, 0.0)
            spb = spm.astype(BF16)
            tt = tt_ref[...]
            tail = jnp.concatenate(
                [_dot(spb[:, h * BLOCK:(h + 1) * BLOCK], tt) for h in range(2)], axis=1)
            c = c_ref[p, lo:, :]
            w = jnp.exp(zz - sp + tail + c)
            if mask is not None:
                w = jnp.where(mask, w, 0.0)
            w_ref[p, lo:, :] = w.astype(BF16)
            tot0 = jnp.sum(spm[:, :BLOCK], axis=-1, keepdims=True)
            tot1 = jnp.sum(spm[:, BLOCK:], axis=-1, keepdims=True)
            c_ref[p, lo:, :] = c - jnp.concatenate([jnp.broadcast_to(tot0, (tq - lo, BLOCK)),
                                                    jnp.broadcast_to(tot1, (tq - lo, BLOCK))], axis=1)

    top = base + Q_SUB - 1
    for p in range(PAIRS):
        scores(p, k_real(p, top), (Q_SUB - 1) * BLOCK)
    for jj in reversed(range(Q_SUB)):
        kb = base + jj
        lo = jj * BLOCK
        step(col_iota(tq - lo) < lax.broadcasted_iota(jnp.int32, (tq - lo, 2 * BLOCK), 0),
             (lambda p, kb=kb: k_real(p, kb - 1)) if jj > 0 else (lambda p, kb=kb: k_real_or_prefix(p, kb - 1)),
             (lambda p, kb=kb: v_real(p, kb + 1)) if jj < Q_SUB - 1 else None,
             lo=lo, lo_next=max(lo - BLOCK, 0), lo_prev=lo + BLOCK)

    def alive():
        return jnp.max(c_ref[...]) > DEAD_LOG_TAIL

    def cond(carry):
        kb, live = carry
        return jnp.logical_and(kb >= 0, live)

    def body(carry):
        kb, _ = carry
        step(None, lambda p: k_real_or_prefix(p, kb - 1), lambda p: v_real(p, kb + 1))
        return kb - 1, alive()

    kb, live = lax.while_loop(cond, body, (base - 1, alive()))
    for p in range(PAIRS):
        values(p, v_real(p, kb + 1), 0)

    @pl.when(jnp.logical_and(kb < 0, live))
    def _():
        step(col_iota(tq) >= LEAD_PAD, None, None)
        for p in range(PAIRS):
            values(p, v_prefix(p), 0)

    head0 = lax.broadcasted_iota(jnp.int32, (tq, LANES), 1) < SB_HEAD_DIM
    for p in range(PAIRS):
        o = acc_ref[p]
        o2 = o * o
        s0 = jnp.sum(jnp.where(head0, o2, 0.0), axis=-1, keepdims=True) * (1.0 / SB_HEAD_DIM)
        s1 = jnp.sum(jnp.where(head0, 0.0, o2), axis=-1, keepdims=True) * (1.0 / SB_HEAD_DIM)
        r = jnp.where(head0, lax.rsqrt(s0 + EPS), lax.rsqrt(s1 + EPS))
        o_ref[:, lanes(p)] = (o * r * gain_ref[:, lanes(p)]).astype(o_ref.dtype)


def _sb_tail_matrix():
    j = np.arange(BLOCK)[:, None]
    s = np.arange(BLOCK)[None, :]
    return jnp.asarray(np.where(j > s, -1.0, 0.0), dtype=BF16)


def _sb_attention(q, kv, kv_meta, gain, b, seq):
    t, width = q.shape
    nqt = seq // (Q_SUB * BLOCK)
    gw = PAIRS * LANES
    ngroup = width // gw
    kv3 = [a.reshape(b, seq, width) for a in kv]
    tq = Q_SUB * BLOCK
    kv_spec = pl.BlockSpec((None, seq, gw), lambda bi, hp, qt: (bi, 0, hp))
    meta_spec = pl.BlockSpec((BLOCK, gw), lambda bi, hp, qt: (0, hp))
    return pl.pallas_call(
        _sb_kernel,
        grid=(b, ngroup, nqt),
        in_specs=[pl.BlockSpec((tq, gw), lambda bi, hp, qt: (bi * nqt + qt, hp)),
                  kv_spec, kv_spec, kv_spec, kv_spec,
                  meta_spec, meta_spec, meta_spec, meta_spec,
                  pl.BlockSpec((BLOCK, BLOCK), lambda bi, hp, qt: (0, 0)),
                  pl.BlockSpec((1, gw), lambda bi, hp, qt: (0, hp))],
        out_specs=pl.BlockSpec((tq, gw), lambda bi, hp, qt: (bi * nqt + qt, hp)),
        out_shape=jax.ShapeDtypeStruct((t, width), BF16),
        scratch_shapes=[pltpu.VMEM((PAIRS, tq, 2 * BLOCK), F32), pltpu.VMEM((PAIRS, tq, LANES), F32),
                        pltpu.VMEM((PAIRS, tq, 2 * BLOCK), F32), pltpu.VMEM((PAIRS, tq, 2 * BLOCK), BF16)],
        compiler_params=pltpu.CompilerParams(
            dimension_semantics=("arbitrary", "arbitrary", "arbitrary"),
            vmem_limit_bytes=40 * 1024 * 1024),
        name="sb_attn",
    )(q, *kv3, *kv_meta, _sb_tail_matrix(), gain)


def _hg_block(hq_ref, hk_ref, hv_ref, lf_ref, lmat_ref, st_ref, a_ref, qt_ref, kh_ref, oacc_ref, f_ref, n_heads,
              with_output):
    lf = lf_ref[...]
    h1 = lf.astype(BF16)
    r1 = lf - h1.astype(F32)
    h2 = r1.astype(BF16)
    h3 = (r1 - h2.astype(F32)).astype(BF16)
    lmat = lmat_ref[...]
    cs = _dot(lmat, h1) + _dot(lmat, h2) + _dot(lmat, h3)
    a = cs[:BLOCK]
    alast = cs[BLOCK:]
    a_ref[...] = alast
    kh_ref[...] = hk_ref[...] * jnp.exp(alast - a)
    if with_output:
        f_ref[...] = jnp.exp(lf)
        qt_ref[...] = hq_ref[...] * jnp.exp(a)
    ridx = lax.broadcasted_iota(jnp.int32, (SUB, 1), 0)

    for i in range(BLOCK // SUB):
        r0 = i * SUB
        rows = pl.ds(r0, SUB)
        for hd in range(n_heads):
            cols = slice(hd * HG_HEAD_DIM, (hd + 1) * HG_HEAD_DIM)
            v_i = hv_ref[rows, cols]
            st = st_ref[hd]
            if with_output:
                o_i = _dot_nt(qt_ref[rows, cols], st)
                e = hq_ref[rows, cols]
                for s in reversed(range(SUB)):
                    if s < SUB - 1:
                        e = jnp.where(ridx > s, e * f_ref[pl.ds(r0 + s + 1, 1), cols], e)
                    sc = jnp.sum(e * hk_ref[pl.ds(r0 + s, 1), cols], axis=-1, keepdims=True)
                    o_i = o_i + jnp.where(ridx >= s, sc, 0.0) * hv_ref[pl.ds(r0 + s, 1), cols]
                oacc_ref[rows, cols] = o_i
            decay = jnp.exp(a_ref[pl.ds(r0, 1), cols])
            st_ref[hd] = st * decay + _dot_tn(v_i, kh_ref[rows, cols])


def _hg_kernel(hq_ref, hk_ref, hv_ref, lf_ref, gate_ref, mk_ref, mv_ref, mlf_ref, gain_ref, lmat_ref, o_ref,
               st_ref, a_ref, qt_ref, kh_ref, oacc_ref, f_ref, *, n_heads):
    scratch = (st_ref, a_ref, qt_ref, kh_ref, oacc_ref, f_ref)

    @pl.when(pl.program_id(1) == 0)
    def _():
        st_ref[...] = jnp.zeros_like(st_ref)
        _hg_block(None, mk_ref, mv_ref, mlf_ref, lmat_ref, *scratch, n_heads, with_output=False)

    _hg_block(hq_ref, hk_ref, hv_ref, lf_ref, lmat_ref, *scratch, n_heads, with_output=True)
    for hd in range(n_heads):
        cols = slice(hd * HG_HEAD_DIM, (hd + 1) * HG_HEAD_DIM)
        o = _rms(oacc_ref[:, cols]) * gain_ref[:, cols] * gate_ref[:, cols]
        o_ref[:, cols] = o.astype(o_ref.dtype)


def _hg_cumsum_matrix():
    t = np.arange(BLOCK)[:, None]
    s = np.arange(BLOCK)[None, :]
    same = (t // SUB) == (s // SUB)
    incl = np.where(same & (s <= t), 1.0, 0.0)
    full = np.where(same, 1.0, 0.0)
    return jnp.asarray(np.concatenate([incl, full], axis=0), dtype=BF16)


def _hgrn2(hq, hk, hv, lf, gate, meta, gain, b, seq):
    t, width = hq.shape
    nc = seq // BLOCK
    n_heads = width // HG_HEAD_DIM
    blk = pl.BlockSpec((BLOCK, width), lambda bi, ci: (bi * nc + ci, 0))
    mblk = pl.BlockSpec((BLOCK, width), lambda bi, ci: (0, 0))
    kern = functools.partial(_hg_kernel, n_heads=n_heads)
    return pl.pallas_call(
        kern,
        grid=(b, nc),
        in_specs=[blk, blk, blk, blk, blk, mblk, mblk, mblk,
                  pl.BlockSpec((1, width), lambda bi, ci: (0, 0)),
                  pl.BlockSpec((2 * BLOCK, BLOCK), lambda bi, ci: (0, 0))],
        out_specs=blk,
        out_shape=jax.ShapeDtypeStruct((t, width), BF16),
        scratch_shapes=[pltpu.VMEM((n_heads, HG_HEAD_DIM, HG_HEAD_DIM), F32),
                        pltpu.VMEM((BLOCK, width), F32), pltpu.VMEM((BLOCK, width), F32),
                        pltpu.VMEM((BLOCK, width), F32), pltpu.VMEM((BLOCK, width), F32),
                        pltpu.VMEM((BLOCK, width), F32)],
        compiler_params=pltpu.CompilerParams(dimension_semantics=("arbitrary", "arbitrary")),
        name="hgrn2",
    )(hq, hk, hv, lf, gate, *meta, gain, _hg_cumsum_matrix())


R_E1, R_E2, R_RANK1, R_RANK2, R_G1, R_G2 = range(6)


def _outproj_kernel(osb_ref, ohg_ref, h_ref, w_ref, g_ref, wr_ref, br_ref, tri_ref,
                    h2_ref, m_ref, route_ref, cnt_ref, carry_ref, *, width):
    @pl.when(pl.program_id(0) == 0)
    def _():
        carry_ref[...] = jnp.zeros_like(carry_ref)

    h2 = h_ref[...] + _dot(osb_ref[...], w_ref[:width, :]) + _dot(ohg_ref[...], w_ref[width:, :])
    h2_ref[...] = h2
    m = _rms(h2) * g_ref[...]
    _tiles_store(m_ref, m, m.shape[1] // LANES)
    lg = _dot(m, wr_ref[...]) + br_ref[...]
    tm = lg.shape[0]
    lane = lax.broadcasted_iota(jnp.int32, (tm, LANES), 1)
    neg = jnp.float32(-1e30)

    def first_argmax(vals):
        vmax = jnp.max(vals, axis=-1, keepdims=True)
        idx = jnp.min(jnp.where(vals == vmax, lane, LANES), axis=-1, keepdims=True)
        return vmax, idx

    is_grp = lane < N_GROUPS
    gl = jnp.where(is_grp, lg, neg)
    gmax, gidx = first_argmax(gl)
    p_grp = 1.0 / jnp.sum(jnp.where(is_grp, jnp.exp(gl - gmax), 0.0), axis=-1, keepdims=True)
    lo = N_GROUPS + gidx * EXPERTS_PER_GROUP
    el = jnp.where((lane >= lo) & (lane < lo + EXPERTS_PER_GROUP), lg, neg)
    v1, i1 = first_argmax(el)
    sel1 = lane == i1
    v2, i2 = first_argmax(jnp.where(sel1, neg, el))
    sel2 = lane == i2
    dlt = jnp.exp(v2 - v1)
    g1 = p_grp / (1.0 + dlt)
    g2 = g1 * dlt

    chosen = jnp.where(sel1 | sel2, 1.0, 0.0)
    carry = carry_ref[0:1, :]
    before = _dot(tri_ref[...], chosen.astype(BF16)) + carry
    r1 = jnp.sum(jnp.where(sel1, before, 0.0), axis=-1, keepdims=True)
    r2 = jnp.sum(jnp.where(sel2, before, 0.0), axis=-1, keepdims=True)
    carry = carry + jnp.sum(chosen, axis=0, keepdims=True)
    carry_ref[0:1, :] = carry
    cnt_ref[...] = jnp.broadcast_to(carry, cnt_ref.shape)

    rec = jnp.zeros((tm, LANES), F32)
    for ln, val in ((R_E1, (i1 - N_GROUPS).astype(F32)), (R_E2, (i2 - N_GROUPS).astype(F32)),
                    (R_RANK1, r1), (R_RANK2, r2), (R_G1, g1), (R_G2, g2)):
        rec = jnp.where(lane == ln, val, rec)
    route_ref[...] = rec


def _outproj(o_sb, o_hg, h, w_out, g_ffn, w_r, b_r, tm):
    t, d = h.shape
    width = o_sb.shape[1]
    row = lambda i: (i, 0)
    const = lambda i: (0, 0)
    tri = jnp.asarray(np.tril(np.ones((tm, tm), np.float32), -1), dtype=BF16)
    kern = functools.partial(_outproj_kernel, width=width)
    return pl.pallas_call(
        kern,
        grid=(t // tm,),
        in_specs=[pl.BlockSpec((tm, width), row), pl.BlockSpec((tm, width), row),
                  pl.BlockSpec((tm, d), row), pl.BlockSpec((2 * width, d), const),
                  pl.BlockSpec((1, d), const), pl.BlockSpec((d, LANES), const),
                  pl.BlockSpec((1, LANES), const), pl.BlockSpec((tm, tm), const)],
        out_specs=[pl.BlockSpec((tm, d), row), pl.BlockSpec((tm * (d // LANES), LANES), row),
                   pl.BlockSpec((tm, LANES), row), pl.BlockSpec((8, LANES), const)],
        out_shape=[jax.ShapeDtypeStruct((t, d), F32), jax.ShapeDtypeStruct((t * (d // LANES), LANES), F32),
                   jax.ShapeDtypeStruct((t, LANES), F32), jax.ShapeDtypeStruct((8, LANES), F32)],
        scratch_shapes=[pltpu.VMEM((8, LANES), F32)],
        compiler_params=pltpu.CompilerParams(
            dimension_semantics=("arbitrary",), vmem_limit_bytes=40 * 1024 * 1024),
        name="outproj",
    )(o_sb, o_hg, h, w_out, g_ffn, w_r, b_r, tri)


def _row_copy(src, dst, sem):
    return pltpu.make_async_copy(src, dst, sem)


def _dispatch_kernel(d1_ref, d2_ref, zs_ref, zn_ref, m_hbm, xs_hbm, zero_ref, stage_ref, sems, lsems, zsem,
                     *, tile, n_tiles, ch):
    i = pl.program_id(0)
    slot = i % 3
    lines = tile * ch

    def tile_wait(s):
        _row_copy(m_hbm.at[pl.ds(0, 2 * lines), :], xs_hbm.at[pl.ds(0, 2 * lines), :], sems.at[s]).wait()

    def tile_load(step, s):
        rows = pl.ds(pl.multiple_of(step * lines, lines), lines)
        return _row_copy(m_hbm.at[rows, :], stage_ref.at[s], lsems.at[s])

    @pl.when(i == 0)
    def _():
        tile_load(0, 0).start()
        zero_ref[...] = jnp.zeros_like(zero_ref)

        def zero_block(j):
            rows = pl.ds(pl.multiple_of(zs_ref[j], MOE_BM * ch), MOE_BM * ch)
            return _row_copy(zero_ref, xs_hbm.at[rows, :], zsem)

        for j in range(2 * N_EXPERTS):
            @pl.when(zn_ref[j] > 0)
            def _():
                zero_block(j).start()
        for j in range(2 * N_EXPERTS):
            @pl.when(zn_ref[j] > 0)
            def _():
                zero_block(j).wait()

    nxt = (i + 1) % 3

    @pl.when(i >= 2)
    def _():
        tile_wait(nxt)

    @pl.when(i + 1 < n_tiles)
    def _():
        tile_load(i + 1, nxt).start()

    tile_load(i, slot).wait()
    base = i * tile
    for r in range(tile):
        src = stage_ref.at[slot, pl.ds(r * ch, ch), :]
        for prio, d_ref in enumerate((d1_ref, d2_ref)):
            dst = xs_hbm.at[pl.ds(pl.multiple_of(d_ref[base + r], ch), ch), :]
            _row_copy(src, dst, sems.at[slot]).start(priority=prio)

    @pl.when(i == n_tiles - 1)
    def _():
        if n_tiles > 1:
            tile_wait((i + 2) % 3)
        tile_wait(slot)


def _dispatch(d1, d2, zero_start, zero_n, m_tiles, n_tok, n_slots, tile=BLOCK):
    ch = m_tiles.shape[0] // n_tok
    n_tiles = n_tok // tile
    kern = functools.partial(_dispatch_kernel, tile=tile, n_tiles=n_tiles, ch=ch)
    grid_spec = pltpu.PrefetchScalarGridSpec(
        num_scalar_prefetch=4,
        grid=(n_tiles,),
        in_specs=[pl.BlockSpec(memory_space=pl.ANY)],
        out_specs=pl.BlockSpec(memory_space=pl.ANY),
        scratch_shapes=[pltpu.VMEM((MOE_BM * ch, LANES), F32), pltpu.VMEM((3, tile * ch, LANES), F32),
                        pltpu.SemaphoreType.DMA((3,)), pltpu.SemaphoreType.DMA((3,)),
                        pltpu.SemaphoreType.DMA(())],
    )
    return pl.pallas_call(
        kern,
        grid_spec=grid_spec,
        out_shape=jax.ShapeDtypeStruct((n_slots * ch, LANES), F32),
        compiler_params=pltpu.CompilerParams(dimension_semantics=("arbitrary",)),
        name="dispatch",
    )(d1, d2, zero_start, zero_n, m_tiles)


def _expert_kernel(be_ref, nu_ref, xs_ref, wg_ref, wu_ref, wd_ref, y_ref, *, ch):
    i = pl.program_id(0)

    @pl.when(i < nu_ref[0])
    def _():
        xs = _tiles_load(xs_ref, MOE_BM, ch)
        gt = _dot(xs, wg_ref[...])
        hb = gt * jax.nn.sigmoid(gt) * _dot(xs, wu_ref[...])
        _tiles_store(y_ref, _dot(hb, wd_ref[...]), ch)

    @pl.when(i >= nu_ref[0])
    def _():
        y_ref[...] = jnp.zeros_like(y_ref)


def _experts(block_e, n_used, xs_tiles, n_slots, w_gate, w_up, w_down):
    d, ff = w_gate.shape[-2:]
    ch = d // LANES
    last = lambda i, be, nu: (jnp.minimum(i, nu[0] - 1), 0)
    grid_spec = pltpu.PrefetchScalarGridSpec(
        num_scalar_prefetch=2,
        grid=(n_slots // MOE_BM,),
        in_specs=[pl.BlockSpec((MOE_BM * ch, LANES), last),
                  pl.BlockSpec((None, d, ff), lambda i, be, nu: (be[i], 0, 0)),
                  pl.BlockSpec((None, d, ff), lambda i, be, nu: (be[i], 0, 0)),
                  pl.BlockSpec((None, ff, d), lambda i, be, nu: (be[i], 0, 0))],
        out_specs=pl.BlockSpec((MOE_BM * ch, LANES), lambda i, be, nu: (i, 0)),
    )
    return pl.pallas_call(
        functools.partial(_expert_kernel, ch=ch),
        grid_spec=grid_spec,
        out_shape=jax.ShapeDtypeStruct((n_slots * ch, LANES), F32),
        compiler_params=pltpu.CompilerParams(
            dimension_semantics=("arbitrary",), vmem_limit_bytes=40 * 1024 * 1024),
        name="experts",
    )(block_e, n_used, xs_tiles, w_gate, w_up, w_down)


def _combine_kernel(d1_ref, d2_ref, h2_ref, route_ref, ys_hbm, g_ref, o_ref, ya_ref, yb_ref, sems,
                    *, tile, n_tiles, ch):
    i = pl.program_id(0)
    slot = i % 2
    lines = tile * ch

    def issue(step, s):
        base = step * tile
        for r in range(tile):
            for prio, (d_ref, y_ref) in enumerate(((d1_ref, ya_ref), (d2_ref, yb_ref))):
                src = ys_hbm.at[pl.ds(pl.multiple_of(d_ref[base + r], ch), ch), :]
                _row_copy(src, y_ref.at[s, pl.ds(r * ch, ch), :], sems.at[s]).start(priority=prio)

    @pl.when(i == 0)
    def _():
        issue(0, 0)

    @pl.when(i + 1 < n_tiles)
    def _():
        issue(i + 1, 1 - slot)

    _row_copy(ys_hbm.at[pl.ds(0, lines), :], ya_ref.at[slot], sems.at[slot]).wait()
    _row_copy(ys_hbm.at[pl.ds(0, lines), :], yb_ref.at[slot], sems.at[slot]).wait()
    rec = route_ref[...]
    g1 = rec[:, R_G1:R_G1 + 1]
    g2 = rec[:, R_G2:R_G2 + 1]
    h = h2_ref[...] + (g1 * _tiles_load(ya_ref.at[slot], tile, ch) + g2 * _tiles_load(yb_ref.at[slot], tile, ch))
    o_ref[...] = _rms(h) * g_ref[...]


def _combine(d1, d2, h2, route, ys, g_final, tile=BLOCK):
    t, d = h2.shape
    ch = d // LANES
    n_tiles = t // tile
    kern = functools.partial(_combine_kernel, tile=tile, n_tiles=n_tiles, ch=ch)
    grid_spec = pltpu.PrefetchScalarGridSpec(
        num_scalar_prefetch=2,
        grid=(n_tiles,),
        in_specs=[pl.BlockSpec((tile, d), lambda i, a, b: (i, 0)),
                  pl.BlockSpec((tile, LANES), lambda i, a, b: (i, 0)),
                  pl.BlockSpec(memory_space=pl.ANY),
                  pl.BlockSpec((1, d), lambda i, a, b: (0, 0))],
        out_specs=pl.BlockSpec((tile, d), lambda i, a, b: (i, 0)),
        scratch_shapes=[pltpu.VMEM((2, tile * ch, LANES), F32), pltpu.VMEM((2, tile * ch, LANES), F32),
                        pltpu.SemaphoreType.DMA((2,))],
    )
    return pl.pallas_call(
        kern,
        grid_spec=grid_spec,
        out_shape=jax.ShapeDtypeStruct((t, d), F32),
        compiler_params=pltpu.CompilerParams(dimension_semantics=("arbitrary",)),
        name="combine",
    )(d1, d2, h2, route, ys, g_final)


def _segment_layout(route, counts_row, n_tok, ch):
    counts = counts_row[N_GROUPS:N_GROUPS + N_EXPERTS].astype(jnp.int32)
    n_slots = n_tok * TOP_K + N_EXPERTS * MOE_BM
    n_blocks = n_slots // MOE_BM
    padded = (counts + MOE_BM - 1) // MOE_BM * MOE_BM
    padded_end = jnp.cumsum(padded)
    padded_start = padded_end - padded
    e1 = route[:, R_E1].astype(jnp.int32)
    e2 = route[:, R_E2].astype(jnp.int32)
    onehot = jnp.arange(N_EXPERTS, dtype=jnp.int32)[None, :]
    start1 = jnp.sum(jnp.where(e1[:, None] == onehot, padded_start[None, :], 0), axis=1)
    start2 = jnp.sum(jnp.where(e2[:, None] == onehot, padded_start[None, :], 0), axis=1)
    d1 = (start1 + route[:, R_RANK1].astype(jnp.int32)) * ch
    d2 = (start2 + route[:, R_RANK2].astype(jnp.int32)) * ch
    block_e = jnp.sum((jnp.arange(n_blocks, dtype=jnp.int32)[:, None] * MOE_BM >= padded_end[None, :])
                      .astype(jnp.int32), axis=1)
    block_e = jnp.minimum(block_e, N_EXPERTS - 1)
    n_used = (padded_end[-1] // MOE_BM).reshape(1)
    trailing = n_used[0] + jnp.arange(N_EXPERTS, dtype=jnp.int32)
    zero_start = jnp.concatenate([jnp.maximum(padded_end - MOE_BM, 0),
                                  jnp.minimum(trailing, n_blocks - 1) * MOE_BM]) * ch
    zero_flag = jnp.concatenate([counts, (trailing < n_blocks).astype(jnp.int32)])
    return d1, d2, block_e, n_used, zero_start, zero_flag, n_slots


def kernel(x, meta_tokens, lb_logits, g_mix, w_in, sb_gain, hg_gain, w_out, g_ffn, w_router_group,
           b_router_group, w_router_expert, b_router_expert, w_expert_gate, w_expert_up, w_expert_down,
           g_final):
    b, seq, d = x.shape
    depth = w_in.shape[0]
    assert depth == 1, "single-layer block"
    assert seq % (Q_SUB * BLOCK) == 0
    t = b * seq
    tm = 512
    layer = 0

    xr = x.reshape(t, d)
    prefix = jnp.concatenate([jnp.zeros((LEAD_PAD, d), x.dtype), meta_tokens.astype(x.dtype)], axis=0)
    prefix_mask = (jnp.arange(BLOCK) >= LEAD_PAD).astype(F32)[:, None]
    lower_bounds = jnp.cumsum(jax.nn.softmax(lb_logits.astype(F32), axis=0), axis=0)
    lb = lower_bounds[layer][None, :]
    g_mix_l = g_mix[layer][None, :]
    w_in_l = w_in[layer].astype(BF16)

    q, k0, k1, v0, v1, hq, hk, hv, lf, hg = _proj(xr, jnp.ones((t, 1), F32), g_mix_l, w_in_l, lb, tm)
    _, mk0, mk1, mv0, mv1, _, mhk, mhv, mlf, _ = _proj(prefix, prefix_mask, g_mix_l, w_in_l, lb, BLOCK)

    o_sb = _sb_attention(q, (k0, k1, v0, v1), (mk0, mk1, mv0, mv1), sb_gain[layer][None, :], b, seq)
    o_hg = _hgrn2(hq, hk, hv, lf, hg, (mhk, mhv, mlf), hg_gain[layer][None, :], b, seq)

    w_r = jnp.zeros((d, LANES), F32)
    w_r = w_r.at[:, :N_GROUPS].set(w_router_group[layer])
    w_r = w_r.at[:, N_GROUPS:N_GROUPS + N_EXPERTS].set(w_router_expert[layer])
    b_r = jnp.zeros((1, LANES), F32)
    b_r = b_r.at[0, :N_GROUPS].set(b_router_group[layer])
    b_r = b_r.at[0, N_GROUPS:N_GROUPS + N_EXPERTS].set(b_router_expert[layer])
    h2, m_tiles, route, counts = _outproj(o_sb, o_hg, xr, w_out[layer].astype(BF16), g_ffn[layer][None, :],
                                          w_r, b_r, tm)

    d1, d2, block_e, n_used, zero_start, zero_n, n_slots = _segment_layout(route, counts[0], t, d // LANES)
    xs_tiles = _dispatch(d1, d2, zero_start, zero_n, m_tiles, t, n_slots)
    ys_tiles = _experts(block_e, n_used, xs_tiles, n_slots,
                        w_expert_gate[layer], w_expert_up[layer], w_expert_down[layer])
    out = _combine(d1, d2, h2, route, ys_tiles, g_final[None, :])
    return out.reshape(b, seq, d)
```

```python
import functools

import numpy as np
import jax
import jax.numpy as jnp
from jax import lax
from jax.experimental import pallas as pl
from jax.experimental.pallas import tpu as pltpu

BLOCK = 128
N_META = 16
LEAD_PAD = BLOCK - N_META
SB_HEAD_DIM = 64
HG_HEAD_DIM = 128
SUB = 16
HG_BLOCKS = 2
Q_SUB = 4
PAIRS = 2
DEAD_LOG_TAIL = -104.0
N_GROUPS = 4
EXPERTS_PER_GROUP = 8
N_EXPERTS = N_GROUPS * EXPERTS_PER_GROUP
TOP_K = 2
MOE_BM = 512
EPS = 1e-6
LANES = 128

F32 = jnp.float32
BF16 = jnp.bfloat16


def _dot(a, b):
    return jnp.dot(a, b, preferred_element_type=F32)


def _dot_nt(a, b):
    return lax.dot_general(a, b, (((1,), (1,)), ((), ())), preferred_element_type=F32)


def _dot_tn(a, b):
    return lax.dot_general(a, b, (((0,), (0,)), ((), ())), preferred_element_type=F32)


def _rms(x):
    return x * lax.rsqrt(jnp.mean(x * x, axis=-1, keepdims=True) + EPS)


def _tiles_load(ref, n, chunks):
    return jnp.concatenate([ref[pl.ds(c, n, stride=chunks), :] for c in range(chunks)], axis=1)


def _tiles_store(ref, x, chunks):
    n = x.shape[0]
    for c in range(chunks):
        ref[pl.ds(c, n, stride=chunks), :] = x[:, c * LANES:(c + 1) * LANES]


def _proj_kernel(h_ref, mask_ref, g_ref, w_ref, lb_ref,
                 q_ref, k0_ref, k1_ref, v0_ref, v1_ref,
                 hq_ref, hk_ref, hv_ref, lf_ref, hg_ref, *, width):
    a = (_rms(h_ref[...]) * g_ref[...]).astype(BF16)

    def p(i):
        return _dot(a, w_ref[:, i * width:(i + 1) * width])

    lane = lax.broadcasted_iota(jnp.int32, (1, width), 1)
    head0 = (lane & (LANES - 1)) < SB_HEAD_DIM
    q_ref[...] = (p(0) * (SB_HEAD_DIM ** -0.5)).astype(BF16)
    k = p(1)
    k0_ref[...] = jnp.where(head0, k, 0.0).astype(BF16)
    k1_ref[...] = jnp.where(head0, 0.0, k).astype(BF16)
    v = p(2)
    v0_ref[...] = jnp.where(head0, v, 0.0).astype(BF16)
    v1_ref[...] = jnp.where(head0, 0.0, v).astype(BF16)
    hq = p(3)
    hq_ref[...] = hq * jax.nn.sigmoid(hq)
    f = p(4)
    lb = lb_ref[...]
    sig = jax.nn.sigmoid(f)
    lf_ref[...] = jnp.log(lb + (1.0 - lb) * sig)
    hk_ref[...] = mask_ref[...] * ((1.0 - lb) * jax.nn.sigmoid(-f))
    hv_ref[...] = p(5)
    g = p(6)
    hg_ref[...] = g * jax.nn.sigmoid(g)


def _proj(h, mask, g_mix, w_in, lb, tm):
    t, d = h.shape
    width = d // 2
    kern = functools.partial(_proj_kernel, width=width)
    row = lambda i: (i, 0)
    const = lambda i: (0, 0)
    outs = ([jax.ShapeDtypeStruct((t, width), BF16)] * 5
            + [jax.ShapeDtypeStruct((t, width), F32)] * 5)
    return pl.pallas_call(
        kern,
        grid=(t // tm,),
        in_specs=[pl.BlockSpec((tm, d), row), pl.BlockSpec((tm, 1), row),
                  pl.BlockSpec((1, d), const), pl.BlockSpec((d, 7 * width), const),
                  pl.BlockSpec((1, width), const)],
        out_specs=[pl.BlockSpec((tm, width), row)] * 10,
        out_shape=outs,
        compiler_params=pltpu.CompilerParams(
            dimension_semantics=("arbitrary",), vmem_limit_bytes=52 * 1024 * 1024),
        name="proj",
    )(h, mask, g_mix, w_in, lb)


def _sb_kernel(q_ref, k0_ref, k1_ref, v0_ref, v1_ref, mk0_ref, mk1_ref, mv0_ref, mv1_ref,
               tt_ref, gain_ref, o_ref, c_ref, acc_ref, zz_ref, w_ref):
    tq = Q_SUB * BLOCK
    base = pl.program_id(2) * Q_SUB
    c_ref[...] = jnp.zeros_like(c_ref)
    acc_ref[...] = jnp.zeros_like(acc_ref)

    def col_iota(rows):
        return lax.broadcasted_iota(jnp.int32, (rows, 2 * BLOCK), 1) & (BLOCK - 1)

    def lanes(p):
        return slice(p * LANES, (p + 1) * LANES)

    def k_real(p, kb):
        rows = pl.ds(pl.multiple_of(kb * BLOCK, BLOCK), BLOCK)
        return jnp.concatenate([k0_ref[rows, lanes(p)], k1_ref[rows, lanes(p)]], axis=0)

    def v_real(p, kb):
        rows = pl.ds(pl.multiple_of(kb * BLOCK, BLOCK), BLOCK)
        return jnp.concatenate([v0_ref[rows, lanes(p)], v1_ref[rows, lanes(p)]], axis=0)

    def k_real_or_prefix(p, kb):
        k_prefix = jnp.concatenate([mk0_ref[:, lanes(p)], mk1_ref[:, lanes(p)]], axis=0)
        return jnp.where(kb >= 0, k_real(p, jnp.maximum(kb, 0)), k_prefix)

    def v_prefix(p):
        return jnp.concatenate([mv0_ref[:, lanes(p)], mv1_ref[:, lanes(p)]], axis=0)

    def scores(p, k2, lo):
        zz_ref[p, lo:, :] = _dot_nt(q_ref[lo:, lanes(p)], k2)

    def values(p, v2, lo):
        acc_ref[p, lo:, :] += _dot(w_ref[p, lo:, :], v2)

    def step(mask, k_next, v_prev, lo=0, lo_next=0, lo_prev=0):
        for p in range(PAIRS):
            zz = zz_ref[p, lo:, :]
            if k_next is not None:
                scores(p, k_next(p), lo_next)
            if v_prev is not None:
                values(p, v_prev(p), lo_prev)
            neg_abs = lax.bitcast_convert_type(
                lax.bitcast_convert_type(zz, jnp.uint32) | jnp.uint32(0x80000000), F32)
            sp = jnp.maximum(zz, 0.0) + jnp.log(1.0 + jnp.exp(neg_abs))
            spm = sp if mask is None else jnp.where(mask, sp, 0.0)
            spb = spm.astype(BF16)
            tt = tt_ref[...]
            tail = jnp.concatenate(
                [_dot(spb[:, h * BLOCK:(h + 1) * BLOCK], tt) for h in range(2)], axis=1)
            c = c_ref[p, lo:, :]
            w = jnp.exp(zz - sp + tail + c)
            if mask is not None:
                w = jnp.where(mask, w, 0.0)
            w_ref[p, lo:, :] = w.astype(BF16)
            tot0 = jnp.sum(spm[:, :BLOCK], axis=-1, keepdims=True)
            tot1 = jnp.sum(spm[:, BLOCK:], axis=-1, keepdims=True)
            c_ref[p, lo:, :] = c - jnp.concatenate([jnp.broadcast_to(tot0, (tq - lo, BLOCK)),
                                                    jnp.broadcast_to(tot1, (tq - lo, BLOCK))], axis=1)

    top = base + Q_SUB - 1
    for p in range(PAIRS):
        scores(p, k_real(p, top), (Q_SUB - 1) * BLOCK)
    for jj in reversed(range(Q_SUB)):
        kb = base + jj
        lo = jj * BLOCK
        step(col_iota(tq - lo) < lax.broadcasted_iota(jnp.int32, (tq - lo, 2 * BLOCK), 0),
             (lambda p, kb=kb: k_real(p, kb - 1)) if jj > 0 else (lambda p, kb=kb: k_real_or_prefix(p, kb - 1)),
             (lambda p, kb=kb: v_real(p, kb + 1)) if jj < Q_SUB - 1 else None,
             lo=lo, lo_next=max(lo - BLOCK, 0), lo_prev=lo + BLOCK)

    def alive():
        return jnp.max(c_ref[...]) > DEAD_LOG_TAIL

    def cond(carry):
        kb, live = carry
        return jnp.logical_and(kb >= 0, live)

    def body(carry):
        kb, _ = carry
        step(None, lambda p: k_real_or_prefix(p, kb - 1), lambda p: v_real(p, kb + 1))
        return kb - 1, alive()

    kb, live = lax.while_loop(cond, body, (base - 1, alive()))
    for p in range(PAIRS):
        values(p, v_real(p, kb + 1), 0)

    @pl.when(jnp.logical_and(kb < 0, live))
    def _():
        step(col_iota(tq) >= LEAD_PAD, None, None)
        for p in range(PAIRS):
            values(p, v_prefix(p), 0)

    head0 = lax.broadcasted_iota(jnp.int32, (tq, LANES), 1) < SB_HEAD_DIM
    for p in range(PAIRS):
        o = acc_ref[p]
        o2 = o * o
        s0 = jnp.sum(jnp.where(head0, o2, 0.0), axis=-1, keepdims=True) * (1.0 / SB_HEAD_DIM)
        s1 = jnp.sum(jnp.where(head0, 0.0, o2), axis=-1, keepdims=True) * (1.0 / SB_HEAD_DIM)
        r = jnp.where(head0, lax.rsqrt(s0 + EPS), lax.rsqrt(s1 + EPS))
        o_ref[:, lanes(p)] = (o * r * gain_ref[:, lanes(p)]).astype(o_ref.dtype)


def _sb_tail_matrix():
    j = np.arange(BLOCK)[:, None]
    s = np.arange(BLOCK)[None, :]
    return jnp.asarray(np.where(j > s, -1.0, 0.0), dtype=BF16)


def _sb_attention(q, kv, kv_meta, gain, b, seq):
    t, width = q.shape
    nqt = seq // (Q_SUB * BLOCK)
    gw = PAIRS * LANES
    ngroup = width // gw
    kv3 = [a.reshape(b, seq, width) for a in kv]
    tq = Q_SUB * BLOCK
    kv_spec = pl.BlockSpec((None, seq, gw), lambda bi, hp, qt: (bi, 0, hp))
    meta_spec = pl.BlockSpec((BLOCK, gw), lambda bi, hp, qt: (0, hp))
    return pl.pallas_call(
        _sb_kernel,
        grid=(b, ngroup, nqt),
        in_specs=[pl.BlockSpec((tq, gw), lambda bi, hp, qt: (bi * nqt + qt, hp)),
                  kv_spec, kv_spec, kv_spec, kv_spec,
                  meta_spec, meta_spec, meta_spec, meta_spec,
                  pl.BlockSpec((BLOCK, BLOCK), lambda bi, hp, qt: (0, 0)),
                  pl.BlockSpec((1, gw), lambda bi, hp, qt: (0, hp))],
        out_specs=pl.BlockSpec((tq, gw), lambda bi, hp, qt: (bi * nqt + qt, hp)),
        out_shape=jax.ShapeDtypeStruct((t, width), BF16),
        scratch_shapes=[pltpu.VMEM((PAIRS, tq, 2 * BLOCK), F32), pltpu.VMEM((PAIRS, tq, LANES), F32),
                        pltpu.VMEM((PAIRS, tq, 2 * BLOCK), F32), pltpu.VMEM((PAIRS, tq, 2 * BLOCK), BF16)],
        compiler_params=pltpu.CompilerParams(
            dimension_semantics=("arbitrary", "arbitrary", "arbitrary"),
            vmem_limit_bytes=40 * 1024 * 1024),
        name="sb_attn",
    )(q, *kv3, *kv_meta, _sb_tail_matrix(), gain)


def _hg_block(hq_ref, hk_ref, hv_ref, lf_ref, lmat_ref, st_ref, a_ref, qt_ref, kh_ref, oacc_ref, f_ref, n_heads,
              with_output):
    lf = lf_ref[...]
    h1 = lf.astype(BF16)
    r1 = lf - h1.astype(F32)
    h2 = r1.astype(BF16)
    h3 = (r1 - h2.astype(F32)).astype(BF16)
    lmat = lmat_ref[...]
    cs = _dot(lmat, h1) + _dot(lmat, h2) + _dot(lmat, h3)
    a = cs[:BLOCK]
    alast = cs[BLOCK:]
    a_ref[...] = alast
    kh_ref[...] = hk_ref[...] * jnp.exp(alast - a)
    if with_output:
        f_ref[...] = jnp.exp(lf)
        qt_ref[...] = hq_ref[...] * jnp.exp(a)
    ridx = lax.broadcasted_iota(jnp.int32, (SUB, 1), 0)

    for i in range(BLOCK // SUB):
        r0 = i * SUB
        rows = pl.ds(r0, SUB)
        for hd in range(n_heads):
            cols = slice(hd * HG_HEAD_DIM, (hd + 1) * HG_HEAD_DIM)
            v_i = hv_ref[rows, cols]
            st = st_ref[hd]
            if with_output:
                o_i = _dot_nt(qt_ref[rows, cols], st)
                e = hq_ref[rows, cols]
                for s in reversed(range(SUB)):
                    if s < SUB - 1:
                        e = jnp.where(ridx > s, e * f_ref[pl.ds(r0 + s + 1, 1), cols], e)
                    sc = jnp.sum(e * hk_ref[pl.ds(r0 + s, 1), cols], axis=-1, keepdims=True)
                    o_i = o_i + jnp.where(ridx >= s, sc, 0.0) * hv_ref[pl.ds(r0 + s, 1), cols]
                oacc_ref[rows, cols] = o_i
            decay = jnp.exp(a_ref[pl.ds(r0, 1), cols])
            st_ref[hd] = st * decay + _dot_tn(v_i, kh_ref[rows, cols])


def _hg_kernel(hq_ref, hk_ref, hv_ref, lf_ref, gate_ref, mk_ref, mv_ref, mlf_ref, gain_ref, lmat_ref, o_ref,
               st_ref, a_ref, qt_ref, kh_ref, oacc_ref, f_ref, *, n_heads):
    scratch = (st_ref, a_ref, qt_ref, kh_ref, oacc_ref, f_ref)

    @pl.when(pl.program_id(1) == 0)
    def _():
        st_ref[...] = jnp.zeros_like(st_ref)
        _hg_block(None, mk_ref, mv_ref, mlf_ref, lmat_ref, *scratch, n_heads, with_output=False)

    for j in range(HG_BLOCKS):
        rows = pl.ds(j * BLOCK, BLOCK)
        _hg_block(hq_ref.at[rows, :], hk_ref.at[rows, :], hv_ref.at[rows, :], lf_ref.at[rows, :], lmat_ref,
                  *scratch, n_heads, with_output=True)
        for hd in range(n_heads):
            cols = slice(hd * HG_HEAD_DIM, (hd + 1) * HG_HEAD_DIM)
            o = _rms(oacc_ref[:, cols]) * gain_ref[:, cols] * gate_ref[rows, cols]
            o_ref[rows, cols] = o.astype(o_ref.dtype)


def _hg_cumsum_matrix():
    t = np.arange(BLOCK)[:, None]
    s = np.arange(BLOCK)[None, :]
    same = (t // SUB) == (s // SUB)
    incl = np.where(same & (s <= t), 1.0, 0.0)
    full = np.where(same, 1.0, 0.0)
    return jnp.asarray(np.concatenate([incl, full], axis=0), dtype=BF16)


def _hgrn2(hq, hk, hv, lf, gate, meta, gain, b, seq):
    t, width = hq.shape
    rows = HG_BLOCKS * BLOCK
    nc = seq // rows
    n_heads = width // HG_HEAD_DIM
    blk = pl.BlockSpec((rows, width), lambda bi, ci: (bi * nc + ci, 0))
    mblk = pl.BlockSpec((BLOCK, width), lambda bi, ci: (0, 0))
    kern = functools.partial(_hg_kernel, n_heads=n_heads)
    return pl.pallas_call(
        kern,
        grid=(b, nc),
        in_specs=[blk, blk, blk, blk, blk, mblk, mblk, mblk,
                  pl.BlockSpec((1, width), lambda bi, ci: (0, 0)),
                  pl.BlockSpec((2 * BLOCK, BLOCK), lambda bi, ci: (0, 0))],
        out_specs=blk,
        out_shape=jax.ShapeDtypeStruct((t, width), BF16),
        scratch_shapes=[pltpu.VMEM((n_heads, HG_HEAD_DIM, HG_HEAD_DIM), F32),
                        pltpu.VMEM((BLOCK, width), F32), pltpu.VMEM((BLOCK, width), F32),
                        pltpu.VMEM((BLOCK, width), F32), pltpu.VMEM((BLOCK, width), F32),
                        pltpu.VMEM((BLOCK, width), F32)],
        compiler_params=pltpu.CompilerParams(dimension_semantics=("arbitrary", "arbitrary")),
        name="hgrn2",
    )(hq, hk, hv, lf, gate, *meta, gain, _hg_cumsum_matrix())


R_E1, R_E2, R_RANK1, R_RANK2, R_G1, R_G2 = range(6)


def _outproj_kernel(osb_ref, ohg_ref, h_ref, w_ref, g_ref, wr_ref, br_ref, tri_ref,
                    h2_ref, m_ref, route_ref, cnt_ref, carry_ref, *, width):
    @pl.when(pl.program_id(0) == 0)
    def _():
        carry_ref[...] = jnp.zeros_like(carry_ref)

    h2 = h_ref[...] + _dot(osb_ref[...], w_ref[:width, :]) + _dot(ohg_ref[...], w_ref[width:, :])
    h2_ref[...] = h2
    m = _rms(h2) * g_ref[...]
    _tiles_store(m_ref, m, m.shape[1] // LANES)
    lg = _dot(m, wr_ref[...]) + br_ref[...]
    tm = lg.shape[0]
    lane = lax.broadcasted_iota(jnp.int32, (tm, LANES), 1)
    neg = jnp.float32(-1e30)

    def first_argmax(vals):
        vmax = jnp.max(vals, axis=-1, keepdims=True)
        idx = jnp.min(jnp.where(vals == vmax, lane, LANES), axis=-1, keepdims=True)
        return vmax, idx

    is_grp = lane < N_GROUPS
    gl = jnp.where(is_grp, lg, neg)
    gmax, gidx = first_argmax(gl)
    p_grp = 1.0 / jnp.sum(jnp.where(is_grp, jnp.exp(gl - gmax), 0.0), axis=-1, keepdims=True)
    lo = N_GROUPS + gidx * EXPERTS_PER_GROUP
    el = jnp.where((lane >= lo) & (lane < lo + EXPERTS_PER_GROUP), lg, neg)
    v1, i1 = first_argmax(el)
    sel1 = lane == i1
    v2, i2 = first_argmax(jnp.where(sel1, neg, el))
    sel2 = lane == i2
    dlt = jnp.exp(v2 - v1)
    g1 = p_grp / (1.0 + dlt)
    g2 = g1 * dlt

    chosen = jnp.where(sel1 | sel2, 1.0, 0.0)
    carry = carry_ref[0:1, :]
    before = _dot(tri_ref[...], chosen.astype(BF16)) + carry
    r1 = jnp.sum(jnp.where(sel1, before, 0.0), axis=-1, keepdims=True)
    r2 = jnp.sum(jnp.where(sel2, before, 0.0), axis=-1, keepdims=True)
    carry = carry + jnp.sum(chosen, axis=0, keepdims=True)
    carry_ref[0:1, :] = carry
    cnt_ref[...] = jnp.broadcast_to(carry, cnt_ref.shape)

    rec = jnp.zeros((tm, LANES), F32)
    for ln, val in ((R_E1, (i1 - N_GROUPS).astype(F32)), (R_E2, (i2 - N_GROUPS).astype(F32)),
                    (R_RANK1, r1), (R_RANK2, r2), (R_G1, g1), (R_G2, g2)):
        rec = jnp.where(lane == ln, val, rec)
    route_ref[...] = rec


def _outproj(o_sb, o_hg, h, w_out, g_ffn, w_r, b_r, tm):
    t, d = h.shape
    width = o_sb.shape[1]
    row = lambda i: (i, 0)
    const = lambda i: (0, 0)
    tri = jnp.asarray(np.tril(np.ones((tm, tm), np.float32), -1), dtype=BF16)
    kern = functools.partial(_outproj_kernel, width=width)
    return pl.pallas_call(
        kern,
        grid=(t // tm,),
        in_specs=[pl.BlockSpec((tm, width), row), pl.BlockSpec((tm, width), row),
                  pl.BlockSpec((tm, d), row), pl.BlockSpec((2 * width, d), const),
                  pl.BlockSpec((1, d), const), pl.BlockSpec((d, LANES), const),
                  pl.BlockSpec((1, LANES), const), pl.BlockSpec((tm, tm), const)],
        out_specs=[pl.BlockSpec((tm, d), row), pl.BlockSpec((tm * (d // LANES), LANES), row),
                   pl.BlockSpec((tm, LANES), row), pl.BlockSpec((8, LANES), const)],
        out_shape=[jax.ShapeDtypeStruct((t, d), F32), jax.ShapeDtypeStruct((t * (d // LANES), LANES), F32),
                   jax.ShapeDtypeStruct((t, LANES), F32), jax.ShapeDtypeStruct((8, LANES), F32)],
        scratch_shapes=[pltpu.VMEM((8, LANES), F32)],
        compiler_params=pltpu.CompilerParams(
            dimension_semantics=("arbitrary",), vmem_limit_bytes=40 * 1024 * 1024),
        name="outproj",
    )(o_sb, o_hg, h, w_out, g_ffn, w_r, b_r, tri)


def _row_copy(src, dst, sem):
    return pltpu.make_async_copy(src, dst, sem)


def _dispatch_kernel(d1_ref, d2_ref, zs_ref, zn_ref, m_hbm, xs_hbm, zero_ref, stage_ref, sems, lsems, zsem,
                     *, tile, n_tiles, ch):
    i = pl.program_id(0)
    slot = i % 3
    lines = tile * ch

    def tile_wait(s):
        _row_copy(m_hbm.at[pl.ds(0, 2 * lines), :], xs_hbm.at[pl.ds(0, 2 * lines), :], sems.at[s]).wait()

    def tile_load(step, s):
        rows = pl.ds(pl.multiple_of(step * lines, lines), lines)
        return _row_copy(m_hbm.at[rows, :], stage_ref.at[s], lsems.at[s])

    @pl.when(i == 0)
    def _():
        tile_load(0, 0).start()
        zero_ref[...] = jnp.zeros_like(zero_ref)

        def zero_block(j):
            rows = pl.ds(pl.multiple_of(zs_ref[j], MOE_BM * ch), MOE_BM * ch)
            return _row_copy(zero_ref, xs_hbm.at[rows, :], zsem)

        for j in range(2 * N_EXPERTS):
            @pl.when(zn_ref[j] > 0)
            def _():
                zero_block(j).start()
        for j in range(2 * N_EXPERTS):
            @pl.when(zn_ref[j] > 0)
            def _():
                zero_block(j).wait()

    nxt = (i + 1) % 3

    @pl.when(i >= 2)
    def _():
        tile_wait(nxt)

    @pl.when(i + 1 < n_tiles)
    def _():
        tile_load(i + 1, nxt).start()

    tile_load(i, slot).wait()
    base = i * tile
    for r in range(tile):
        src = stage_ref.at[slot, pl.ds(r * ch, ch), :]
        for prio, d_ref in enumerate((d1_ref, d2_ref)):
            dst = xs_hbm.at[pl.ds(pl.multiple_of(d_ref[base + r], ch), ch), :]
            _row_copy(src, dst, sems.at[slot]).start(priority=prio)

    @pl.when(i == n_tiles - 1)
    def _():
        if n_tiles > 1:
            tile_wait((i + 2) % 3)
        tile_wait(slot)


def _dispatch(d1, d2, zero_start, zero_n, m_tiles, n_tok, n_slots, tile=BLOCK):
    ch = m_tiles.shape[0] // n_tok
    n_tiles = n_tok // tile
    kern = functools.partial(_dispatch_kernel, tile=tile, n_tiles=n_tiles, ch=ch)
    grid_spec = pltpu.PrefetchScalarGridSpec(
        num_scalar_prefetch=4,
        grid=(n_tiles,),
        in_specs=[pl.BlockSpec(memory_space=pl.ANY)],
        out_specs=pl.BlockSpec(memory_space=pl.ANY),
        scratch_shapes=[pltpu.VMEM((MOE_BM * ch, LANES), F32), pltpu.VMEM((3, tile * ch, LANES), F32),
                        pltpu.SemaphoreType.DMA((3,)), pltpu.SemaphoreType.DMA((3,)),
                        pltpu.SemaphoreType.DMA(())],
    )
    return pl.pallas_call(
        kern,
        grid_spec=grid_spec,
        out_shape=jax.ShapeDtypeStruct((n_slots * ch, LANES), F32),
        compiler_params=pltpu.CompilerParams(dimension_semantics=("arbitrary",)),
        name="dispatch",
    )(d1, d2, zero_start, zero_n, m_tiles)


def _expert_kernel(be_ref, nu_ref, xs_ref, wg_ref, wu_ref, wd_ref, y_ref, *, ch):
    i = pl.program_id(0)

    @pl.when(i < nu_ref[0])
    def _():
        xs = _tiles_load(xs_ref, MOE_BM, ch)
        gt = _dot(xs, wg_ref[...])
        hb = gt * jax.nn.sigmoid(gt) * _dot(xs, wu_ref[...])
        _tiles_store(y_ref, _dot(hb, wd_ref[...]), ch)

    @pl.when(i >= nu_ref[0])
    def _():
        y_ref[...] = jnp.zeros_like(y_ref)


def _experts(block_e, n_used, xs_tiles, n_slots, w_gate, w_up, w_down):
    d, ff = w_gate.shape[-2:]
    ch = d // LANES
    last = lambda i, be, nu: (jnp.minimum(i, nu[0] - 1), 0)
    grid_spec = pltpu.PrefetchScalarGridSpec(
        num_scalar_prefetch=2,
        grid=(n_slots // MOE_BM,),
        in_specs=[pl.BlockSpec((MOE_BM * ch, LANES), last),
                  pl.BlockSpec((None, d, ff), lambda i, be, nu: (be[i], 0, 0)),
                  pl.BlockSpec((None, d, ff), lambda i, be, nu: (be[i], 0, 0)),
                  pl.BlockSpec((None, ff, d), lambda i, be, nu: (be[i], 0, 0))],
        out_specs=pl.BlockSpec((MOE_BM * ch, LANES), lambda i, be, nu: (i, 0)),
    )
    return pl.pallas_call(
        functools.partial(_expert_kernel, ch=ch),
        grid_spec=grid_spec,
        out_shape=jax.ShapeDtypeStruct((n_slots * ch, LANES), F32),
        compiler_params=pltpu.CompilerParams(
            dimension_semantics=("arbitrary",), vmem_limit_bytes=40 * 1024 * 1024),
        name="experts",
    )(block_e, n_used, xs_tiles, w_gate, w_up, w_down)


def _combine_kernel(d1_ref, d2_ref, h2_ref, route_ref, ys_hbm, g_ref, o_ref, ya_ref, yb_ref, sems,
                    *, tile, n_tiles, ch):
    i = pl.program_id(0)
    slot = i % 2
    lines = tile * ch

    def issue(step, s):
        base = step * tile
        for r in range(tile):
            for prio, (d_ref, y_ref) in enumerate(((d1_ref, ya_ref), (d2_ref, yb_ref))):
                src = ys_hbm.at[pl.ds(pl.multiple_of(d_ref[base + r], ch), ch), :]
                _row_copy(src, y_ref.at[s, pl.ds(r * ch, ch), :], sems.at[s]).start(priority=prio)

    @pl.when(i == 0)
    def _():
        issue(0, 0)

    @pl.when(i + 1 < n_tiles)
    def _():
        issue(i + 1, 1 - slot)

    _row_copy(ys_hbm.at[pl.ds(0, lines), :], ya_ref.at[slot], sems.at[slot]).wait()
    _row_copy(ys_hbm.at[pl.ds(0, lines), :], yb_ref.at[slot], sems.at[slot]).wait()
    rec = route_ref[...]
    g1 = rec[:, R_G1:R_G1 + 1]
    g2 = rec[:, R_G2:R_G2 + 1]
    h = h2_ref[...] + (g1 * _tiles_load(ya_ref.at[slot], tile, ch) + g2 * _tiles_load(yb_ref.at[slot], tile, ch))
    o_ref[...] = _rms(h) * g_ref[...]


def _combine(d1, d2, h2, route, ys, g_final, tile=BLOCK):
    t, d = h2.shape
    ch = d // LANES
    n_tiles = t // tile
    kern = functools.partial(_combine_kernel, tile=tile, n_tiles=n_tiles, ch=ch)
    grid_spec = pltpu.PrefetchScalarGridSpec(
        num_scalar_prefetch=2,
        grid=(n_tiles,),
        in_specs=[pl.BlockSpec((tile, d), lambda i, a, b: (i, 0)),
                  pl.BlockSpec((tile, LANES), lambda i, a, b: (i, 0)),
                  pl.BlockSpec(memory_space=pl.ANY),
                  pl.BlockSpec((1, d), lambda i, a, b: (0, 0))],
        out_specs=pl.BlockSpec((tile, d), lambda i, a, b: (i, 0)),
        scratch_shapes=[pltpu.VMEM((2, tile * ch, LANES), F32), pltpu.VMEM((2, tile * ch, LANES), F32),
                        pltpu.SemaphoreType.DMA((2,))],
    )
    return pl.pallas_call(
        kern,
        grid_spec=grid_spec,
        out_shape=jax.ShapeDtypeStruct((t, d), F32),
        compiler_params=pltpu.CompilerParams(dimension_semantics=("arbitrary",)),
        name="combine",
    )(d1, d2, h2, route, ys, g_final)


def _segment_layout(route, counts_row, n_tok, ch):
    counts = counts_row[N_GROUPS:N_GROUPS + N_EXPERTS].astype(jnp.int32)
    n_slots = n_tok * TOP_K + N_EXPERTS * MOE_BM
    n_blocks = n_slots // MOE_BM
    padded = (counts + MOE_BM - 1) // MOE_BM * MOE_BM
    padded_end = jnp.cumsum(padded)
    padded_start = padded_end - padded
    e1 = route[:, R_E1].astype(jnp.int32)
    e2 = route[:, R_E2].astype(jnp.int32)
    onehot = jnp.arange(N_EXPERTS, dtype=jnp.int32)[None, :]
    start1 = jnp.sum(jnp.where(e1[:, None] == onehot, padded_start[None, :], 0), axis=1)
    start2 = jnp.sum(jnp.where(e2[:, None] == onehot, padded_start[None, :], 0), axis=1)
    d1 = (start1 + route[:, R_RANK1].astype(jnp.int32)) * ch
    d2 = (start2 + route[:, R_RANK2].astype(jnp.int32)) * ch
    block_e = jnp.sum((jnp.arange(n_blocks, dtype=jnp.int32)[:, None] * MOE_BM >= padded_end[None, :])
                      .astype(jnp.int32), axis=1)
    block_e = jnp.minimum(block_e, N_EXPERTS - 1)
    n_used = (padded_end[-1] // MOE_BM).reshape(1)
    trailing = n_used[0] + jnp.arange(N_EXPERTS, dtype=jnp.int32)
    zero_start = jnp.concatenate([jnp.maximum(padded_end - MOE_BM, 0),
                                  jnp.minimum(trailing, n_blocks - 1) * MOE_BM]) * ch
    zero_flag = jnp.concatenate([counts, (trailing < n_blocks).astype(jnp.int32)])
    return d1, d2, block_e, n_used, zero_start, zero_flag, n_slots


def kernel(x, meta_tokens, lb_logits, g_mix, w_in, sb_gain, hg_gain, w_out, g_ffn, w_router_group,
           b_router_group, w_router_expert, b_router_expert, w_expert_gate, w_expert_up, w_expert_down,
           g_final):
    b, seq, d = x.shape
    depth = w_in.shape[0]
    assert depth == 1, "single-layer block"
    assert seq % (Q_SUB * BLOCK) == 0
    t = b * seq
    tm = 512
    layer = 0

    xr = x.reshape(t, d)
    prefix = jnp.concatenate([jnp.zeros((LEAD_PAD, d), x.dtype), meta_tokens.astype(x.dtype)], axis=0)
    prefix_mask = (jnp.arange(BLOCK) >= LEAD_PAD).astype(F32)[:, None]
    lower_bounds = jnp.cumsum(jax.nn.softmax(lb_logits.astype(F32), axis=0), axis=0)
    lb = lower_bounds[layer][None, :]
    g_mix_l = g_mix[layer][None, :]
    w_in_l = w_in[layer].astype(BF16)

    q, k0, k1, v0, v1, hq, hk, hv, lf, hg = _proj(xr, jnp.ones((t, 1), F32), g_mix_l, w_in_l, lb, tm)
    _, mk0, mk1, mv0, mv1, _, mhk, mhv, mlf, _ = _proj(prefix, prefix_mask, g_mix_l, w_in_l, lb, BLOCK)

    o_sb = _sb_attention(q, (k0, k1, v0, v1), (mk0, mk1, mv0, mv1), sb_gain[layer][None, :], b, seq)
    o_hg = _hgrn2(hq, hk, hv, lf, hg, (mhk, mhv, mlf), hg_gain[layer][None, :], b, seq)

    w_r = jnp.zeros((d, LANES), F32)
    w_r = w_r.at[:, :N_GROUPS].set(w_router_group[layer])
    w_r = w_r.at[:, N_GROUPS:N_GROUPS + N_EXPERTS].set(w_router_expert[layer])
    b_r = jnp.zeros((1, LANES), F32)
    b_r = b_r.at[0, :N_GROUPS].set(b_router_group[layer])
    b_r = b_r.at[0, N_GROUPS:N_GROUPS + N_EXPERTS].set(b_router_expert[layer])
    h2, m_tiles, route, counts = _outproj(o_sb, o_hg, xr, w_out[layer].astype(BF16), g_ffn[layer][None, :],
                                          w_r, b_r, tm)

    d1, d2, block_e, n_used, zero_start, zero_n, n_slots = _segment_layout(route, counts[0], t, d // LANES)
    xs_tiles = _dispatch(d1, d2, zero_start, zero_n, m_tiles, t, n_slots)
    ys_tiles = _experts(block_e, n_used, xs_tiles, n_slots,
                        w_expert_gate[layer], w_expert_up[layer], w_expert_down[layer])
    out = _combine(d1, d2, h2, route, ys_tiles, g_final[None, :])
    return out.reshape(b, seq, d)
```

```python
import functools

import numpy as np
import jax
import jax.numpy as jnp
from jax import lax
from jax.experimental import pallas as pl
from jax.experimental.pallas import tpu as pltpu

BLOCK = 128
N_META = 16
LEAD_PAD = BLOCK - N_META
SB_HEAD_DIM = 64
HG_HEAD_DIM = 128
SUB = 16
HG_BLOCKS = 2
Q_SUB = 4
PAIRS = 4
DEAD_LOG_TAIL = -104.0
N_GROUPS = 4
EXPERTS_PER_GROUP = 8
N_EXPERTS = N_GROUPS * EXPERTS_PER_GROUP
TOP_K = 2
MOE_BM = 512
EPS = 1e-6
LANES = 128

F32 = jnp.float32
BF16 = jnp.bfloat16


def _dot(a, b):
    return jnp.dot(a, b, preferred_element_type=F32)


def _dot_nt(a, b):
    return lax.dot_general(a, b, (((1,), (1,)), ((), ())), preferred_element_type=F32)


def _dot_tn(a, b):
    return lax.dot_general(a, b, (((0,), (0,)), ((), ())), preferred_element_type=F32)


def _rms(x):
    return x * lax.rsqrt(jnp.mean(x * x, axis=-1, keepdims=True) + EPS)


def _tiles_load(ref, n, chunks):
    return jnp.concatenate([ref[pl.ds(c, n, stride=chunks), :] for c in range(chunks)], axis=1)


def _tiles_store(ref, x, chunks):
    n = x.shape[0]
    for c in range(chunks):
        ref[pl.ds(c, n, stride=chunks), :] = x[:, c * LANES:(c + 1) * LANES]


def _proj_kernel(h_ref, mask_ref, g_ref, w_ref, lb_ref,
                 q_ref, k0_ref, k1_ref, v0_ref, v1_ref,
                 hq_ref, hk_ref, hv_ref, lf_ref, hg_ref, *, width):
    a = (_rms(h_ref[...]) * g_ref[...]).astype(BF16)

    def p(i):
        return _dot(a, w_ref[:, i * width:(i + 1) * width])

    lane = lax.broadcasted_iota(jnp.int32, (1, width), 1)
    head0 = (lane & (LANES - 1)) < SB_HEAD_DIM
    q_ref[...] = (p(0) * (SB_HEAD_DIM ** -0.5)).astype(BF16)
    k = p(1)
    k0_ref[...] = jnp.where(head0, k, 0.0).astype(BF16)
    k1_ref[...] = jnp.where(head0, 0.0, k).astype(BF16)
    v = p(2)
    v0_ref[...] = jnp.where(head0, v, 0.0).astype(BF16)
    v1_ref[...] = jnp.where(head0, 0.0, v).astype(BF16)
    hq = p(3)
    hq_ref[...] = hq * jax.nn.sigmoid(hq)
    f = p(4)
    lb = lb_ref[...]
    sig = jax.nn.sigmoid(f)
    lf_ref[...] = jnp.log(lb + (1.0 - lb) * sig)
    hk_ref[...] = mask_ref[...] * ((1.0 - lb) * jax.nn.sigmoid(-f))
    hv_ref[...] = p(5)
    g = p(6)
    hg_ref[...] = g * jax.nn.sigmoid(g)


def _proj(h, mask, g_mix, w_in, lb, tm):
    t, d = h.shape
    width = d // 2
    kern = functools.partial(_proj_kernel, width=width)
    row = lambda i: (i, 0)
    const = lambda i: (0, 0)
    outs = ([jax.ShapeDtypeStruct((t, width), BF16)] * 5
            + [jax.ShapeDtypeStruct((t, width), F32)] * 5)
    return pl.pallas_call(
        kern,
        grid=(t // tm,),
        in_specs=[pl.BlockSpec((tm, d), row), pl.BlockSpec((tm, 1), row),
                  pl.BlockSpec((1, d), const), pl.BlockSpec((d, 7 * width), const),
                  pl.BlockSpec((1, width), const)],
        out_specs=[pl.BlockSpec((tm, width), row)] * 10,
        out_shape=outs,
        compiler_params=pltpu.CompilerParams(
            dimension_semantics=("arbitrary",), vmem_limit_bytes=52 * 1024 * 1024),
        name="proj",
    )(h, mask, g_mix, w_in, lb)


def _sb_kernel(q_ref, k0_ref, k1_ref, v0_ref, v1_ref, mk0_ref, mk1_ref, mv0_ref, mv1_ref,
               tt_ref, gain_ref, o_ref, c_ref, acc_ref, zz_ref, w_ref):
    tq = Q_SUB * BLOCK
    base = pl.program_id(2) * Q_SUB
    c_ref[...] = jnp.zeros_like(c_ref)
    acc_ref[...] = jnp.zeros_like(acc_ref)

    def col_iota(rows):
        return lax.broadcasted_iota(jnp.int32, (rows, 2 * BLOCK), 1) & (BLOCK - 1)

    def lanes(p):
        return slice(p * LANES, (p + 1) * LANES)

    def k_real(p, kb):
        rows = pl.ds(pl.multiple_of(kb * BLOCK, BLOCK), BLOCK)
        return jnp.concatenate([k0_ref[rows, lanes(p)], k1_ref[rows, lanes(p)]], axis=0)

    def v_real(p, kb):
        rows = pl.ds(pl.multiple_of(kb * BLOCK, BLOCK), BLOCK)
        return jnp.concatenate([v0_ref[rows, lanes(p)], v1_ref[rows, lanes(p)]], axis=0)

    def k_real_or_prefix(p, kb):
        k_prefix = jnp.concatenate([mk0_ref[:, lanes(p)], mk1_ref[:, lanes(p)]], axis=0)
        return jnp.where(kb >= 0, k_real(p, jnp.maximum(kb, 0)), k_prefix)

    def v_prefix(p):
        return jnp.concatenate([mv0_ref[:, lanes(p)], mv1_ref[:, lanes(p)]], axis=0)

    def scores(p, k2, lo):
        zz_ref[p, lo:, :] = _dot_nt(q_ref[lo:, lanes(p)], k2)

    def values(p, v2, lo):
        acc_ref[p, lo:, :] += _dot(w_ref[p, lo:, :], v2)

    def step(mask, k_next, v_prev, lo=0, lo_next=0, lo_prev=0):
        for p in range(PAIRS):
            zz = zz_ref[p, lo:, :]
            if k_next is not None:
                scores(p, k_next(p), lo_next)
            if v_prev is not None:
                values(p, v_prev(p), lo_prev)
            neg_abs = lax.bitcast_convert_type(
                lax.bitcast_convert_type(zz, jnp.uint32) | jnp.uint32(0x80000000), F32)
            sp = jnp.maximum(zz, 0.0) + jnp.log(1.0 + jnp.exp(neg_abs))
            spm = sp if mask is None else jnp.where(mask, sp, 0.0)
            spb = spm.astype(BF16)
            tt = tt_ref[...]
            tail = jnp.concatenate(
                [_dot(spb[:, h * BLOCK:(h + 1) * BLOCK], tt) for h in range(2)], axis=1)
            c = c_ref[p, lo:, :]
            w = jnp.exp(zz - sp + tail + c)
            if mask is not None:
                w = jnp.where(mask, w, 0.0)
            w_ref[p, lo:, :] = w.astype(BF16)
            tot0 = jnp.sum(spm[:, :BLOCK], axis=-1, keepdims=True)
            tot1 = jnp.sum(spm[:, BLOCK:], axis=-1, keepdims=True)
            c_ref[p, lo:, :] = c - jnp.concatenate([jnp.broadcast_to(tot0, (tq - lo, BLOCK)),
                                                    jnp.broadcast_to(tot1, (tq - lo, BLOCK))], axis=1)

    top = base + Q_SUB - 1
    for p in range(PAIRS):
        scores(p, k_real(p, top), (Q_SUB - 1) * BLOCK)
    for jj in reversed(range(Q_SUB)):
        kb = base + jj
        lo = jj * BLOCK
        step(col_iota(tq - lo) < lax.broadcasted_iota(jnp.int32, (tq - lo, 2 * BLOCK), 0),
             (lambda p, kb=kb: k_real(p, kb - 1)) if jj > 0 else (lambda p, kb=kb: k_real_or_prefix(p, kb - 1)),
             (lambda p, kb=kb: v_real(p, kb + 1)) if jj < Q_SUB - 1 else None,
             lo=lo, lo_next=max(lo - BLOCK, 0), lo_prev=lo + BLOCK)

    def alive():
        return jnp.max(c_ref[...]) > DEAD_LOG_TAIL

    def cond(carry):
        kb, live = carry
        return jnp.logical_and(kb >= 0, live)

    def body(carry):
        kb, _ = carry
        step(None, lambda p: k_real_or_prefix(p, kb - 1), lambda p: v_real(p, kb + 1))
        return kb - 1, alive()

    kb, live = lax.while_loop(cond, body, (base - 1, alive()))
    for p in range(PAIRS):
        values(p, v_real(p, kb + 1), 0)

    @pl.when(jnp.logical_and(kb < 0, live))
    def _():
        step(col_iota(tq) >= LEAD_PAD, None, None)
        for p in range(PAIRS):
            values(p, v_prefix(p), 0)

    head0 = lax.broadcasted_iota(jnp.int32, (tq, LANES), 1) < SB_HEAD_DIM
    for p in range(PAIRS):
        o = acc_ref[p]
        o2 = o * o
        s0 = jnp.sum(jnp.where(head0, o2, 0.0), axis=-1, keepdims=True) * (1.0 / SB_HEAD_DIM)
        s1 = jnp.sum(jnp.where(head0, 0.0, o2), axis=-1, keepdims=True) * (1.0 / SB_HEAD_DIM)
        r = jnp.where(head0, lax.rsqrt(s0 + EPS), lax.rsqrt(s1 + EPS))
        o_ref[:, lanes(p)] = (o * r * gain_ref[:, lanes(p)]).astype(o_ref.dtype)


def _sb_tail_matrix():
    j = np.arange(BLOCK)[:, None]
    s = np.arange(BLOCK)[None, :]
    return jnp.asarray(np.where(j > s, -1.0, 0.0), dtype=BF16)


def _sb_attention(q, kv, kv_meta, gain, b, seq):
    t, width = q.shape
    nqt = seq // (Q_SUB * BLOCK)
    gw = PAIRS * LANES
    ngroup = width // gw
    kv3 = [a.reshape(b, seq, width) for a in kv]
    tq = Q_SUB * BLOCK
    kv_spec = pl.BlockSpec((None, seq, gw), lambda bi, hp, qt: (bi, 0, hp))
    meta_spec = pl.BlockSpec((BLOCK, gw), lambda bi, hp, qt: (0, hp))
    return pl.pallas_call(
        _sb_kernel,
        grid=(b, ngroup, nqt),
        in_specs=[pl.BlockSpec((tq, gw), lambda bi, hp, qt: (bi * nqt + qt, hp)),
                  kv_spec, kv_spec, kv_spec, kv_spec,
                  meta_spec, meta_spec, meta_spec, meta_spec,
                  pl.BlockSpec((BLOCK, BLOCK), lambda bi, hp, qt: (0, 0)),
                  pl.BlockSpec((1, gw), lambda bi, hp, qt: (0, hp))],
        out_specs=pl.BlockSpec((tq, gw), lambda bi, hp, qt: (bi * nqt + qt, hp)),
        out_shape=jax.ShapeDtypeStruct((t, width), BF16),
        scratch_shapes=[pltpu.VMEM((PAIRS, tq, 2 * BLOCK), F32), pltpu.VMEM((PAIRS, tq, LANES), F32),
                        pltpu.VMEM((PAIRS, tq, 2 * BLOCK), F32), pltpu.VMEM((PAIRS, tq, 2 * BLOCK), BF16)],
        compiler_params=pltpu.CompilerParams(
            dimension_semantics=("arbitrary", "arbitrary", "arbitrary"),
            vmem_limit_bytes=56 * 1024 * 1024),
        name="sb_attn",
    )(q, *kv3, *kv_meta, _sb_tail_matrix(), gain)


def _hg_block(hq_ref, hk_ref, hv_ref, lf_ref, lmat_ref, st_ref, a_ref, qt_ref, kh_ref, oacc_ref, f_ref, n_heads,
              with_output):
    lf = lf_ref[...]
    h1 = lf.astype(BF16)
    r1 = lf - h1.astype(F32)
    h2 = r1.astype(BF16)
    h3 = (r1 - h2.astype(F32)).astype(BF16)
    lmat = lmat_ref[...]
    cs = _dot(lmat, h1) + _dot(lmat, h2) + _dot(lmat, h3)
    a = cs[:BLOCK]
    alast = cs[BLOCK:]
    a_ref[...] = alast
    kh_ref[...] = hk_ref[...] * jnp.exp(alast - a)
    if with_output:
        f_ref[...] = jnp.exp(lf)
        qt_ref[...] = hq_ref[...] * jnp.exp(a)
    ridx = lax.broadcasted_iota(jnp.int32, (SUB, 1), 0)

    for i in range(BLOCK // SUB):
        r0 = i * SUB
        rows = pl.ds(r0, SUB)
        for hd in range(n_heads):
            cols = slice(hd * HG_HEAD_DIM, (hd + 1) * HG_HEAD_DIM)
            v_i = hv_ref[rows, cols]
            st = st_ref[hd]
            if with_output:
                o_i = _dot_nt(qt_ref[rows, cols], st)
                e = hq_ref[rows, cols]
                for s in reversed(range(SUB)):
                    if s < SUB - 1:
                        e = jnp.where(ridx > s, e * f_ref[pl.ds(r0 + s + 1, 1), cols], e)
                    sc = jnp.sum(e * hk_ref[pl.ds(r0 + s, 1), cols], axis=-1, keepdims=True)
                    o_i = o_i + jnp.where(ridx >= s, sc, 0.0) * hv_ref[pl.ds(r0 + s, 1), cols]
                oacc_ref[rows, cols] = o_i
            decay = jnp.exp(a_ref[pl.ds(r0, 1), cols])
            st_ref[hd] = st * decay + _dot_tn(v_i, kh_ref[rows, cols])


def _hg_kernel(hq_ref, hk_ref, hv_ref, lf_ref, gate_ref, mk_ref, mv_ref, mlf_ref, gain_ref, lmat_ref, o_ref,
               st_ref, a_ref, qt_ref, kh_ref, oacc_ref, f_ref, *, n_heads):
    scratch = (st_ref, a_ref, qt_ref, kh_ref, oacc_ref, f_ref)

    @pl.when(pl.program_id(1) == 0)
    def _():
        st_ref[...] = jnp.zeros_like(st_ref)
        _hg_block(None, mk_ref, mv_ref, mlf_ref, lmat_ref, *scratch, n_heads, with_output=False)

    for j in range(HG_BLOCKS):
        rows = pl.ds(j * BLOCK, BLOCK)
        _hg_block(hq_ref.at[rows, :], hk_ref.at[rows, :], hv_ref.at[rows, :], lf_ref.at[rows, :], lmat_ref,
                  *scratch, n_heads, with_output=True)
        for hd in range(n_heads):
            cols = slice(hd * HG_HEAD_DIM, (hd + 1) * HG_HEAD_DIM)
            o = _rms(oacc_ref[:, cols]) * gain_ref[:, cols] * gate_ref[rows, cols]
            o_ref[rows, cols] = o.astype(o_ref.dtype)


def _hg_cumsum_matrix():
    t = np.arange(BLOCK)[:, None]
    s = np.arange(BLOCK)[None, :]
    same = (t // SUB) == (s // SUB)
    incl = np.where(same & (s <= t), 1.0, 0.0)
    full = np.where(same, 1.0, 0.0)
    return jnp.asarray(np.concatenate([incl, full], axis=0), dtype=BF16)


def _hgrn2(hq, hk, hv, lf, gate, meta, gain, b, seq):
    t, width = hq.shape
    rows = HG_BLOCKS * BLOCK
    nc = seq // rows
    n_heads = width // HG_HEAD_DIM
    blk = pl.BlockSpec((rows, width), lambda bi, ci: (bi * nc + ci, 0))
    mblk = pl.BlockSpec((BLOCK, width), lambda bi, ci: (0, 0))
    kern = functools.partial(_hg_kernel, n_heads=n_heads)
    return pl.pallas_call(
        kern,
        grid=(b, nc),
        in_specs=[blk, blk, blk, blk, blk, mblk, mblk, mblk,
                  pl.BlockSpec((1, width), lambda bi, ci: (0, 0)),
                  pl.BlockSpec((2 * BLOCK, BLOCK), lambda bi, ci: (0, 0))],
        out_specs=blk,
        out_shape=jax.ShapeDtypeStruct((t, width), BF16),
        scratch_shapes=[pltpu.VMEM((n_heads, HG_HEAD_DIM, HG_HEAD_DIM), F32),
                        pltpu.VMEM((BLOCK, width), F32), pltpu.VMEM((BLOCK, width), F32),
                        pltpu.VMEM((BLOCK, width), F32), pltpu.VMEM((BLOCK, width), F32),
                        pltpu.VMEM((BLOCK, width), F32)],
        compiler_params=pltpu.CompilerParams(dimension_semantics=("arbitrary", "arbitrary")),
        name="hgrn2",
    )(hq, hk, hv, lf, gate, *meta, gain, _hg_cumsum_matrix())


R_E1, R_E2, R_RANK1, R_RANK2, R_G1, R_G2 = range(6)


def _outproj_kernel(osb_ref, ohg_ref, h_ref, w_ref, g_ref, wr_ref, br_ref, tri_ref,
                    h2_ref, m_ref, route_ref, cnt_ref, carry_ref, *, width):
    @pl.when(pl.program_id(0) == 0)
    def _():
        carry_ref[...] = jnp.zeros_like(carry_ref)

    h2 = h_ref[...] + _dot(osb_ref[...], w_ref[:width, :]) + _dot(ohg_ref[...], w_ref[width:, :])
    h2_ref[...] = h2
    m = _rms(h2) * g_ref[...]
    _tiles_store(m_ref, m, m.shape[1] // LANES)
    lg = _dot(m, wr_ref[...]) + br_ref[...]
    tm = lg.shape[0]
    lane = lax.broadcasted_iota(jnp.int32, (tm, LANES), 1)
    neg = jnp.float32(-1e30)

    def first_argmax(vals):
        vmax = jnp.max(vals, axis=-1, keepdims=True)
        idx = jnp.min(jnp.where(vals == vmax, lane, LANES), axis=-1, keepdims=True)
        return vmax, idx

    is_grp = lane < N_GROUPS
    gl = jnp.where(is_grp, lg, neg)
    gmax, gidx = first_argmax(gl)
    p_grp = 1.0 / jnp.sum(jnp.where(is_grp, jnp.exp(gl - gmax), 0.0), axis=-1, keepdims=True)
    lo = N_GROUPS + gidx * EXPERTS_PER_GROUP
    el = jnp.where((lane >= lo) & (lane < lo + EXPERTS_PER_GROUP), lg, neg)
    v1, i1 = first_argmax(el)
    sel1 = lane == i1
    v2, i2 = first_argmax(jnp.where(sel1, neg, el))
    sel2 = lane == i2
    dlt = jnp.exp(v2 - v1)
    g1 = p_grp / (1.0 + dlt)
    g2 = g1 * dlt

    chosen = jnp.where(sel1 | sel2, 1.0, 0.0)
    carry = carry_ref[0:1, :]
    before = _dot(tri_ref[...], chosen.astype(BF16)) + carry
    r1 = jnp.sum(jnp.where(sel1, before, 0.0), axis=-1, keepdims=True)
    r2 = jnp.sum(jnp.where(sel2, before, 0.0), axis=-1, keepdims=True)
    carry = carry + jnp.sum(chosen, axis=0, keepdims=True)
    carry_ref[0:1, :] = carry
    cnt_ref[...] = jnp.broadcast_to(carry, cnt_ref.shape)

    rec = jnp.zeros((tm, LANES), F32)
    for ln, val in ((R_E1, (i1 - N_GROUPS).astype(F32)), (R_E2, (i2 - N_GROUPS).astype(F32)),
                    (R_RANK1, r1), (R_RANK2, r2), (R_G1, g1), (R_G2, g2)):
        rec = jnp.where(lane == ln, val, rec)
    route_ref[...] = rec


def _outproj(o_sb, o_hg, h, w_out, g_ffn, w_r, b_r, tm):
    t, d = h.shape
    width = o_sb.shape[1]
    row = lambda i: (i, 0)
    const = lambda i: (0, 0)
    tri = jnp.asarray(np.tril(np.ones((tm, tm), np.float32), -1), dtype=BF16)
    kern = functools.partial(_outproj_kernel, width=width)
    return pl.pallas_call(
        kern,
        grid=(t // tm,),
        in_specs=[pl.BlockSpec((tm, width), row), pl.BlockSpec((tm, width), row),
                  pl.BlockSpec((tm, d), row), pl.BlockSpec((2 * width, d), const),
                  pl.BlockSpec((1, d), const), pl.BlockSpec((d, LANES), const),
                  pl.BlockSpec((1, LANES), const), pl.BlockSpec((tm, tm), const)],
        out_specs=[pl.BlockSpec((tm, d), row), pl.BlockSpec((tm * (d // LANES), LANES), row),
                   pl.BlockSpec((tm, LANES), row), pl.BlockSpec((8, LANES), const)],
        out_shape=[jax.ShapeDtypeStruct((t, d), F32), jax.ShapeDtypeStruct((t * (d // LANES), LANES), F32),
                   jax.ShapeDtypeStruct((t, LANES), F32), jax.ShapeDtypeStruct((8, LANES), F32)],
        scratch_shapes=[pltpu.VMEM((8, LANES), F32)],
        compiler_params=pltpu.CompilerParams(
            dimension_semantics=("arbitrary",), vmem_limit_bytes=40 * 1024 * 1024),
        name="outproj",
    )(o_sb, o_hg, h, w_out, g_ffn, w_r, b_r, tri)


def _row_copy(src, dst, sem):
    return pltpu.make_async_copy(src, dst, sem)


def _dispatch_kernel(d1_ref, d2_ref, zs_ref, zn_ref, m_hbm, xs_hbm, zero_ref, stage_ref, sems, lsems, zsem,
                     *, tile, n_tiles, ch):
    i = pl.program_id(0)
    slot = i % 3
    lines = tile * ch

    def tile_wait(s):
        _row_copy(m_hbm.at[pl.ds(0, 2 * lines), :], xs_hbm.at[pl.ds(0, 2 * lines), :], sems.at[s]).wait()

    def tile_load(step, s):
        rows = pl.ds(pl.multiple_of(step * lines, lines), lines)
        return _row_copy(m_hbm.at[rows, :], stage_ref.at[s], lsems.at[s])

    @pl.when(i == 0)
    def _():
        tile_load(0, 0).start()
        zero_ref[...] = jnp.zeros_like(zero_ref)

        def zero_block(j):
            rows = pl.ds(pl.multiple_of(zs_ref[j], MOE_BM * ch), MOE_BM * ch)
            return _row_copy(zero_ref, xs_hbm.at[rows, :], zsem)

        for j in range(2 * N_EXPERTS):
            @pl.when(zn_ref[j] > 0)
            def _():
                zero_block(j).start()
        for j in range(2 * N_EXPERTS):
            @pl.when(zn_ref[j] > 0)
            def _():
                zero_block(j).wait()

    nxt = (i + 1) % 3

    @pl.when(i >= 2)
    def _():
        tile_wait(nxt)

    @pl.when(i + 1 < n_tiles)
    def _():
        tile_load(i + 1, nxt).start()

    tile_load(i, slot).wait()
    base = i * tile
    for r in range(tile):
        src = stage_ref.at[slot, pl.ds(r * ch, ch), :]
        for prio, d_ref in enumerate((d1_ref, d2_ref)):
            dst = xs_hbm.at[pl.ds(pl.multiple_of(d_ref[base + r], ch), ch), :]
            _row_copy(src, dst, sems.at[slot]).start(priority=prio)

    @pl.when(i == n_tiles - 1)
    def _():
        if n_tiles > 1:
            tile_wait((i + 2) % 3)
        tile_wait(slot)


def _dispatch(d1, d2, zero_start, zero_n, m_tiles, n_tok, n_slots, tile=BLOCK):
    ch = m_tiles.shape[0] // n_tok
    n_tiles = n_tok // tile
    kern = functools.partial(_dispatch_kernel, tile=tile, n_tiles=n_tiles, ch=ch)
    grid_spec = pltpu.PrefetchScalarGridSpec(
        num_scalar_prefetch=4,
        grid=(n_tiles,),
        in_specs=[pl.BlockSpec(memory_space=pl.ANY)],
        out_specs=pl.BlockSpec(memory_space=pl.ANY),
        scratch_shapes=[pltpu.VMEM((MOE_BM * ch, LANES), F32), pltpu.VMEM((3, tile * ch, LANES), F32),
                        pltpu.SemaphoreType.DMA((3,)), pltpu.SemaphoreType.DMA((3,)),
                        pltpu.SemaphoreType.DMA(())],
    )
    return pl.pallas_call(
        kern,
        grid_spec=grid_spec,
        out_shape=jax.ShapeDtypeStruct((n_slots * ch, LANES), F32),
        compiler_params=pltpu.CompilerParams(dimension_semantics=("arbitrary",)),
        name="dispatch",
    )(d1, d2, zero_start, zero_n, m_tiles)


def _expert_kernel(be_ref, nu_ref, xs_ref, wg_ref, wu_ref, wd_ref, y_ref, *, ch):
    i = pl.program_id(0)

    @pl.when(i < nu_ref[0])
    def _():
        xs = _tiles_load(xs_ref, MOE_BM, ch)
        gt = _dot(xs, wg_ref[...])
        hb = gt * jax.nn.sigmoid(gt) * _dot(xs, wu_ref[...])
        _tiles_store(y_ref, _dot(hb, wd_ref[...]), ch)

    @pl.when(i >= nu_ref[0])
    def _():
        y_ref[...] = jnp.zeros_like(y_ref)


def _experts(block_e, n_used, xs_tiles, n_slots, w_gate, w_up, w_down):
    d, ff = w_gate.shape[-2:]
    ch = d // LANES
    last = lambda i, be, nu: (jnp.minimum(i, nu[0] - 1), 0)
    grid_spec = pltpu.PrefetchScalarGridSpec(
        num_scalar_prefetch=2,
        grid=(n_slots // MOE_BM,),
        in_specs=[pl.BlockSpec((MOE_BM * ch, LANES), last),
                  pl.BlockSpec((None, d, ff), lambda i, be, nu: (be[i], 0, 0)),
                  pl.BlockSpec((None, d, ff), lambda i, be, nu: (be[i], 0, 0)),
                  pl.BlockSpec((None, ff, d), lambda i, be, nu: (be[i], 0, 0))],
        out_specs=pl.BlockSpec((MOE_BM * ch, LANES), lambda i, be, nu: (i, 0)),
    )
    return pl.pallas_call(
        functools.partial(_expert_kernel, ch=ch),
        grid_spec=grid_spec,
        out_shape=jax.ShapeDtypeStruct((n_slots * ch, LANES), F32),
        compiler_params=pltpu.CompilerParams(
            dimension_semantics=("arbitrary",), vmem_limit_bytes=40 * 1024 * 1024),
        name="experts",
    )(block_e, n_used, xs_tiles, w_gate, w_up, w_down)


def _combine_kernel(d1_ref, d2_ref, h2_ref, route_ref, ys_hbm, g_ref, o_ref, ya_ref, yb_ref, sems,
                    *, tile, n_tiles, ch):
    i = pl.program_id(0)
    slot = i % 2
    lines = tile * ch

    def issue(step, s):
        base = step * tile
        for r in range(tile):
            for prio, (d_ref, y_ref) in enumerate(((d1_ref, ya_ref), (d2_ref, yb_ref))):
                src = ys_hbm.at[pl.ds(pl.multiple_of(d_ref[base + r], ch), ch), :]
                _row_copy(src, y_ref.at[s, pl.ds(r * ch, ch), :], sems.at[s]).start(priority=prio)

    @pl.when(i == 0)
    def _():
        issue(0, 0)

    @pl.when(i + 1 < n_tiles)
    def _():
        issue(i + 1, 1 - slot)

    _row_copy(ys_hbm.at[pl.ds(0, lines), :], ya_ref.at[slot], sems.at[slot]).wait()
    _row_copy(ys_hbm.at[pl.ds(0, lines), :], yb_ref.at[slot], sems.at[slot]).wait()
    rec = route_ref[...]
    g1 = rec[:, R_G1:R_G1 + 1]
    g2 = rec[:, R_G2:R_G2 + 1]
    h = h2_ref[...] + (g1 * _tiles_load(ya_ref.at[slot], tile, ch) + g2 * _tiles_load(yb_ref.at[slot], tile, ch))
    o_ref[...] = _rms(h) * g_ref[...]


def _combine(d1, d2, h2, route, ys, g_final, tile=BLOCK):
    t, d = h2.shape
    ch = d // LANES
    n_tiles = t // tile
    kern = functools.partial(_combine_kernel, tile=tile, n_tiles=n_tiles, ch=ch)
    grid_spec = pltpu.PrefetchScalarGridSpec(
        num_scalar_prefetch=2,
        grid=(n_tiles,),
        in_specs=[pl.BlockSpec((tile, d), lambda i, a, b: (i, 0)),
                  pl.BlockSpec((tile, LANES), lambda i, a, b: (i, 0)),
                  pl.BlockSpec(memory_space=pl.ANY),
                  pl.BlockSpec((1, d), lambda i, a, b: (0, 0))],
        out_specs=pl.BlockSpec((tile, d), lambda i, a, b: (i, 0)),
        scratch_shapes=[pltpu.VMEM((2, tile * ch, LANES), F32), pltpu.VMEM((2, tile * ch, LANES), F32),
                        pltpu.SemaphoreType.DMA((2,))],
    )
    return pl.pallas_call(
        kern,
        grid_spec=grid_spec,
        out_shape=jax.ShapeDtypeStruct((t, d), F32),
        compiler_params=pltpu.CompilerParams(dimension_semantics=("arbitrary",)),
        name="combine",
    )(d1, d2, h2, route, ys, g_final)


def _segment_layout(route, counts_row, n_tok, ch):
    counts = counts_row[N_GROUPS:N_GROUPS + N_EXPERTS].astype(jnp.int32)
    n_slots = n_tok * TOP_K + N_EXPERTS * MOE_BM
    n_blocks = n_slots // MOE_BM
    padded = (counts + MOE_BM - 1) // MOE_BM * MOE_BM
    padded_end = jnp.cumsum(padded)
    padded_start = padded_end - padded
    e1 = route[:, R_E1].astype(jnp.int32)
    e2 = route[:, R_E2].astype(jnp.int32)
    onehot = jnp.arange(N_EXPERTS, dtype=jnp.int32)[None, :]
    start1 = jnp.sum(jnp.where(e1[:, None] == onehot, padded_start[None, :], 0), axis=1)
    start2 = jnp.sum(jnp.where(e2[:, None] == onehot, padded_start[None, :], 0), axis=1)
    d1 = (start1 + route[:, R_RANK1].astype(jnp.int32)) * ch
    d2 = (start2 + route[:, R_RANK2].astype(jnp.int32)) * ch
    block_e = jnp.sum((jnp.arange(n_blocks, dtype=jnp.int32)[:, None] * MOE_BM >= padded_end[None, :])
                      .astype(jnp.int32), axis=1)
    block_e = jnp.minimum(block_e, N_EXPERTS - 1)
    n_used = (padded_end[-1] // MOE_BM).reshape(1)
    trailing = n_used[0] + jnp.arange(N_EXPERTS, dtype=jnp.int32)
    zero_start = jnp.concatenate([jnp.maximum(padded_end - MOE_BM, 0),
                                  jnp.minimum(trailing, n_blocks - 1) * MOE_BM]) * ch
    zero_flag = jnp.concatenate([counts, (trailing < n_blocks).astype(jnp.int32)])
    return d1, d2, block_e, n_used, zero_start, zero_flag, n_slots


def kernel(x, meta_tokens, lb_logits, g_mix, w_in, sb_gain, hg_gain, w_out, g_ffn, w_router_group,
           b_router_group, w_router_expert, b_router_expert, w_expert_gate, w_expert_up, w_expert_down,
           g_final):
    b, seq, d = x.shape
    depth = w_in.shape[0]
    assert depth == 1, "single-layer block"
    assert seq % (Q_SUB * BLOCK) == 0
    t = b * seq
    tm = 512
    layer = 0

    xr = x.reshape(t, d)
    prefix = jnp.concatenate([jnp.zeros((LEAD_PAD, d), x.dtype), meta_tokens.astype(x.dtype)], axis=0)
    prefix_mask = (jnp.arange(BLOCK) >= LEAD_PAD).astype(F32)[:, None]
    lower_bounds = jnp.cumsum(jax.nn.softmax(lb_logits.astype(F32), axis=0), axis=0)
    lb = lower_bounds[layer][None, :]
    g_mix_l = g_mix[layer][None, :]
    w_in_l = w_in[layer].astype(BF16)

    q, k0, k1, v0, v1, hq, hk, hv, lf, hg = _proj(xr, jnp.ones((t, 1), F32), g_mix_l, w_in_l, lb, tm)
    _, mk0, mk1, mv0, mv1, _, mhk, mhv, mlf, _ = _proj(prefix, prefix_mask, g_mix_l, w_in_l, lb, BLOCK)

    o_sb = _sb_attention(q, (k0, k1, v0, v1), (mk0, mk1, mv0, mv1), sb_gain[layer][None, :], b, seq)
    o_hg = _hgrn2(hq, hk, hv, lf, hg, (mhk, mhv, mlf), hg_gain[layer][None, :], b, seq)

    w_r = jnp.zeros((d, LANES), F32)
    w_r = w_r.at[:, :N_GROUPS].set(w_router_group[layer])
    w_r = w_r.at[:, N_GROUPS:N_GROUPS + N_EXPERTS].set(w_router_expert[layer])
    b_r = jnp.zeros((1, LANES), F32)
    b_r = b_r.at[0, :N_GROUPS].set(b_router_group[layer])
    b_r = b_r.at[0, N_GROUPS:N_GROUPS + N_EXPERTS].set(b_router_expert[layer])
    h2, m_tiles, route, counts = _outproj(o_sb, o_hg, xr, w_out[layer].astype(BF16), g_ffn[layer][None, :],
                                          w_r, b_r, tm)

    d1, d2, block_e, n_used, zero_start, zero_n, n_slots = _segment_layout(route, counts[0], t, d // LANES)
    xs_tiles = _dispatch(d1, d2, zero_start, zero_n, m_tiles, t, n_slots)
    ys_tiles = _experts(block_e, n_used, xs_tiles, n_slots,
                        w_expert_gate[layer], w_expert_up[layer], w_expert_down[layer])
    out = _combine(d1, d2, h2, route, ys_tiles, g_final[None, :])
    return out.reshape(b, seq, d)
```

```python
import functools

import numpy as np
import jax
import jax.numpy as jnp
from jax import lax
from jax.experimental import pallas as pl
from jax.experimental.pallas import tpu as pltpu

BLOCK = 128
N_META = 16
LEAD_PAD = BLOCK - N_META
SB_HEAD_DIM = 64
HG_HEAD_DIM = 128
SUB = 16
HG_BLOCKS = 2
Q_SUB = 4
PAIRS = 4
DEAD_LOG_TAIL = -104.0
N_GROUPS = 4
EXPERTS_PER_GROUP = 8
N_EXPERTS = N_GROUPS * EXPERTS_PER_GROUP
TOP_K = 2
MOE_BM = 512
EPS = 1e-6
LANES = 128

F32 = jnp.float32
BF16 = jnp.bfloat16


def _dot(a, b):
    return jnp.dot(a, b, preferred_element_type=F32)


def _dot_nt(a, b):
    return lax.dot_general(a, b, (((1,), (1,)), ((), ())), preferred_element_type=F32)


def _dot_tn(a, b):
    return lax.dot_general(a, b, (((0,), (0,)), ((), ())), preferred_element_type=F32)


def _rms(x):
    return x * lax.rsqrt(jnp.mean(x * x, axis=-1, keepdims=True) + EPS)


def _tiles_load(ref, n, chunks):
    return jnp.concatenate([ref[pl.ds(c, n, stride=chunks), :] for c in range(chunks)], axis=1)


def _tiles_store(ref, x, chunks):
    n = x.shape[0]
    for c in range(chunks):
        ref[pl.ds(c, n, stride=chunks), :] = x[:, c * LANES:(c + 1) * LANES]


def _proj_kernel(h_ref, mask_ref, g_ref, w_ref, lb_ref,
                 q_ref, k0_ref, k1_ref, v0_ref, v1_ref,
                 hq_ref, hk_ref, hv_ref, lf_ref, hg_ref, *, width):
    a = (_rms(h_ref[...]) * g_ref[...]).astype(BF16)

    def p(i):
        return _dot(a, w_ref[:, i * width:(i + 1) * width])

    lane = lax.broadcasted_iota(jnp.int32, (1, width), 1)
    head0 = (lane & (LANES - 1)) < SB_HEAD_DIM
    q_ref[...] = (p(0) * (SB_HEAD_DIM ** -0.5)).astype(BF16)
    k = p(1)
    k0_ref[...] = jnp.where(head0, k, 0.0).astype(BF16)
    k1_ref[...] = jnp.where(head0, 0.0, k).astype(BF16)
    v = p(2)
    v0_ref[...] = jnp.where(head0, v, 0.0).astype(BF16)
    v1_ref[...] = jnp.where(head0, 0.0, v).astype(BF16)
    hq = p(3)
    hq_ref[...] = hq * jax.nn.sigmoid(hq)
    f = p(4)
    lb = lb_ref[...]
    sig = jax.nn.sigmoid(f)
    lf_ref[...] = jnp.log(lb + (1.0 - lb) * sig)
    hk_ref[...] = mask_ref[...] * ((1.0 - lb) * jax.nn.sigmoid(-f))
    hv_ref[...] = p(5)
    g = p(6)
    hg_ref[...] = g * jax.nn.sigmoid(g)


def _proj(h, mask, g_mix, w_in, lb, tm):
    t, d = h.shape
    width = d // 2
    kern = functools.partial(_proj_kernel, width=width)
    row = lambda i: (i, 0)
    const = lambda i: (0, 0)
    outs = ([jax.ShapeDtypeStruct((t, width), BF16)] * 5
            + [jax.ShapeDtypeStruct((t, width), F32)] * 5)
    return pl.pallas_call(
        kern,
        grid=(t // tm,),
        in_specs=[pl.BlockSpec((tm, d), row), pl.BlockSpec((tm, 1), row),
                  pl.BlockSpec((1, d), const), pl.BlockSpec((d, 7 * width), const),
                  pl.BlockSpec((1, width), const)],
        out_specs=[pl.BlockSpec((tm, width), row)] * 10,
        out_shape=outs,
        compiler_params=pltpu.CompilerParams(
            dimension_semantics=("arbitrary",), vmem_limit_bytes=52 * 1024 * 1024),
        name="proj",
    )(h, mask, g_mix, w_in, lb)


def _sb_kernel(q_ref, k0_ref, k1_ref, v0_ref, v1_ref, mk0_ref, mk1_ref, mv0_ref, mv1_ref,
               tt_ref, gain_ref, o_ref, c_ref, acc_ref, zz_ref, w_ref):
    tq = Q_SUB * BLOCK
    base = pl.program_id(2) * Q_SUB
    c_ref[...] = jnp.zeros_like(c_ref)
    acc_ref[...] = jnp.zeros_like(acc_ref)

    def col_iota(rows):
        return lax.broadcasted_iota(jnp.int32, (rows, 2 * BLOCK), 1) & (BLOCK - 1)

    def lanes(p):
        return slice(p * LANES, (p + 1) * LANES)

    def k_real(p, kb):
        rows = pl.ds(pl.multiple_of(kb * BLOCK, BLOCK), BLOCK)
        return jnp.concatenate([k0_ref[rows, lanes(p)], k1_ref[rows, lanes(p)]], axis=0)

    def v_real(p, kb):
        rows = pl.ds(pl.multiple_of(kb * BLOCK, BLOCK), BLOCK)
        return jnp.concatenate([v0_ref[rows, lanes(p)], v1_ref[rows, lanes(p)]], axis=0)

    def k_real_or_prefix(p, kb):
        k_prefix = jnp.concatenate([mk0_ref[:, lanes(p)], mk1_ref[:, lanes(p)]], axis=0)
        return jnp.where(kb >= 0, k_real(p, jnp.maximum(kb, 0)), k_prefix)

    def v_prefix(p):
        return jnp.concatenate([mv0_ref[:, lanes(p)], mv1_ref[:, lanes(p)]], axis=0)

    def scores(p, k2, lo):
        zz_ref[p, lo:, :] = _dot_nt(q_ref[lo:, lanes(p)], k2)

    def values(p, v2, lo):
        acc_ref[p, lo:, :] += _dot(w_ref[p, lo:, :], v2)

    def step(mask, k_next, v_prev, lo=0, lo_next=0, lo_prev=0):
        for p in range(PAIRS):
            zz = zz_ref[p, lo:, :]
            if k_next is not None:
                scores(p, k_next(p), lo_next)
            if v_prev is not None:
                values(p, v_prev(p), lo_prev)
            neg_abs = lax.bitcast_convert_type(
                lax.bitcast_convert_type(zz, jnp.uint32) | jnp.uint32(0x80000000), F32)
            sp = jnp.maximum(zz, 0.0) + jnp.log(1.0 + jnp.exp(neg_abs))
            spm = sp if mask is None else jnp.where(mask, sp, 0.0)
            spb = spm.astype(BF16)
            tt = tt_ref[...]
            tail = jnp.concatenate(
                [_dot(spb[:, h * BLOCK:(h + 1) * BLOCK], tt) for h in range(2)], axis=1)
            c = c_ref[p, lo:, :]
            w = jnp.exp(zz - sp + tail + c)
            if mask is not None:
                w = jnp.where(mask, w, 0.0)
            w_ref[p, lo:, :] = w.astype(BF16)
            tot0 = jnp.sum(spm[:, :BLOCK], axis=-1, keepdims=True)
            tot1 = jnp.sum(spm[:, BLOCK:], axis=-1, keepdims=True)
            c_ref[p, lo:, :] = c - jnp.concatenate([jnp.broadcast_to(tot0, (tq - lo, BLOCK)),
                                                    jnp.broadcast_to(tot1, (tq - lo, BLOCK))], axis=1)

    top = base + Q_SUB - 1
    for p in range(PAIRS):
        scores(p, k_real(p, top), (Q_SUB - 1) * BLOCK)
    for jj in reversed(range(Q_SUB)):
        kb = base + jj
        lo = jj * BLOCK
        step(col_iota(tq - lo) < lax.broadcasted_iota(jnp.int32, (tq - lo, 2 * BLOCK), 0),
             (lambda p, kb=kb: k_real(p, kb - 1)) if jj > 0 else (lambda p, kb=kb: k_real_or_prefix(p, kb - 1)),
             (lambda p, kb=kb: v_real(p, kb + 1)) if jj < Q_SUB - 1 else None,
             lo=lo, lo_next=max(lo - BLOCK, 0), lo_prev=lo + BLOCK)

    def alive():
        return jnp.max(c_ref[...]) > DEAD_LOG_TAIL

    def cond(carry):
        kb, live = carry
        return jnp.logical_and(kb >= 0, live)

    def body(carry):
        kb, _ = carry
        step(None, lambda p: k_real_or_prefix(p, kb - 1), lambda p: v_real(p, kb + 1))
        return kb - 1, alive()

    kb, live = lax.while_loop(cond, body, (base - 1, alive()))
    for p in range(PAIRS):
        values(p, v_real(p, kb + 1), 0)

    @pl.when(jnp.logical_and(kb < 0, live))
    def _():
        step(col_iota(tq) >= LEAD_PAD, None, None)
        for p in range(PAIRS):
            values(p, v_prefix(p), 0)

    head0 = lax.broadcasted_iota(jnp.int32, (tq, LANES), 1) < SB_HEAD_DIM
    for p in range(PAIRS):
        o = acc_ref[p]
        o2 = o * o
        s0 = jnp.sum(jnp.where(head0, o2, 0.0), axis=-1, keepdims=True) * (1.0 / SB_HEAD_DIM)
        s1 = jnp.sum(jnp.where(head0, 0.0, o2), axis=-1, keepdims=True) * (1.0 / SB_HEAD_DIM)
        r = jnp.where(head0, lax.rsqrt(s0 + EPS), lax.rsqrt(s1 + EPS))
        o_ref[:, lanes(p)] = (o * r * gain_ref[:, lanes(p)]).astype(o_ref.dtype)


def _sb_tail_matrix():
    j = np.arange(BLOCK)[:, None]
    s = np.arange(BLOCK)[None, :]
    return jnp.asarray(np.where(j > s, -1.0, 0.0), dtype=BF16)


def _sb_attention(q, kv, kv_meta, gain, b, seq):
    t, width = q.shape
    nqt = seq // (Q_SUB * BLOCK)
    gw = PAIRS * LANES
    ngroup = width // gw
    kv3 = [a.reshape(b, seq, width) for a in kv]
    tq = Q_SUB * BLOCK
    kv_spec = pl.BlockSpec((None, seq, gw), lambda bi, hp, qt: (bi, 0, hp))
    meta_spec = pl.BlockSpec((BLOCK, gw), lambda bi, hp, qt: (0, hp))
    return pl.pallas_call(
        _sb_kernel,
        grid=(b, ngroup, nqt),
        in_specs=[pl.BlockSpec((tq, gw), lambda bi, hp, qt: (bi * nqt + qt, hp)),
                  kv_spec, kv_spec, kv_spec, kv_spec,
                  meta_spec, meta_spec, meta_spec, meta_spec,
                  pl.BlockSpec((BLOCK, BLOCK), lambda bi, hp, qt: (0, 0)),
                  pl.BlockSpec((1, gw), lambda bi, hp, qt: (0, hp))],
        out_specs=pl.BlockSpec((tq, gw), lambda bi, hp, qt: (bi * nqt + qt, hp)),
        out_shape=jax.ShapeDtypeStruct((t, width), BF16),
        scratch_shapes=[pltpu.VMEM((PAIRS, tq, 2 * BLOCK), F32), pltpu.VMEM((PAIRS, tq, LANES), F32),
                        pltpu.VMEM((PAIRS, tq, 2 * BLOCK), F32), pltpu.VMEM((PAIRS, tq, 2 * BLOCK), BF16)],
        compiler_params=pltpu.CompilerParams(
            dimension_semantics=("arbitrary", "arbitrary", "arbitrary"),
            vmem_limit_bytes=56 * 1024 * 1024),
        name="sb_attn",
    )(q, *kv3, *kv_meta, _sb_tail_matrix(), gain)


def _hg_block(hq_ref, hk_ref, hv_ref, lf_ref, lmat_ref, st_ref, a_ref, qt_ref, kh_ref, oacc_ref, f_ref, n_heads,
              with_output):
    lf = lf_ref[...]
    h1 = lf.astype(BF16)
    r1 = lf - h1.astype(F32)
    h2 = r1.astype(BF16)
    h3 = (r1 - h2.astype(F32)).astype(BF16)
    lmat = lmat_ref[...]
    cs = _dot(lmat, h1) + _dot(lmat, h2) + _dot(lmat, h3)
    a = cs[:BLOCK]
    alast = cs[BLOCK:]
    a_ref[...] = alast
    kh_ref[...] = hk_ref[...] * jnp.exp(alast - a)
    if with_output:
        f_ref[...] = jnp.exp(lf)
        qt_ref[...] = hq_ref[...] * jnp.exp(a)
    ridx = lax.broadcasted_iota(jnp.int32, (SUB, 1), 0)

    for i in range(BLOCK // SUB):
        r0 = i * SUB
        rows = pl.ds(r0, SUB)
        for hd in range(n_heads):
            cols = slice(hd * HG_HEAD_DIM, (hd + 1) * HG_HEAD_DIM)
            v_i = hv_ref[rows, cols]
            st = st_ref[hd]
            if with_output:
                o_i = _dot_nt(qt_ref[rows, cols], st)
                e = hq_ref[rows, cols]
                for s in reversed(range(SUB)):
                    if s < SUB - 1:
                        e = jnp.where(ridx > s, e * f_ref[pl.ds(r0 + s + 1, 1), cols], e)
                    sc = jnp.sum(e * hk_ref[pl.ds(r0 + s, 1), cols], axis=-1, keepdims=True)
                    o_i = o_i + jnp.where(ridx >= s, sc, 0.0) * hv_ref[pl.ds(r0 + s, 1), cols]
                oacc_ref[rows, cols] = o_i
            decay = jnp.exp(a_ref[pl.ds(r0, 1), cols])
            st_ref[hd] = st * decay + _dot_tn(v_i, kh_ref[rows, cols])


def _hg_kernel(hq_ref, hk_ref, hv_ref, lf_ref, gate_ref, mk_ref, mv_ref, mlf_ref, gain_ref, lmat_ref, o_ref,
               st_ref, a_ref, qt_ref, kh_ref, oacc_ref, f_ref, *, n_heads):
    scratch = (st_ref, a_ref, qt_ref, kh_ref, oacc_ref, f_ref)

    @pl.when(pl.program_id(1) == 0)
    def _():
        st_ref[...] = jnp.zeros_like(st_ref)
        _hg_block(None, mk_ref, mv_ref, mlf_ref, lmat_ref, *scratch, n_heads, with_output=False)

    for j in range(HG_BLOCKS):
        rows = pl.ds(j * BLOCK, BLOCK)
        _hg_block(hq_ref.at[rows, :], hk_ref.at[rows, :], hv_ref.at[rows, :], lf_ref.at[rows, :], lmat_ref,
                  *scratch, n_heads, with_output=True)
        for hd in range(n_heads):
            cols = slice(hd * HG_HEAD_DIM, (hd + 1) * HG_HEAD_DIM)
            o = _rms(oacc_ref[:, cols]) * gain_ref[:, cols] * gate_ref[rows, cols]
            o_ref[rows, cols] = o.astype(o_ref.dtype)


def _hg_cumsum_matrix():
    t = np.arange(BLOCK)[:, None]
    s = np.arange(BLOCK)[None, :]
    same = (t // SUB) == (s // SUB)
    incl = np.where(same & (s <= t), 1.0, 0.0)
    full = np.where(same, 1.0, 0.0)
    return jnp.asarray(np.concatenate([incl, full], axis=0), dtype=BF16)


def _hgrn2(hq, hk, hv, lf, gate, meta, gain, b, seq):
    t, width = hq.shape
    rows = HG_BLOCKS * BLOCK
    nc = seq // rows
    n_heads = width // HG_HEAD_DIM
    blk = pl.BlockSpec((rows, width), lambda bi, ci: (bi * nc + ci, 0))
    mblk = pl.BlockSpec((BLOCK, width), lambda bi, ci: (0, 0))
    kern = functools.partial(_hg_kernel, n_heads=n_heads)
    return pl.pallas_call(
        kern,
        grid=(b, nc),
        in_specs=[blk, blk, blk, blk, blk, mblk, mblk, mblk,
                  pl.BlockSpec((1, width), lambda bi, ci: (0, 0)),
                  pl.BlockSpec((2 * BLOCK, BLOCK), lambda bi, ci: (0, 0))],
        out_specs=blk,
        out_shape=jax.ShapeDtypeStruct((t, width), BF16),
        scratch_shapes=[pltpu.VMEM((n_heads, HG_HEAD_DIM, HG_HEAD_DIM), F32),
                        pltpu.VMEM((BLOCK, width), F32), pltpu.VMEM((BLOCK, width), F32),
                        pltpu.VMEM((BLOCK, width), F32), pltpu.VMEM((BLOCK, width), F32),
                        pltpu.VMEM((BLOCK, width), F32)],
        compiler_params=pltpu.CompilerParams(dimension_semantics=("arbitrary", "arbitrary")),
        name="hgrn2",
    )(hq, hk, hv, lf, gate, *meta, gain, _hg_cumsum_matrix())


R_E1, R_E2, R_RANK1, R_RANK2, R_G1, R_G2 = range(6)


def _outproj_kernel(osb_ref, ohg_ref, h_ref, w_ref, g_ref, wr_ref, br_ref, tri_ref,
                    h2_ref, m_ref, route_ref, cnt_ref, carry_ref, *, width):
    @pl.when(pl.program_id(0) == 0)
    def _():
        carry_ref[...] = jnp.zeros_like(carry_ref)

    h2 = h_ref[...] + _dot(osb_ref[...], w_ref[:width, :]) + _dot(ohg_ref[...], w_ref[width:, :])
    h2_ref[...] = h2
    m = _rms(h2) * g_ref[...]
    _tiles_store(m_ref, m, m.shape[1] // LANES)
    lgt = _dot_nt(wr_ref[...], m) + br_ref[...]
    tm = lgt.shape[1]
    eg = EXPERTS_PER_GROUP
    row = lax.broadcasted_iota(jnp.int32, (eg, tm), 0)
    neg = jnp.float32(-1e30)

    def first_argmax(vals):
        vmax = jnp.max(vals, axis=0, keepdims=True)
        idx = jnp.min(jnp.where(vals == vmax, row, eg), axis=0, keepdims=True)
        return vmax, idx

    def pick_group(tiles, gidx):
        out = tiles[0]
        for g in range(1, N_GROUPS):
            out = jnp.where(gidx == g, tiles[g], out)
        return out

    is_grp = row < N_GROUPS
    gl = jnp.where(is_grp, lgt[0:eg], neg)
    gmax, gidx = first_argmax(gl)
    p_grp = 1.0 / jnp.sum(jnp.where(is_grp, jnp.exp(gl - gmax), 0.0), axis=0, keepdims=True)
    el = pick_group([lgt[eg * (g + 1):eg * (g + 2)] for g in range(N_GROUPS)], gidx)
    v1, i1 = first_argmax(el)
    sel1 = row == i1
    v2, i2 = first_argmax(jnp.where(sel1, neg, el))
    sel2 = row == i2
    dlt = jnp.exp(v2 - v1)
    g1 = p_grp / (1.0 + dlt)
    g2 = g1 * dlt

    pair = jnp.where(sel1 | sel2, 1.0, 0.0)
    chosen = jnp.concatenate([jnp.where(gidx == g, pair, 0.0) for g in range(N_GROUPS)], axis=0)
    carry = carry_ref[:, 0:1]
    before = _dot(chosen.astype(BF16), tri_ref[...]) + carry
    bg = pick_group([before[eg * g:eg * (g + 1)] for g in range(N_GROUPS)], gidx)
    r1 = jnp.sum(jnp.where(sel1, bg, 0.0), axis=0, keepdims=True)
    r2 = jnp.sum(jnp.where(sel2, bg, 0.0), axis=0, keepdims=True)
    carry = carry + jnp.sum(chosen, axis=1, keepdims=True)
    carry_ref[...] = jnp.broadcast_to(carry, carry_ref.shape)
    cnt_ref[...] = jnp.broadcast_to(carry, cnt_ref.shape)

    e1 = (gidx * eg + i1).astype(F32)
    e2 = (gidx * eg + i2).astype(F32)
    rec = jnp.zeros((eg, tm), F32)
    for ln, val in ((R_E1, e1), (R_E2, e2), (R_RANK1, r1), (R_RANK2, r2), (R_G1, g1), (R_G2, g2)):
        rec = jnp.where(row == ln, val, rec)
    route_ref[...] = jnp.concatenate([rec, jnp.zeros((route_ref.shape[0] - eg, tm), F32)], axis=0)


def _outproj(o_sb, o_hg, h, w_out, g_ffn, w_r, b_r, tm):
    t, d = h.shape
    width = o_sb.shape[1]
    row = lambda i: (i, 0)
    const = lambda i: (0, 0)
    tri = jnp.asarray(np.triu(np.ones((tm, tm), np.float32), 1), dtype=BF16)
    kern = functools.partial(_outproj_kernel, width=width)
    return pl.pallas_call(
        kern,
        grid=(t // tm,),
        in_specs=[pl.BlockSpec((tm, width), row), pl.BlockSpec((tm, width), row),
                  pl.BlockSpec((tm, d), row), pl.BlockSpec((2 * width, d), const),
                  pl.BlockSpec((1, d), const), pl.BlockSpec((LANES, d), const),
                  pl.BlockSpec((LANES, 1), const), pl.BlockSpec((tm, tm), const)],
        out_specs=[pl.BlockSpec((tm, d), row), pl.BlockSpec((tm * (d // LANES), LANES), row),
                   pl.BlockSpec((LANES, tm), lambda i: (0, i)), pl.BlockSpec((N_EXPERTS, LANES), const)],
        out_shape=[jax.ShapeDtypeStruct((t, d), F32), jax.ShapeDtypeStruct((t * (d // LANES), LANES), F32),
                   jax.ShapeDtypeStruct((LANES, t), F32), jax.ShapeDtypeStruct((N_EXPERTS, LANES), F32)],
        scratch_shapes=[pltpu.VMEM((N_EXPERTS, LANES), F32)],
        compiler_params=pltpu.CompilerParams(
            dimension_semantics=("arbitrary",), vmem_limit_bytes=40 * 1024 * 1024),
        name="outproj",
    )(o_sb, o_hg, h, w_out, g_ffn, w_r, b_r, tri)


def _row_copy(src, dst, sem):
    return pltpu.make_async_copy(src, dst, sem)


def _dispatch_kernel(d1_ref, d2_ref, zs_ref, zn_ref, m_hbm, xs_hbm, zero_ref, stage_ref, sems, lsems, zsem,
                     *, tile, n_tiles, ch):
    i = pl.program_id(0)
    slot = i % 3
    lines = tile * ch

    def tile_wait(s):
        _row_copy(m_hbm.at[pl.ds(0, 2 * lines), :], xs_hbm.at[pl.ds(0, 2 * lines), :], sems.at[s]).wait()

    def tile_load(step, s):
        rows = pl.ds(pl.multiple_of(step * lines, lines), lines)
        return _row_copy(m_hbm.at[rows, :], stage_ref.at[s], lsems.at[s])

    @pl.when(i == 0)
    def _():
        tile_load(0, 0).start()
        zero_ref[...] = jnp.zeros_like(zero_ref)

        def zero_block(j):
            rows = pl.ds(pl.multiple_of(zs_ref[j], MOE_BM * ch), MOE_BM * ch)
            return _row_copy(zero_ref, xs_hbm.at[rows, :], zsem)

        for j in range(2 * N_EXPERTS):
            @pl.when(zn_ref[j] > 0)
            def _():
                zero_block(j).start()
        for j in range(2 * N_EXPERTS):
            @pl.when(zn_ref[j] > 0)
            def _():
                zero_block(j).wait()

    nxt = (i + 1) % 3

    @pl.when(i >= 2)
    def _():
        tile_wait(nxt)

    @pl.when(i + 1 < n_tiles)
    def _():
        tile_load(i + 1, nxt).start()

    tile_load(i, slot).wait()
    base = i * tile
    for r in range(tile):
        src = stage_ref.at[slot, pl.ds(r * ch, ch), :]
        for prio, d_ref in enumerate((d1_ref, d2_ref)):
            dst = xs_hbm.at[pl.ds(pl.multiple_of(d_ref[base + r], ch), ch), :]
            _row_copy(src, dst, sems.at[slot]).start(priority=prio)

    @pl.when(i == n_tiles - 1)
    def _():
        if n_tiles > 1:
            tile_wait((i + 2) % 3)
        tile_wait(slot)


def _dispatch(d1, d2, zero_start, zero_n, m_tiles, n_tok, n_slots, tile=BLOCK):
    ch = m_tiles.shape[0] // n_tok
    n_tiles = n_tok // tile
    kern = functools.partial(_dispatch_kernel, tile=tile, n_tiles=n_tiles, ch=ch)
    grid_spec = pltpu.PrefetchScalarGridSpec(
        num_scalar_prefetch=4,
        grid=(n_tiles,),
        in_specs=[pl.BlockSpec(memory_space=pl.ANY)],
        out_specs=pl.BlockSpec(memory_space=pl.ANY),
        scratch_shapes=[pltpu.VMEM((MOE_BM * ch, LANES), F32), pltpu.VMEM((3, tile * ch, LANES), F32),
                        pltpu.SemaphoreType.DMA((3,)), pltpu.SemaphoreType.DMA((3,)),
                        pltpu.SemaphoreType.DMA(())],
    )
    return pl.pallas_call(
        kern,
        grid_spec=grid_spec,
        out_shape=jax.ShapeDtypeStruct((n_slots * ch, LANES), F32),
        compiler_params=pltpu.CompilerParams(dimension_semantics=("arbitrary",)),
        name="dispatch",
    )(d1, d2, zero_start, zero_n, m_tiles)


def _expert_kernel(be_ref, nu_ref, xs_ref, wg_ref, wu_ref, wd_ref, y_ref, *, ch):
    i = pl.program_id(0)

    @pl.when(i < nu_ref[0])
    def _():
        xs = _tiles_load(xs_ref, MOE_BM, ch)
        gt = _dot(xs, wg_ref[...])
        hb = gt * jax.nn.sigmoid(gt) * _dot(xs, wu_ref[...])
        _tiles_store(y_ref, _dot(hb, wd_ref[...]), ch)

    @pl.when(i >= nu_ref[0])
    def _():
        y_ref[...] = jnp.zeros_like(y_ref)


def _experts(block_e, n_used, xs_tiles, n_slots, w_gate, w_up, w_down):
    d, ff = w_gate.shape[-2:]
    ch = d // LANES
    last = lambda i, be, nu: (jnp.minimum(i, nu[0] - 1), 0)
    grid_spec = pltpu.PrefetchScalarGridSpec(
        num_scalar_prefetch=2,
        grid=(n_slots // MOE_BM,),
        in_specs=[pl.BlockSpec((MOE_BM * ch, LANES), last),
                  pl.BlockSpec((None, d, ff), lambda i, be, nu: (be[i], 0, 0)),
                  pl.BlockSpec((None, d, ff), lambda i, be, nu: (be[i], 0, 0)),
                  pl.BlockSpec((None, ff, d), lambda i, be, nu: (be[i], 0, 0))],
        out_specs=pl.BlockSpec((MOE_BM * ch, LANES), lambda i, be, nu: (i, 0)),
    )
    return pl.pallas_call(
        functools.partial(_expert_kernel, ch=ch),
        grid_spec=grid_spec,
        out_shape=jax.ShapeDtypeStruct((n_slots * ch, LANES), F32),
        compiler_params=pltpu.CompilerParams(
            dimension_semantics=("arbitrary",), vmem_limit_bytes=40 * 1024 * 1024),
        name="experts",
    )(block_e, n_used, xs_tiles, w_gate, w_up, w_down)


def _combine_kernel(d1_ref, d2_ref, h2_ref, route_ref, ys_hbm, g_ref, o_ref, ya_ref, yb_ref, sems,
                    *, tile, n_tiles, ch):
    i = pl.program_id(0)
    slot = i % 2
    lines = tile * ch

    def issue(step, s):
        base = step * tile
        for r in range(tile):
            for prio, (d_ref, y_ref) in enumerate(((d1_ref, ya_ref), (d2_ref, yb_ref))):
                src = ys_hbm.at[pl.ds(pl.multiple_of(d_ref[base + r], ch), ch), :]
                _row_copy(src, y_ref.at[s, pl.ds(r * ch, ch), :], sems.at[s]).start(priority=prio)

    @pl.when(i == 0)
    def _():
        issue(0, 0)

    @pl.when(i + 1 < n_tiles)
    def _():
        issue(i + 1, 1 - slot)

    _row_copy(ys_hbm.at[pl.ds(0, lines), :], ya_ref.at[slot], sems.at[slot]).wait()
    _row_copy(ys_hbm.at[pl.ds(0, lines), :], yb_ref.at[slot], sems.at[slot]).wait()
    rec = route_ref[...].T
    g1 = rec[:, R_G1:R_G1 + 1]
    g2 = rec[:, R_G2:R_G2 + 1]
    h = h2_ref[...] + (g1 * _tiles_load(ya_ref.at[slot], tile, ch) + g2 * _tiles_load(yb_ref.at[slot], tile, ch))
    o_ref[...] = _rms(h) * g_ref[...]


def _combine(d1, d2, h2, route, ys, g_final, tile=BLOCK):
    t, d = h2.shape
    ch = d // LANES
    n_tiles = t // tile
    kern = functools.partial(_combine_kernel, tile=tile, n_tiles=n_tiles, ch=ch)
    grid_spec = pltpu.PrefetchScalarGridSpec(
        num_scalar_prefetch=2,
        grid=(n_tiles,),
        in_specs=[pl.BlockSpec((tile, d), lambda i, a, b: (i, 0)),
                  pl.BlockSpec((LANES, tile), lambda i, a, b: (0, i)),
                  pl.BlockSpec(memory_space=pl.ANY),
                  pl.BlockSpec((1, d), lambda i, a, b: (0, 0))],
        out_specs=pl.BlockSpec((tile, d), lambda i, a, b: (i, 0)),
        scratch_shapes=[pltpu.VMEM((2, tile * ch, LANES), F32), pltpu.VMEM((2, tile * ch, LANES), F32),
                        pltpu.SemaphoreType.DMA((2,))],
    )
    return pl.pallas_call(
        kern,
        grid_spec=grid_spec,
        out_shape=jax.ShapeDtypeStruct((t, d), F32),
        compiler_params=pltpu.CompilerParams(dimension_semantics=("arbitrary",)),
        name="combine",
    )(d1, d2, h2, route, ys, g_final)


def _segment_layout(route, counts_row, n_tok, ch):
    counts = counts_row.astype(jnp.int32)
    n_slots = n_tok * TOP_K + N_EXPERTS * MOE_BM
    n_blocks = n_slots // MOE_BM
    padded = (counts + MOE_BM - 1) // MOE_BM * MOE_BM
    padded_end = jnp.cumsum(padded)
    padded_start = padded_end - padded
    e1 = route[R_E1].astype(jnp.int32)
    e2 = route[R_E2].astype(jnp.int32)
    onehot = jnp.arange(N_EXPERTS, dtype=jnp.int32)[None, :]
    start1 = jnp.sum(jnp.where(e1[:, None] == onehot, padded_start[None, :], 0), axis=1)
    start2 = jnp.sum(jnp.where(e2[:, None] == onehot, padded_start[None, :], 0), axis=1)
    d1 = (start1 + route[R_RANK1].astype(jnp.int32)) * ch
    d2 = (start2 + route[R_RANK2].astype(jnp.int32)) * ch
    block_e = jnp.sum((jnp.arange(n_blocks, dtype=jnp.int32)[:, None] * MOE_BM >= padded_end[None, :])
                      .astype(jnp.int32), axis=1)
    block_e = jnp.minimum(block_e, N_EXPERTS - 1)
    n_used = (padded_end[-1] // MOE_BM).reshape(1)
    trailing = n_used[0] + jnp.arange(N_EXPERTS, dtype=jnp.int32)
    zero_start = jnp.concatenate([jnp.maximum(padded_end - MOE_BM, 0),
                                  jnp.minimum(trailing, n_blocks - 1) * MOE_BM]) * ch
    zero_flag = jnp.concatenate([counts, (trailing < n_blocks).astype(jnp.int32)])
    return d1, d2, block_e, n_used, zero_start, zero_flag, n_slots


def kernel(x, meta_tokens, lb_logits, g_mix, w_in, sb_gain, hg_gain, w_out, g_ffn, w_router_group,
           b_router_group, w_router_expert, b_router_expert, w_expert_gate, w_expert_up, w_expert_down,
           g_final):
    b, seq, d = x.shape
    depth = w_in.shape[0]
    assert depth == 1, "single-layer block"
    assert seq % (Q_SUB * BLOCK) == 0
    t = b * seq
    tm = 512
    layer = 0

    xr = x.reshape(t, d)
    prefix = jnp.concatenate([jnp.zeros((LEAD_PAD, d), x.dtype), meta_tokens.astype(x.dtype)], axis=0)
    prefix_mask = (jnp.arange(BLOCK) >= LEAD_PAD).astype(F32)[:, None]
    lower_bounds = jnp.cumsum(jax.nn.softmax(lb_logits.astype(F32), axis=0), axis=0)
    lb = lower_bounds[layer][None, :]
    g_mix_l = g_mix[layer][None, :]
    w_in_l = w_in[layer].astype(BF16)

    q, k0, k1, v0, v1, hq, hk, hv, lf, hg = _proj(xr, jnp.ones((t, 1), F32), g_mix_l, w_in_l, lb, tm)
    _, mk0, mk1, mv0, mv1, _, mhk, mhv, mlf, _ = _proj(prefix, prefix_mask, g_mix_l, w_in_l, lb, BLOCK)

    o_sb = _sb_attention(q, (k0, k1, v0, v1), (mk0, mk1, mv0, mv1), sb_gain[layer][None, :], b, seq)
    o_hg = _hgrn2(hq, hk, hv, lf, hg, (mhk, mhv, mlf), hg_gain[layer][None, :], b, seq)

    eo = EXPERTS_PER_GROUP
    w_r = jnp.zeros((LANES, d), F32)
    w_r = w_r.at[:N_GROUPS].set(w_router_group[layer].T)
    w_r = w_r.at[eo:eo + N_EXPERTS].set(w_router_expert[layer].T)
    b_r = jnp.zeros((LANES, 1), F32)
    b_r = b_r.at[:N_GROUPS, 0].set(b_router_group[layer])
    b_r = b_r.at[eo:eo + N_EXPERTS, 0].set(b_router_expert[layer])
    h2, m_tiles, route, counts = _outproj(o_sb, o_hg, xr, w_out[layer].astype(BF16), g_ffn[layer][None, :],
                                          w_r, b_r, tm)

    d1, d2, block_e, n_used, zero_start, zero_n, n_slots = _segment_layout(route, counts[:, 0], t, d // LANES)
    xs_tiles = _dispatch(d1, d2, zero_start, zero_n, m_tiles, t, n_slots)
    ys_tiles = _experts(block_e, n_used, xs_tiles, n_slots,
                        w_expert_gate[layer], w_expert_up[layer], w_expert_down[layer])
    out = _combine(d1, d2, h2, route, ys_tiles, g_final[None, :])
    return out.reshape(b, seq, d)
```

```python
import functools

import numpy as np
import jax
import jax.numpy as jnp
from jax import lax
from jax.experimental import pallas as pl
from jax.experimental.pallas import tpu as pltpu

BLOCK = 128
N_META = 16
LEAD_PAD = BLOCK - N_META
SB_HEAD_DIM = 64
HG_HEAD_DIM = 128
SUB = 16
HG_BLOCKS = 2
Q_SUB = 4
PAIRS = 4
DEAD_LOG_TAIL = -88.0
N_GROUPS = 4
EXPERTS_PER_GROUP = 8
N_EXPERTS = N_GROUPS * EXPERTS_PER_GROUP
TOP_K = 2
MOE_BM = 512
EPS = 1e-6
LANES = 128

F32 = jnp.float32
BF16 = jnp.bfloat16


def _dot(a, b):
    return jnp.dot(a, b, preferred_element_type=F32)


def _dot_nt(a, b):
    return lax.dot_general(a, b, (((1,), (1,)), ((), ())), preferred_element_type=F32)


def _dot_tn(a, b):
    return lax.dot_general(a, b, (((0,), (0,)), ((), ())), preferred_element_type=F32)


def _rms(x):
    return x * lax.rsqrt(jnp.mean(x * x, axis=-1, keepdims=True) + EPS)


def _tiles_load(ref, n, chunks):
    return jnp.concatenate([ref[pl.ds(c, n, stride=chunks), :] for c in range(chunks)], axis=1)


def _tiles_store(ref, x, chunks):
    n = x.shape[0]
    for c in range(chunks):
        ref[pl.ds(c, n, stride=chunks), :] = x[:, c * LANES:(c + 1) * LANES]


def _proj_kernel(h_ref, mask_ref, g_ref, w_ref, lb_ref,
                 q_ref, k0_ref, k1_ref, v0_ref, v1_ref,
                 hq_ref, hk_ref, hv_ref, lf_ref, hg_ref, *, width):
    a = (_rms(h_ref[...]) * g_ref[...]).astype(BF16)

    def p(i):
        return _dot(a, w_ref[:, i * width:(i + 1) * width])

    lane = lax.broadcasted_iota(jnp.int32, (1, width), 1)
    head0 = (lane & (LANES - 1)) < SB_HEAD_DIM
    q_ref[...] = (p(0) * (SB_HEAD_DIM ** -0.5)).astype(BF16)
    k = p(1)
    k0_ref[...] = jnp.where(head0, k, 0.0).astype(BF16)
    k1_ref[...] = jnp.where(head0, 0.0, k).astype(BF16)
    v = p(2)
    v0_ref[...] = jnp.where(head0, v, 0.0).astype(BF16)
    v1_ref[...] = jnp.where(head0, 0.0, v).astype(BF16)
    hq = p(3)
    hq_ref[...] = hq * jax.nn.sigmoid(hq)
    f = p(4)
    lb = lb_ref[...]
    sig = jax.nn.sigmoid(f)
    lf_ref[...] = jnp.log(lb + (1.0 - lb) * sig)
    hk_ref[...] = mask_ref[...] * ((1.0 - lb) * jax.nn.sigmoid(-f))
    hv_ref[...] = p(5)
    g = p(6)
    hg_ref[...] = g * jax.nn.sigmoid(g)


def _proj(h, mask, g_mix, w_in, lb, tm):
    t, d = h.shape
    width = d // 2
    kern = functools.partial(_proj_kernel, width=width)
    row = lambda i: (i, 0)
    const = lambda i: (0, 0)
    outs = ([jax.ShapeDtypeStruct((t, width), BF16)] * 5
            + [jax.ShapeDtypeStruct((t, width), F32)] * 5)
    return pl.pallas_call(
        kern,
        grid=(t // tm,),
        in_specs=[pl.BlockSpec((tm, d), row), pl.BlockSpec((tm, 1), row),
                  pl.BlockSpec((1, d), const), pl.BlockSpec((d, 7 * width), const),
                  pl.BlockSpec((1, width), const)],
        out_specs=[pl.BlockSpec((tm, width), row)] * 10,
        out_shape=outs,
        compiler_params=pltpu.CompilerParams(
            dimension_semantics=("arbitrary",), vmem_limit_bytes=52 * 1024 * 1024),
        name="proj",
    )(h, mask, g_mix, w_in, lb)


def _sb_kernel(q_ref, k0_ref, k1_ref, v0_ref, v1_ref, mk0_ref, mk1_ref, mv0_ref, mv1_ref,
               tt_ref, gain_ref, o_ref, c_ref, acc_ref, zz_ref, w_ref):
    tq = Q_SUB * BLOCK
    base = pl.program_id(2) * Q_SUB
    c_ref[...] = jnp.zeros_like(c_ref)
    acc_ref[...] = jnp.zeros_like(acc_ref)

    def col_iota(rows):
        return lax.broadcasted_iota(jnp.int32, (rows, 2 * BLOCK), 1) & (BLOCK - 1)

    def lanes(p):
        return slice(p * LANES, (p + 1) * LANES)

    def k_real(p, kb):
        rows = pl.ds(pl.multiple_of(kb * BLOCK, BLOCK), BLOCK)
        return jnp.concatenate([k0_ref[rows, lanes(p)], k1_ref[rows, lanes(p)]], axis=0)

    def v_real(p, kb):
        rows = pl.ds(pl.multiple_of(kb * BLOCK, BLOCK), BLOCK)
        return jnp.concatenate([v0_ref[rows, lanes(p)], v1_ref[rows, lanes(p)]], axis=0)

    def k_real_or_prefix(p, kb):
        k_prefix = jnp.concatenate([mk0_ref[:, lanes(p)], mk1_ref[:, lanes(p)]], axis=0)
        return jnp.where(kb >= 0, k_real(p, jnp.maximum(kb, 0)), k_prefix)

    def v_prefix(p):
        return jnp.concatenate([mv0_ref[:, lanes(p)], mv1_ref[:, lanes(p)]], axis=0)

    def scores(p, k2, lo):
        zz_ref[p, lo:, :] = _dot_nt(q_ref[lo:, lanes(p)], k2)

    def values(p, v2, lo):
        acc_ref[p, lo:, :] += _dot(w_ref[p, lo:, :], v2)

    def step(mask, k_next, v_prev, lo=0, lo_next=0, lo_prev=0):
        for p in range(PAIRS):
            zz = zz_ref[p, lo:, :]
            if k_next is not None:
                scores(p, k_next(p), lo_next)
            if v_prev is not None:
                values(p, v_prev(p), lo_prev)
            neg_abs = lax.bitcast_convert_type(
                lax.bitcast_convert_type(zz, jnp.uint32) | jnp.uint32(0x80000000), F32)
            sp = jnp.maximum(zz, 0.0) + jnp.log(1.0 + jnp.exp(neg_abs))
            spm = sp if mask is None else jnp.where(mask, sp, 0.0)
            spb = spm.astype(BF16)
            tt = tt_ref[...]
            tail = jnp.concatenate(
                [_dot(spb[:, h * BLOCK:(h + 1) * BLOCK], tt) for h in range(2)], axis=1)
            c = c_ref[p, lo:, :]
            w = jnp.exp(zz - sp + tail + c)
            if mask is not None:
                w = jnp.where(mask, w, 0.0)
            w_ref[p, lo:, :] = w.astype(BF16)
            tot0 = jnp.sum(spm[:, :BLOCK], axis=-1, keepdims=True)
            tot1 = jnp.sum(spm[:, BLOCK:], axis=-1, keepdims=True)
            c_ref[p, lo:, :] = c - jnp.concatenate([jnp.broadcast_to(tot0, (tq - lo, BLOCK)),
                                                    jnp.broadcast_to(tot1, (tq - lo, BLOCK))], axis=1)

    top = base + Q_SUB - 1
    for p in range(PAIRS):
        scores(p, k_real(p, top), (Q_SUB - 1) * BLOCK)
    for jj in reversed(range(Q_SUB)):
        kb = base + jj
        lo = jj * BLOCK
        step(col_iota(tq - lo) < lax.broadcasted_iota(jnp.int32, (tq - lo, 2 * BLOCK), 0),
             (lambda p, kb=kb: k_real(p, kb - 1)) if jj > 0 else (lambda p, kb=kb: k_real_or_prefix(p, kb - 1)),
             (lambda p, kb=kb: v_real(p, kb + 1)) if jj < Q_SUB - 1 else None,
             lo=lo, lo_next=max(lo - BLOCK, 0), lo_prev=lo + BLOCK)

    def alive():
        return jnp.max(c_ref[...]) > DEAD_LOG_TAIL

    def cond(carry):
        kb, live = carry
        return jnp.logical_and(kb >= 0, live)

    def body(carry):
        kb, _ = carry
        step(None, lambda p: k_real_or_prefix(p, kb - 1), lambda p: v_real(p, kb + 1))
        return kb - 1, alive()

    kb, live = lax.while_loop(cond, body, (base - 1, alive()))
    for p in range(PAIRS):
        values(p, v_real(p, kb + 1), 0)

    @pl.when(jnp.logical_and(kb < 0, live))
    def _():
        step(col_iota(tq) >= LEAD_PAD, None, None)
        for p in range(PAIRS):
            values(p, v_prefix(p), 0)

    head0 = lax.broadcasted_iota(jnp.int32, (tq, LANES), 1) < SB_HEAD_DIM
    for p in range(PAIRS):
        o = acc_ref[p]
        o2 = o * o
        s0 = jnp.sum(jnp.where(head0, o2, 0.0), axis=-1, keepdims=True) * (1.0 / SB_HEAD_DIM)
        s1 = jnp.sum(jnp.where(head0, 0.0, o2), axis=-1, keepdims=True) * (1.0 / SB_HEAD_DIM)
        r = jnp.where(head0, lax.rsqrt(s0 + EPS), lax.rsqrt(s1 + EPS))
        o_ref[:, lanes(p)] = (o * r * gain_ref[:, lanes(p)]).astype(o_ref.dtype)


def _sb_tail_matrix():
    j = np.arange(BLOCK)[:, None]
    s = np.arange(BLOCK)[None, :]
    return jnp.asarray(np.where(j > s, -1.0, 0.0), dtype=BF16)


def _sb_attention(q, kv, kv_meta, gain, b, seq):
    t, width = q.shape
    nqt = seq // (Q_SUB * BLOCK)
    gw = PAIRS * LANES
    ngroup = width // gw
    kv3 = [a.reshape(b, seq, width) for a in kv]
    tq = Q_SUB * BLOCK
    kv_spec = pl.BlockSpec((None, seq, gw), lambda bi, hp, qt: (bi, 0, hp))
    meta_spec = pl.BlockSpec((BLOCK, gw), lambda bi, hp, qt: (0, hp))
    return pl.pallas_call(
        _sb_kernel,
        grid=(b, ngroup, nqt),
        in_specs=[pl.BlockSpec((tq, gw), lambda bi, hp, qt: (bi * nqt + qt, hp)),
                  kv_spec, kv_spec, kv_spec, kv_spec,
                  meta_spec, meta_spec, meta_spec, meta_spec,
                  pl.BlockSpec((BLOCK, BLOCK), lambda bi, hp, qt: (0, 0)),
                  pl.BlockSpec((1, gw), lambda bi, hp, qt: (0, hp))],
        out_specs=pl.BlockSpec((tq, gw), lambda bi, hp, qt: (bi * nqt + qt, hp)),
        out_shape=jax.ShapeDtypeStruct((t, width), BF16),
        scratch_shapes=[pltpu.VMEM((PAIRS, tq, 2 * BLOCK), F32), pltpu.VMEM((PAIRS, tq, LANES), F32),
                        pltpu.VMEM((PAIRS, tq, 2 * BLOCK), F32), pltpu.VMEM((PAIRS, tq, 2 * BLOCK), BF16)],
        compiler_params=pltpu.CompilerParams(
            dimension_semantics=("arbitrary", "arbitrary", "arbitrary"),
            vmem_limit_bytes=56 * 1024 * 1024),
        name="sb_attn",
    )(q, *kv3, *kv_meta, _sb_tail_matrix(), gain)


def _hg_block(hq_ref, hk_ref, hv_ref, lf_ref, lmat_ref, st_ref, a_ref, qt_ref, kh_ref, oacc_ref, f_ref, n_heads,
              with_output):
    lf = lf_ref[...]
    h1 = lf.astype(BF16)
    r1 = lf - h1.astype(F32)
    h2 = r1.astype(BF16)
    h3 = (r1 - h2.astype(F32)).astype(BF16)
    lmat = lmat_ref[...]
    cs = _dot(lmat, h1) + _dot(lmat, h2) + _dot(lmat, h3)
    a = cs[:BLOCK]
    alast = cs[BLOCK:]
    a_ref[...] = alast
    kh_ref[...] = hk_ref[...] * jnp.exp(alast - a)
    if with_output:
        f_ref[...] = jnp.exp(lf)
        qt_ref[...] = hq_ref[...] * jnp.exp(a)
    ridx = lax.broadcasted_iota(jnp.int32, (SUB, 1), 0)

    for i in range(BLOCK // SUB):
        r0 = i * SUB
        rows = pl.ds(r0, SUB)
        for hd in range(n_heads):
            cols = slice(hd * HG_HEAD_DIM, (hd + 1) * HG_HEAD_DIM)
            v_i = hv_ref[rows, cols]
            st = st_ref[hd]
            if with_output:
                o_i = _dot_nt(qt_ref[rows, cols], st)
                e = hq_ref[rows, cols]
                for s in reversed(range(SUB)):
                    if s < SUB - 1:
                        e = jnp.where(ridx > s, e * f_ref[pl.ds(r0 + s + 1, 1), cols], e)
                    sc = jnp.sum(e * hk_ref[pl.ds(r0 + s, 1), cols], axis=-1, keepdims=True)
                    o_i = o_i + jnp.where(ridx >= s, sc, 0.0) * hv_ref[pl.ds(r0 + s, 1), cols]
                oacc_ref[rows, cols] = o_i
            decay = jnp.exp(a_ref[pl.ds(r0, 1), cols])
            st_ref[hd] = st * decay + _dot_tn(v_i, kh_ref[rows, cols])


def _hg_kernel(hq_ref, hk_ref, hv_ref, lf_ref, gate_ref, mk_ref, mv_ref, mlf_ref, gain_ref, lmat_ref, o_ref,
               st_ref, a_ref, qt_ref, kh_ref, oacc_ref, f_ref, *, n_heads):
    scratch = (st_ref, a_ref, qt_ref, kh_ref, oacc_ref, f_ref)

    @pl.when(pl.program_id(1) == 0)
    def _():
        st_ref[...] = jnp.zeros_like(st_ref)
        _hg_block(None, mk_ref, mv_ref, mlf_ref, lmat_ref, *scratch, n_heads, with_output=False)

    for j in range(HG_BLOCKS):
        rows = pl.ds(j * BLOCK, BLOCK)
        _hg_block(hq_ref.at[rows, :], hk_ref.at[rows, :], hv_ref.at[rows, :], lf_ref.at[rows, :], lmat_ref,
                  *scratch, n_heads, with_output=True)
        for hd in range(n_heads):
            cols = slice(hd * HG_HEAD_DIM, (hd + 1) * HG_HEAD_DIM)
            o = _rms(oacc_ref[:, cols]) * gain_ref[:, cols] * gate_ref[rows, cols]
            o_ref[rows, cols] = o.astype(o_ref.dtype)


def _hg_cumsum_matrix():
    t = np.arange(BLOCK)[:, None]
    s = np.arange(BLOCK)[None, :]
    same = (t // SUB) == (s // SUB)
    incl = np.where(same & (s <= t), 1.0, 0.0)
    full = np.where(same, 1.0, 0.0)
    return jnp.asarray(np.concatenate([incl, full], axis=0), dtype=BF16)


def _hgrn2(hq, hk, hv, lf, gate, meta, gain, b, seq):
    t, width = hq.shape
    rows = HG_BLOCKS * BLOCK
    nc = seq // rows
    n_heads = width // HG_HEAD_DIM
    blk = pl.BlockSpec((rows, width), lambda bi, ci: (bi * nc + ci, 0))
    mblk = pl.BlockSpec((BLOCK, width), lambda bi, ci: (0, 0))
    kern = functools.partial(_hg_kernel, n_heads=n_heads)
    return pl.pallas_call(
        kern,
        grid=(b, nc),
        in_specs=[blk, blk, blk, blk, blk, mblk, mblk, mblk,
                  pl.BlockSpec((1, width), lambda bi, ci: (0, 0)),
                  pl.BlockSpec((2 * BLOCK, BLOCK), lambda bi, ci: (0, 0))],
        out_specs=blk,
        out_shape=jax.ShapeDtypeStruct((t, width), BF16),
        scratch_shapes=[pltpu.VMEM((n_heads, HG_HEAD_DIM, HG_HEAD_DIM), F32),
                        pltpu.VMEM((BLOCK, width), F32), pltpu.VMEM((BLOCK, width), F32),
                        pltpu.VMEM((BLOCK, width), F32), pltpu.VMEM((BLOCK, width), F32),
                        pltpu.VMEM((BLOCK, width), F32)],
        compiler_params=pltpu.CompilerParams(dimension_semantics=("arbitrary", "arbitrary")),
        name="hgrn2",
    )(hq, hk, hv, lf, gate, *meta, gain, _hg_cumsum_matrix())


R_E1, R_E2, R_RANK1, R_RANK2, R_G1, R_G2 = range(6)


def _outproj_kernel(osb_ref, ohg_ref, h_ref, w_ref, g_ref, wr_ref, br_ref, tri_ref,
                    h2_ref, m_ref, route_ref, cnt_ref, carry_ref, *, width):
    @pl.when(pl.program_id(0) == 0)
    def _():
        carry_ref[...] = jnp.zeros_like(carry_ref)

    h2 = h_ref[...] + _dot(osb_ref[...], w_ref[:width, :]) + _dot(ohg_ref[...], w_ref[width:, :])
    h2_ref[...] = h2
    m = _rms(h2) * g_ref[...]
    _tiles_store(m_ref, m, m.shape[1] // LANES)
    lgt = _dot_nt(wr_ref[...], m) + br_ref[...]
    tm = lgt.shape[1]
    eg = EXPERTS_PER_GROUP
    row = lax.broadcasted_iota(jnp.int32, (eg, tm), 0)
    neg = jnp.float32(-1e30)

    def first_argmax(vals):
        vmax = jnp.max(vals, axis=0, keepdims=True)
        idx = jnp.min(jnp.where(vals == vmax, row, eg), axis=0, keepdims=True)
        return vmax, idx

    def pick_group(tiles, gidx):
        out = tiles[0]
        for g in range(1, N_GROUPS):
            out = jnp.where(gidx == g, tiles[g], out)
        return out

    is_grp = row < N_GROUPS
    gl = jnp.where(is_grp, lgt[0:eg], neg)
    gmax, gidx = first_argmax(gl)
    p_grp = 1.0 / jnp.sum(jnp.where(is_grp, jnp.exp(gl - gmax), 0.0), axis=0, keepdims=True)
    el = pick_group([lgt[eg * (g + 1):eg * (g + 2)] for g in range(N_GROUPS)], gidx)
    v1, i1 = first_argmax(el)
    sel1 = row == i1
    v2, i2 = first_argmax(jnp.where(sel1, neg, el))
    sel2 = row == i2
    dlt = jnp.exp(v2 - v1)
    g1 = p_grp / (1.0 + dlt)
    g2 = g1 * dlt

    pair = jnp.where(sel1 | sel2, 1.0, 0.0)
    chosen = jnp.concatenate([jnp.where(gidx == g, pair, 0.0) for g in range(N_GROUPS)], axis=0)
    carry = carry_ref[:, 0:1]
    before = _dot(chosen.astype(BF16), tri_ref[...]) + carry
    bg = pick_group([before[eg * g:eg * (g + 1)] for g in range(N_GROUPS)], gidx)
    r1 = jnp.sum(jnp.where(sel1, bg, 0.0), axis=0, keepdims=True)
    r2 = jnp.sum(jnp.where(sel2, bg, 0.0), axis=0, keepdims=True)
    carry = carry + jnp.sum(chosen, axis=1, keepdims=True)
    carry_ref[...] = jnp.broadcast_to(carry, carry_ref.shape)
    cnt_ref[...] = jnp.broadcast_to(carry, cnt_ref.shape)

    e1 = (gidx * eg + i1).astype(F32)
    e2 = (gidx * eg + i2).astype(F32)
    rec = jnp.zeros((eg, tm), F32)
    for ln, val in ((R_E1, e1), (R_E2, e2), (R_RANK1, r1), (R_RANK2, r2), (R_G1, g1), (R_G2, g2)):
        rec = jnp.where(row == ln, val, rec)
    route_ref[...] = jnp.concatenate([rec, jnp.zeros((route_ref.shape[0] - eg, tm), F32)], axis=0)


def _outproj(o_sb, o_hg, h, w_out, g_ffn, w_r, b_r, tm):
    t, d = h.shape
    width = o_sb.shape[1]
    row = lambda i: (i, 0)
    const = lambda i: (0, 0)
    tri = jnp.asarray(np.triu(np.ones((tm, tm), np.float32), 1), dtype=BF16)
    kern = functools.partial(_outproj_kernel, width=width)
    return pl.pallas_call(
        kern,
        grid=(t // tm,),
        in_specs=[pl.BlockSpec((tm, width), row), pl.BlockSpec((tm, width), row),
                  pl.BlockSpec((tm, d), row), pl.BlockSpec((2 * width, d), const),
                  pl.BlockSpec((1, d), const), pl.BlockSpec((LANES, d), const),
                  pl.BlockSpec((LANES, 1), const), pl.BlockSpec((tm, tm), const)],
        out_specs=[pl.BlockSpec((tm, d), row), pl.BlockSpec((tm * (d // LANES), LANES), row),
                   pl.BlockSpec((LANES, tm), lambda i: (0, i)), pl.BlockSpec((N_EXPERTS, LANES), const)],
        out_shape=[jax.ShapeDtypeStruct((t, d), F32), jax.ShapeDtypeStruct((t * (d // LANES), LANES), F32),
                   jax.ShapeDtypeStruct((LANES, t), F32), jax.ShapeDtypeStruct((N_EXPERTS, LANES), F32)],
        scratch_shapes=[pltpu.VMEM((N_EXPERTS, LANES), F32)],
        compiler_params=pltpu.CompilerParams(
            dimension_semantics=("arbitrary",), vmem_limit_bytes=40 * 1024 * 1024),
        name="outproj",
    )(o_sb, o_hg, h, w_out, g_ffn, w_r, b_r, tri)


def _row_copy(src, dst, sem):
    return pltpu.make_async_copy(src, dst, sem)


def _dispatch_kernel(d1_ref, d2_ref, zs_ref, zn_ref, m_hbm, xs_hbm, zero_ref, stage_ref, sems, lsems, zsem,
                     *, tile, n_tiles, ch):
    i = pl.program_id(0)
    slot = i % 3
    lines = tile * ch

    def tile_wait(s):
        _row_copy(m_hbm.at[pl.ds(0, 2 * lines), :], xs_hbm.at[pl.ds(0, 2 * lines), :], sems.at[s]).wait()

    def tile_load(step, s):
        rows = pl.ds(pl.multiple_of(step * lines, lines), lines)
        return _row_copy(m_hbm.at[rows, :], stage_ref.at[s], lsems.at[s])

    @pl.when(i == 0)
    def _():
        tile_load(0, 0).start()
        zero_ref[...] = jnp.zeros_like(zero_ref)

        def zero_block(j):
            rows = pl.ds(pl.multiple_of(zs_ref[j], MOE_BM * ch), MOE_BM * ch)
            return _row_copy(zero_ref, xs_hbm.at[rows, :], zsem)

        for j in range(2 * N_EXPERTS):
            @pl.when(zn_ref[j] > 0)
            def _():
                zero_block(j).start()
        for j in range(2 * N_EXPERTS):
            @pl.when(zn_ref[j] > 0)
            def _():
                zero_block(j).wait()

    nxt = (i + 1) % 3

    @pl.when(i >= 2)
    def _():
        tile_wait(nxt)

    @pl.when(i + 1 < n_tiles)
    def _():
        tile_load(i + 1, nxt).start()

    tile_load(i, slot).wait()
    base = i * tile
    for r in range(tile):
        src = stage_ref.at[slot, pl.ds(r * ch, ch), :]
        for prio, d_ref in enumerate((d1_ref, d2_ref)):
            dst = xs_hbm.at[pl.ds(pl.multiple_of(d_ref[base + r], ch), ch), :]
            _row_copy(src, dst, sems.at[slot]).start(priority=prio)

    @pl.when(i == n_tiles - 1)
    def _():
        if n_tiles > 1:
            tile_wait((i + 2) % 3)
        tile_wait(slot)


def _dispatch(d1, d2, zero_start, zero_n, m_tiles, n_tok, n_slots, tile=BLOCK):
    ch = m_tiles.shape[0] // n_tok
    n_tiles = n_tok // tile
    kern = functools.partial(_dispatch_kernel, tile=tile, n_tiles=n_tiles, ch=ch)
    grid_spec = pltpu.PrefetchScalarGridSpec(
        num_scalar_prefetch=4,
        grid=(n_tiles,),
        in_specs=[pl.BlockSpec(memory_space=pl.ANY)],
        out_specs=pl.BlockSpec(memory_space=pl.ANY),
        scratch_shapes=[pltpu.VMEM((MOE_BM * ch, LANES), F32), pltpu.VMEM((3, tile * ch, LANES), F32),
                        pltpu.SemaphoreType.DMA((3,)), pltpu.SemaphoreType.DMA((3,)),
                        pltpu.SemaphoreType.DMA(())],
    )
    return pl.pallas_call(
        kern,
        grid_spec=grid_spec,
        out_shape=jax.ShapeDtypeStruct((n_slots * ch, LANES), F32),
        compiler_params=pltpu.CompilerParams(dimension_semantics=("arbitrary",)),
        name="dispatch",
    )(d1, d2, zero_start, zero_n, m_tiles)


def _expert_kernel(be_ref, nu_ref, xs_ref, wg_ref, wu_ref, wd_ref, y_ref, *, ch):
    i = pl.program_id(0)

    @pl.when(i < nu_ref[0])
    def _():
        xs = _tiles_load(xs_ref, MOE_BM, ch)
        gt = _dot(xs, wg_ref[...])
        hb = gt * jax.nn.sigmoid(gt) * _dot(xs, wu_ref[...])
        _tiles_store(y_ref, _dot(hb, wd_ref[...]), ch)

    @pl.when(i >= nu_ref[0])
    def _():
        y_ref[...] = jnp.zeros_like(y_ref)


def _experts(block_e, n_used, xs_tiles, n_slots, w_gate, w_up, w_down):
    d, ff = w_gate.shape[-2:]
    ch = d // LANES
    last = lambda i, be, nu: (jnp.minimum(i, nu[0] - 1), 0)
    grid_spec = pltpu.PrefetchScalarGridSpec(
        num_scalar_prefetch=2,
        grid=(n_slots // MOE_BM,),
        in_specs=[pl.BlockSpec((MOE_BM * ch, LANES), last),
                  pl.BlockSpec((None, d, ff), lambda i, be, nu: (be[i], 0, 0)),
                  pl.BlockSpec((None, d, ff), lambda i, be, nu: (be[i], 0, 0)),
                  pl.BlockSpec((None, ff, d), lambda i, be, nu: (be[i], 0, 0))],
        out_specs=pl.BlockSpec((MOE_BM * ch, LANES), lambda i, be, nu: (i, 0)),
    )
    return pl.pallas_call(
        functools.partial(_expert_kernel, ch=ch),
        grid_spec=grid_spec,
        out_shape=jax.ShapeDtypeStruct((n_slots * ch, LANES), F32),
        compiler_params=pltpu.CompilerParams(
            dimension_semantics=("arbitrary",), vmem_limit_bytes=40 * 1024 * 1024),
        name="experts",
    )(block_e, n_used, xs_tiles, w_gate, w_up, w_down)


def _combine_kernel(d1_ref, d2_ref, h2_ref, route_ref, ys_hbm, g_ref, o_ref, ya_ref, yb_ref, sems,
                    *, tile, n_tiles, ch):
    i = pl.program_id(0)
    slot = i % 2
    lines = tile * ch

    def issue(step, s):
        base = step * tile
        for r in range(tile):
            for prio, (d_ref, y_ref) in enumerate(((d1_ref, ya_ref), (d2_ref, yb_ref))):
                src = ys_hbm.at[pl.ds(pl.multiple_of(d_ref[base + r], ch), ch), :]
                _row_copy(src, y_ref.at[s, pl.ds(r * ch, ch), :], sems.at[s]).start(priority=prio)

    @pl.when(i == 0)
    def _():
        issue(0, 0)

    @pl.when(i + 1 < n_tiles)
    def _():
        issue(i + 1, 1 - slot)

    _row_copy(ys_hbm.at[pl.ds(0, lines), :], ya_ref.at[slot], sems.at[slot]).wait()
    _row_copy(ys_hbm.at[pl.ds(0, lines), :], yb_ref.at[slot], sems.at[slot]).wait()
    rec = route_ref[...].T
    g1 = rec[:, R_G1:R_G1 + 1]
    g2 = rec[:, R_G2:R_G2 + 1]
    h = h2_ref[...] + (g1 * _tiles_load(ya_ref.at[slot], tile, ch) + g2 * _tiles_load(yb_ref.at[slot], tile, ch))
    o_ref[...] = _rms(h) * g_ref[...]


def _combine(d1, d2, h2, route, ys, g_final, tile=BLOCK):
    t, d = h2.shape
    ch = d // LANES
    n_tiles = t // tile
    kern = functools.partial(_combine_kernel, tile=tile, n_tiles=n_tiles, ch=ch)
    grid_spec = pltpu.PrefetchScalarGridSpec(
        num_scalar_prefetch=2,
        grid=(n_tiles,),
        in_specs=[pl.BlockSpec((tile, d), lambda i, a, b: (i, 0)),
                  pl.BlockSpec((LANES, tile), lambda i, a, b: (0, i)),
                  pl.BlockSpec(memory_space=pl.ANY),
                  pl.BlockSpec((1, d), lambda i, a, b: (0, 0))],
        out_specs=pl.BlockSpec((tile, d), lambda i, a, b: (i, 0)),
        scratch_shapes=[pltpu.VMEM((2, tile * ch, LANES), F32), pltpu.VMEM((2, tile * ch, LANES), F32),
                        pltpu.SemaphoreType.DMA((2,))],
    )
    return pl.pallas_call(
        kern,
        grid_spec=grid_spec,
        out_shape=jax.ShapeDtypeStruct((t, d), F32),
        compiler_params=pltpu.CompilerParams(dimension_semantics=("arbitrary",)),
        name="combine",
    )(d1, d2, h2, route, ys, g_final)


def _segment_layout(route, counts_row, n_tok, ch):
    counts = counts_row.astype(jnp.int32)
    n_slots = n_tok * TOP_K + N_EXPERTS * MOE_BM
    n_blocks = n_slots // MOE_BM
    padded = (counts + MOE_BM - 1) // MOE_BM * MOE_BM
    padded_end = jnp.cumsum(padded)
    padded_start = padded_end - padded
    e1 = route[R_E1].astype(jnp.int32)
    e2 = route[R_E2].astype(jnp.int32)
    onehot = jnp.arange(N_EXPERTS, dtype=jnp.int32)[None, :]
    start1 = jnp.sum(jnp.where(e1[:, None] == onehot, padded_start[None, :], 0), axis=1)
    start2 = jnp.sum(jnp.where(e2[:, None] == onehot, padded_start[None, :], 0), axis=1)
    d1 = (start1 + route[R_RANK1].astype(jnp.int32)) * ch
    d2 = (start2 + route[R_RANK2].astype(jnp.int32)) * ch
    block_e = jnp.sum((jnp.arange(n_blocks, dtype=jnp.int32)[:, None] * MOE_BM >= padded_end[None, :])
                      .astype(jnp.int32), axis=1)
    block_e = jnp.minimum(block_e, N_EXPERTS - 1)
    n_used = (padded_end[-1] // MOE_BM).reshape(1)
    trailing = n_used[0] + jnp.arange(N_EXPERTS, dtype=jnp.int32)
    zero_start = jnp.concatenate([jnp.maximum(padded_end - MOE_BM, 0),
                                  jnp.minimum(trailing, n_blocks - 1) * MOE_BM]) * ch
    zero_flag = jnp.concatenate([counts, (trailing < n_blocks).astype(jnp.int32)])
    return d1, d2, block_e, n_used, zero_start, zero_flag, n_slots


def kernel(x, meta_tokens, lb_logits, g_mix, w_in, sb_gain, hg_gain, w_out, g_ffn, w_router_group,
           b_router_group, w_router_expert, b_router_expert, w_expert_gate, w_expert_up, w_expert_down,
           g_final):
    b, seq, d = x.shape
    depth = w_in.shape[0]
    assert depth == 1, "single-layer block"
    assert seq % (Q_SUB * BLOCK) == 0
    t = b * seq
    tm = 512
    layer = 0

    xr = x.reshape(t, d)
    prefix = jnp.concatenate([jnp.zeros((LEAD_PAD, d), x.dtype), meta_tokens.astype(x.dtype)], axis=0)
    prefix_mask = (jnp.arange(BLOCK) >= LEAD_PAD).astype(F32)[:, None]
    lower_bounds = jnp.cumsum(jax.nn.softmax(lb_logits.astype(F32), axis=0), axis=0)
    lb = lower_bounds[layer][None, :]
    g_mix_l = g_mix[layer][None, :]
    w_in_l = w_in[layer].astype(BF16)

    q, k0, k1, v0, v1, hq, hk, hv, lf, hg = _proj(xr, jnp.ones((t, 1), F32), g_mix_l, w_in_l, lb, tm)
    _, mk0, mk1, mv0, mv1, _, mhk, mhv, mlf, _ = _proj(prefix, prefix_mask, g_mix_l, w_in_l, lb, BLOCK)

    o_sb = _sb_attention(q, (k0, k1, v0, v1), (mk0, mk1, mv0, mv1), sb_gain[layer][None, :], b, seq)
    o_hg = _hgrn2(hq, hk, hv, lf, hg, (mhk, mhv, mlf), hg_gain[layer][None, :], b, seq)

    eo = EXPERTS_PER_GROUP
    w_r = jnp.zeros((LANES, d), F32)
    w_r = w_r.at[:N_GROUPS].set(w_router_group[layer].T)
    w_r = w_r.at[eo:eo + N_EXPERTS].set(w_router_expert[layer].T)
    b_r = jnp.zeros((LANES, 1), F32)
    b_r = b_r.at[:N_GROUPS, 0].set(b_router_group[layer])
    b_r = b_r.at[eo:eo + N_EXPERTS, 0].set(b_router_expert[layer])
    h2, m_tiles, route, counts = _outproj(o_sb, o_hg, xr, w_out[layer].astype(BF16), g_ffn[layer][None, :],
                                          w_r, b_r, tm)

    d1, d2, block_e, n_used, zero_start, zero_n, n_slots = _segment_layout(route, counts[:, 0], t, d // LANES)
    xs_tiles = _dispatch(d1, d2, zero_start, zero_n, m_tiles, t, n_slots)
    ys_tiles = _experts(block_e, n_used, xs_tiles, n_slots,
                        w_expert_gate[layer], w_expert_up[layer], w_expert_down[layer])
    out = _combine(d1, d2, h2, route, ys_tiles, g_final[None, :])
    return out.reshape(b, seq, d)
```

```python
import functools

import numpy as np
import jax
import jax.numpy as jnp
from jax import lax
from jax.experimental import pallas as pl
from jax.experimental.pallas import tpu as pltpu

BLOCK = 128
N_META = 16
LEAD_PAD = BLOCK - N_META
SB_HEAD_DIM = 64
HG_HEAD_DIM = 128
SUB = 16
HG_BLOCKS = 2
Q_SUB = 4
PAIRS = 4
DEAD_LOG_TAIL = -88.0
N_GROUPS = 4
EXPERTS_PER_GROUP = 8
N_EXPERTS = N_GROUPS * EXPERTS_PER_GROUP
TOP_K = 2
MOE_BM = 512
EPS = 1e-6
LANES = 128

F32 = jnp.float32
BF16 = jnp.bfloat16


def _dot(a, b):
    return jnp.dot(a, b, preferred_element_type=F32)


def _dot_nt(a, b):
    return lax.dot_general(a, b, (((1,), (1,)), ((), ())), preferred_element_type=F32)


def _dot_tn(a, b):
    return lax.dot_general(a, b, (((0,), (0,)), ((), ())), preferred_element_type=F32)


def _rms(x):
    return x * lax.rsqrt(jnp.mean(x * x, axis=-1, keepdims=True) + EPS)


def _tiles_load(ref, n, chunks):
    return jnp.concatenate([ref[pl.ds(c, n, stride=chunks), :] for c in range(chunks)], axis=1)


def _tiles_store(ref, x, chunks):
    n = x.shape[0]
    for c in range(chunks):
        ref[pl.ds(c, n, stride=chunks), :] = x[:, c * LANES:(c + 1) * LANES]


def _proj_kernel(h_ref, mask_ref, g_ref, w_ref, lb_ref,
                 q_ref, k0_ref, k1_ref, v0_ref, v1_ref,
                 hq_ref, hk_ref, hv_ref, lf_ref, hg_ref, *, width):
    a = (_rms(h_ref[...]) * g_ref[...]).astype(BF16)

    def p(i):
        return _dot(a, w_ref[:, i * width:(i + 1) * width])

    lane = lax.broadcasted_iota(jnp.int32, (1, width), 1)
    head0 = (lane & (LANES - 1)) < SB_HEAD_DIM
    q_ref[...] = (p(0) * (SB_HEAD_DIM ** -0.5)).astype(BF16)
    k = p(1)
    k0_ref[...] = jnp.where(head0, k, 0.0).astype(BF16)
    k1_ref[...] = jnp.where(head0, 0.0, k).astype(BF16)
    v = p(2)
    v0_ref[...] = jnp.where(head0, v, 0.0).astype(BF16)
    v1_ref[...] = jnp.where(head0, 0.0, v).astype(BF16)
    hq = p(3)
    hq_ref[...] = hq * jax.nn.sigmoid(hq)
    f = p(4)
    lb = lb_ref[...]
    sig = jax.nn.sigmoid(f)
    lf_ref[...] = jnp.log(lb + (1.0 - lb) * sig)
    hk_ref[...] = mask_ref[...] * ((1.0 - lb) * jax.nn.sigmoid(-f))
    hv_ref[...] = p(5)
    g = p(6)
    hg_ref[...] = g * jax.nn.sigmoid(g)


def _proj(h, mask, g_mix, w_in, lb, tm):
    t, d = h.shape
    width = d // 2
    kern = functools.partial(_proj_kernel, width=width)
    row = lambda i: (i, 0)
    const = lambda i: (0, 0)
    outs = ([jax.ShapeDtypeStruct((t, width), BF16)] * 5
            + [jax.ShapeDtypeStruct((t, width), F32)] * 5)
    return pl.pallas_call(
        kern,
        grid=(t // tm,),
        in_specs=[pl.BlockSpec((tm, d), row), pl.BlockSpec((tm, 1), row),
                  pl.BlockSpec((1, d), const), pl.BlockSpec((d, 7 * width), const),
                  pl.BlockSpec((1, width), const)],
        out_specs=[pl.BlockSpec((tm, width), row)] * 10,
        out_shape=outs,
        compiler_params=pltpu.CompilerParams(
            dimension_semantics=("arbitrary",), vmem_limit_bytes=52 * 1024 * 1024),
        name="proj",
    )(h, mask, g_mix, w_in, lb)


def _sb_kernel(q_ref, k0_ref, k1_ref, v0_ref, v1_ref, mk0_ref, mk1_ref, mv0_ref, mv1_ref,
               tt_ref, gain_ref, o_ref, c_ref, acc_ref, zz_ref, w_ref):
    tq = Q_SUB * BLOCK
    base = pl.program_id(2) * Q_SUB
    c_ref[...] = jnp.zeros_like(c_ref)
    acc_ref[...] = jnp.zeros_like(acc_ref)

    def col_iota(rows):
        return lax.broadcasted_iota(jnp.int32, (rows, 2 * BLOCK), 1) & (BLOCK - 1)

    def lanes(p):
        return slice(p * LANES, (p + 1) * LANES)

    def k_real(p, kb):
        rows = pl.ds(pl.multiple_of(kb * BLOCK, BLOCK), BLOCK)
        return jnp.concatenate([k0_ref[rows, lanes(p)], k1_ref[rows, lanes(p)]], axis=0)

    def v_real(p, kb):
        rows = pl.ds(pl.multiple_of(kb * BLOCK, BLOCK), BLOCK)
        return jnp.concatenate([v0_ref[rows, lanes(p)], v1_ref[rows, lanes(p)]], axis=0)

    def k_real_or_prefix(p, kb):
        k_prefix = jnp.concatenate([mk0_ref[:, lanes(p)], mk1_ref[:, lanes(p)]], axis=0)
        return jnp.where(kb >= 0, k_real(p, jnp.maximum(kb, 0)), k_prefix)

    def v_prefix(p):
        return jnp.concatenate([mv0_ref[:, lanes(p)], mv1_ref[:, lanes(p)]], axis=0)

    def scores(p, k2, lo, hi=None):
        zz_ref[p, lo:hi, :] = _dot_nt(q_ref[lo:hi, lanes(p)], k2)

    def values(p, v2, lo):
        acc_ref[p, lo:, :] += _dot(w_ref[p, lo:, :], v2)

    def step(mask, k_next, v_prev, lo=0, lo_next=0, lo_prev=0, hi=None):
        n_rows = (tq if hi is None else hi) - lo
        for p in range(PAIRS):
            zz = zz_ref[p, lo:hi, :]
            if k_next is not None:
                scores(p, k_next(p), lo_next, hi)
            if v_prev is not None:
                values(p, v_prev(p), lo_prev)
            neg_abs = lax.bitcast_convert_type(
                lax.bitcast_convert_type(zz, jnp.uint32) | jnp.uint32(0x80000000), F32)
            sp = jnp.maximum(zz, 0.0) + jnp.log(1.0 + jnp.exp(neg_abs))
            spm = sp if mask is None else jnp.where(mask, sp, 0.0)
            spb = spm.astype(BF16)
            tt = tt_ref[...]
            tail = jnp.concatenate(
                [_dot(spb[:, h * BLOCK:(h + 1) * BLOCK], tt) for h in range(2)], axis=1)
            c = c_ref[p, lo:hi, :]
            w = jnp.exp(zz - sp + tail + c)
            if mask is not None:
                w = jnp.where(mask, w, 0.0)
            w_ref[p, lo:hi, :] = w.astype(BF16)
            if hi is not None:
                w_ref[p, hi:, :] = jnp.zeros((tq - hi, 2 * BLOCK), BF16)
            tot0 = jnp.sum(spm[:, :BLOCK], axis=-1, keepdims=True)
            tot1 = jnp.sum(spm[:, BLOCK:], axis=-1, keepdims=True)
            c_ref[p, lo:hi, :] = c - jnp.concatenate([jnp.broadcast_to(tot0, (n_rows, BLOCK)),
                                                      jnp.broadcast_to(tot1, (n_rows, BLOCK))], axis=1)

    top = base + Q_SUB - 1
    for p in range(PAIRS):
        scores(p, k_real(p, top), (Q_SUB - 1) * BLOCK)
    for jj in reversed(range(Q_SUB)):
        kb = base + jj
        lo = jj * BLOCK
        step(col_iota(tq - lo) < lax.broadcasted_iota(jnp.int32, (tq - lo, 2 * BLOCK), 0),
             (lambda p, kb=kb: k_real(p, kb - 1)) if jj > 0 else (lambda p, kb=kb: k_real_or_prefix(p, kb - 1)),
             (lambda p, kb=kb: v_real(p, kb + 1)) if jj < Q_SUB - 1 else None,
             lo=lo, lo_next=max(lo - BLOCK, 0), lo_prev=lo + BLOCK)

    half = tq // 2

    def alive():
        lower = jnp.max(c_ref[:, :half, :]) > DEAD_LOG_TAIL
        upper = jnp.max(c_ref[:, half:, :]) > DEAD_LOG_TAIL
        return jnp.logical_or(lower, upper), upper

    def cond(carry):
        kb, live, _ = carry
        return jnp.logical_and(kb >= 0, live)

    def body(carry):
        kb, _, upper = carry
        k_next = lambda p: k_real_or_prefix(p, kb - 1)
        v_prev = lambda p: v_real(p, kb + 1)

        @pl.when(upper)
        def _():
            step(None, k_next, v_prev)

        @pl.when(jnp.logical_not(upper))
        def _():
            step(None, k_next, v_prev, hi=half)

        return (kb - 1, *alive())

    kb, live, _ = lax.while_loop(cond, body, (base - 1, *alive()))
    for p in range(PAIRS):
        values(p, v_real(p, kb + 1), 0)

    @pl.when(jnp.logical_and(kb < 0, live))
    def _():
        step(col_iota(tq) >= LEAD_PAD, None, None)
        for p in range(PAIRS):
            values(p, v_prefix(p), 0)

    head0 = lax.broadcasted_iota(jnp.int32, (tq, LANES), 1) < SB_HEAD_DIM
    for p in range(PAIRS):
        o = acc_ref[p]
        o2 = o * o
        s0 = jnp.sum(jnp.where(head0, o2, 0.0), axis=-1, keepdims=True) * (1.0 / SB_HEAD_DIM)
        s1 = jnp.sum(jnp.where(head0, 0.0, o2), axis=-1, keepdims=True) * (1.0 / SB_HEAD_DIM)
        r = jnp.where(head0, lax.rsqrt(s0 + EPS), lax.rsqrt(s1 + EPS))
        o_ref[:, lanes(p)] = (o * r * gain_ref[:, lanes(p)]).astype(o_ref.dtype)


def _sb_tail_matrix():
    j = np.arange(BLOCK)[:, None]
    s = np.arange(BLOCK)[None, :]
    return jnp.asarray(np.where(j > s, -1.0, 0.0), dtype=BF16)


def _sb_attention(q, kv, kv_meta, gain, b, seq):
    t, width = q.shape
    nqt = seq // (Q_SUB * BLOCK)
    gw = PAIRS * LANES
    ngroup = width // gw
    kv3 = [a.reshape(b, seq, width) for a in kv]
    tq = Q_SUB * BLOCK
    kv_spec = pl.BlockSpec((None, seq, gw), lambda bi, hp, qt: (bi, 0, hp))
    meta_spec = pl.BlockSpec((BLOCK, gw), lambda bi, hp, qt: (0, hp))
    return pl.pallas_call(
        _sb_kernel,
        grid=(b, ngroup, nqt),
        in_specs=[pl.BlockSpec((tq, gw), lambda bi, hp, qt: (bi * nqt + qt, hp)),
                  kv_spec, kv_spec, kv_spec, kv_spec,
                  meta_spec, meta_spec, meta_spec, meta_spec,
                  pl.BlockSpec((BLOCK, BLOCK), lambda bi, hp, qt: (0, 0)),
                  pl.BlockSpec((1, gw), lambda bi, hp, qt: (0, hp))],
        out_specs=pl.BlockSpec((tq, gw), lambda bi, hp, qt: (bi * nqt + qt, hp)),
        out_shape=jax.ShapeDtypeStruct((t, width), BF16),
        scratch_shapes=[pltpu.VMEM((PAIRS, tq, 2 * BLOCK), F32), pltpu.VMEM((PAIRS, tq, LANES), F32),
                        pltpu.VMEM((PAIRS, tq, 2 * BLOCK), F32), pltpu.VMEM((PAIRS, tq, 2 * BLOCK), BF16)],
        compiler_params=pltpu.CompilerParams(
            dimension_semantics=("arbitrary", "arbitrary", "arbitrary"),
            vmem_limit_bytes=56 * 1024 * 1024),
        name="sb_attn",
    )(q, *kv3, *kv_meta, _sb_tail_matrix(), gain)


def _hg_block(hq_ref, hk_ref, hv_ref, lf_ref, lmat_ref, st_ref, a_ref, qt_ref, kh_ref, oacc_ref, f_ref, n_heads,
              with_output):
    lf = lf_ref[...]
    h1 = lf.astype(BF16)
    r1 = lf - h1.astype(F32)
    h2 = r1.astype(BF16)
    h3 = (r1 - h2.astype(F32)).astype(BF16)
    lmat = lmat_ref[...]
    cs = _dot(lmat, h1) + _dot(lmat, h2) + _dot(lmat, h3)
    a = cs[:BLOCK]
    alast = cs[BLOCK:]
    a_ref[...] = alast
    kh_ref[...] = hk_ref[...] * jnp.exp(alast - a)
    if with_output:
        f_ref[...] = jnp.exp(lf)
        qt_ref[...] = hq_ref[...] * jnp.exp(a)
    ridx = lax.broadcasted_iota(jnp.int32, (SUB, 1), 0)

    for i in range(BLOCK // SUB):
        r0 = i * SUB
        rows = pl.ds(r0, SUB)
        for hd in range(n_heads):
            cols = slice(hd * HG_HEAD_DIM, (hd + 1) * HG_HEAD_DIM)
            v_i = hv_ref[rows, cols]
            st = st_ref[hd]
            if with_output:
                o_i = _dot_nt(qt_ref[rows, cols], st)
                e = hq_ref[rows, cols]
                for s in reversed(range(SUB)):
                    if s < SUB - 1:
                        e = jnp.where(ridx > s, e * f_ref[pl.ds(r0 + s + 1, 1), cols], e)
                    sc = jnp.sum(e * hk_ref[pl.ds(r0 + s, 1), cols], axis=-1, keepdims=True)
                    o_i = o_i + jnp.where(ridx >= s, sc, 0.0) * hv_ref[pl.ds(r0 + s, 1), cols]
                oacc_ref[rows, cols] = o_i
            decay = jnp.exp(a_ref[pl.ds(r0, 1), cols])
            st_ref[hd] = st * decay + _dot_tn(v_i, kh_ref[rows, cols])


def _hg_kernel(hq_ref, hk_ref, hv_ref, lf_ref, gate_ref, mk_ref, mv_ref, mlf_ref, gain_ref, lmat_ref, o_ref,
               st_ref, a_ref, qt_ref, kh_ref, oacc_ref, f_ref, *, n_heads):
    scratch = (st_ref, a_ref, qt_ref, kh_ref, oacc_ref, f_ref)

    @pl.when(pl.program_id(1) == 0)
    def _():
        st_ref[...] = jnp.zeros_like(st_ref)
        _hg_block(None, mk_ref, mv_ref, mlf_ref, lmat_ref, *scratch, n_heads, with_output=False)

    for j in range(HG_BLOCKS):
        rows = pl.ds(j * BLOCK, BLOCK)
        _hg_block(hq_ref.at[rows, :], hk_ref.at[rows, :], hv_ref.at[rows, :], lf_ref.at[rows, :], lmat_ref,
                  *scratch, n_heads, with_output=True)
        for hd in range(n_heads):
            cols = slice(hd * HG_HEAD_DIM, (hd + 1) * HG_HEAD_DIM)
            o = _rms(oacc_ref[:, cols]) * gain_ref[:, cols] * gate_ref[rows, cols]
            o_ref[rows, cols] = o.astype(o_ref.dtype)


def _hg_cumsum_matrix():
    t = np.arange(BLOCK)[:, None]
    s = np.arange(BLOCK)[None, :]
    same = (t // SUB) == (s // SUB)
    incl = np.where(same & (s <= t), 1.0, 0.0)
    full = np.where(same, 1.0, 0.0)
    return jnp.asarray(np.concatenate([incl, full], axis=0), dtype=BF16)


def _hgrn2(hq, hk, hv, lf, gate, meta, gain, b, seq):
    t, width = hq.shape
    rows = HG_BLOCKS * BLOCK
    nc = seq // rows
    n_heads = width // HG_HEAD_DIM
    blk = pl.BlockSpec((rows, width), lambda bi, ci: (bi * nc + ci, 0))
    mblk = pl.BlockSpec((BLOCK, width), lambda bi, ci: (0, 0))
    kern = functools.partial(_hg_kernel, n_heads=n_heads)
    return pl.pallas_call(
        kern,
        grid=(b, nc),
        in_specs=[blk, blk, blk, blk, blk, mblk, mblk, mblk,
                  pl.BlockSpec((1, width), lambda bi, ci: (0, 0)),
                  pl.BlockSpec((2 * BLOCK, BLOCK), lambda bi, ci: (0, 0))],
        out_specs=blk,
        out_shape=jax.ShapeDtypeStruct((t, width), BF16),
        scratch_shapes=[pltpu.VMEM((n_heads, HG_HEAD_DIM, HG_HEAD_DIM), F32),
                        pltpu.VMEM((BLOCK, width), F32), pltpu.VMEM((BLOCK, width), F32),
                        pltpu.VMEM((BLOCK, width), F32), pltpu.VMEM((BLOCK, width), F32),
                        pltpu.VMEM((BLOCK, width), F32)],
        compiler_params=pltpu.CompilerParams(dimension_semantics=("arbitrary", "arbitrary")),
        name="hgrn2",
    )(hq, hk, hv, lf, gate, *meta, gain, _hg_cumsum_matrix())


R_E1, R_E2, R_RANK1, R_RANK2, R_G1, R_G2 = range(6)


def _outproj_kernel(osb_ref, ohg_ref, h_ref, w_ref, g_ref, wr_ref, br_ref, tri_ref,
                    h2_ref, m_ref, route_ref, cnt_ref, carry_ref, *, width):
    @pl.when(pl.program_id(0) == 0)
    def _():
        carry_ref[...] = jnp.zeros_like(carry_ref)

    h2 = h_ref[...] + _dot(osb_ref[...], w_ref[:width, :]) + _dot(ohg_ref[...], w_ref[width:, :])
    h2_ref[...] = h2
    m = _rms(h2) * g_ref[...]
    _tiles_store(m_ref, m, m.shape[1] // LANES)
    lgt = _dot_nt(wr_ref[...], m) + br_ref[...]
    tm = lgt.shape[1]
    eg = EXPERTS_PER_GROUP
    row = lax.broadcasted_iota(jnp.int32, (eg, tm), 0)
    neg = jnp.float32(-1e30)

    def first_argmax(vals):
        vmax = jnp.max(vals, axis=0, keepdims=True)
        idx = jnp.min(jnp.where(vals == vmax, row, eg), axis=0, keepdims=True)
        return vmax, idx

    def pick_group(tiles, gidx):
        out = tiles[0]
        for g in range(1, N_GROUPS):
            out = jnp.where(gidx == g, tiles[g], out)
        return out

    is_grp = row < N_GROUPS
    gl = jnp.where(is_grp, lgt[0:eg], neg)
    gmax, gidx = first_argmax(gl)
    p_grp = 1.0 / jnp.sum(jnp.where(is_grp, jnp.exp(gl - gmax), 0.0), axis=0, keepdims=True)
    el = pick_group([lgt[eg * (g + 1):eg * (g + 2)] for g in range(N_GROUPS)], gidx)
    v1, i1 = first_argmax(el)
    sel1 = row == i1
    v2, i2 = first_argmax(jnp.where(sel1, neg, el))
    sel2 = row == i2
    dlt = jnp.exp(v2 - v1)
    g1 = p_grp / (1.0 + dlt)
    g2 = g1 * dlt

    pair = jnp.where(sel1 | sel2, 1.0, 0.0)
    chosen = jnp.concatenate([jnp.where(gidx == g, pair, 0.0) for g in range(N_GROUPS)], axis=0)
    carry = carry_ref[:, 0:1]
    before = _dot(chosen.astype(BF16), tri_ref[...]) + carry
    bg = pick_group([before[eg * g:eg * (g + 1)] for g in range(N_GROUPS)], gidx)
    r1 = jnp.sum(jnp.where(sel1, bg, 0.0), axis=0, keepdims=True)
    r2 = jnp.sum(jnp.where(sel2, bg, 0.0), axis=0, keepdims=True)
    carry = carry + jnp.sum(chosen, axis=1, keepdims=True)
    carry_ref[...] = jnp.broadcast_to(carry, carry_ref.shape)
    cnt_ref[...] = jnp.broadcast_to(carry, cnt_ref.shape)

    e1 = (gidx * eg + i1).astype(F32)
    e2 = (gidx * eg + i2).astype(F32)
    rec = jnp.zeros((eg, tm), F32)
    for ln, val in ((R_E1, e1), (R_E2, e2), (R_RANK1, r1), (R_RANK2, r2), (R_G1, g1), (R_G2, g2)):
        rec = jnp.where(row == ln, val, rec)
    route_ref[...] = jnp.concatenate([rec, jnp.zeros((route_ref.shape[0] - eg, tm), F32)], axis=0)


def _outproj(o_sb, o_hg, h, w_out, g_ffn, w_r, b_r, tm):
    t, d = h.shape
    width = o_sb.shape[1]
    row = lambda i: (i, 0)
    const = lambda i: (0, 0)
    tri = jnp.asarray(np.triu(np.ones((tm, tm), np.float32), 1), dtype=BF16)
    kern = functools.partial(_outproj_kernel, width=width)
    return pl.pallas_call(
        kern,
        grid=(t // tm,),
        in_specs=[pl.BlockSpec((tm, width), row), pl.BlockSpec((tm, width), row),
                  pl.BlockSpec((tm, d), row), pl.BlockSpec((2 * width, d), const),
                  pl.BlockSpec((1, d), const), pl.BlockSpec((LANES, d), const),
                  pl.BlockSpec((LANES, 1), const), pl.BlockSpec((tm, tm), const)],
        out_specs=[pl.BlockSpec((tm, d), row), pl.BlockSpec((tm * (d // LANES), LANES), row),
                   pl.BlockSpec((LANES, tm), lambda i: (0, i)), pl.BlockSpec((N_EXPERTS, LANES), const)],
        out_shape=[jax.ShapeDtypeStruct((t, d), F32), jax.ShapeDtypeStruct((t * (d // LANES), LANES), F32),
                   jax.ShapeDtypeStruct((LANES, t), F32), jax.ShapeDtypeStruct((N_EXPERTS, LANES), F32)],
        scratch_shapes=[pltpu.VMEM((N_EXPERTS, LANES), F32)],
        compiler_params=pltpu.CompilerParams(
            dimension_semantics=("arbitrary",), vmem_limit_bytes=40 * 1024 * 1024),
        name="outproj",
    )(o_sb, o_hg, h, w_out, g_ffn, w_r, b_r, tri)


def _row_copy(src, dst, sem):
    return pltpu.make_async_copy(src, dst, sem)


def _dispatch_kernel(d1_ref, d2_ref, zs_ref, zn_ref, m_hbm, xs_hbm, zero_ref, stage_ref, sems, lsems, zsem,
                     *, tile, n_tiles, ch):
    i = pl.program_id(0)
    slot = i % 3
    lines = tile * ch

    def tile_wait(s):
        _row_copy(m_hbm.at[pl.ds(0, 2 * lines), :], xs_hbm.at[pl.ds(0, 2 * lines), :], sems.at[s]).wait()

    def tile_load(step, s):
        rows = pl.ds(pl.multiple_of(step * lines, lines), lines)
        return _row_copy(m_hbm.at[rows, :], stage_ref.at[s], lsems.at[s])

    @pl.when(i == 0)
    def _():
        tile_load(0, 0).start()
        zero_ref[...] = jnp.zeros_like(zero_ref)

        def zero_block(j):
            rows = pl.ds(pl.multiple_of(zs_ref[j], MOE_BM * ch), MOE_BM * ch)
            return _row_copy(zero_ref, xs_hbm.at[rows, :], zsem)

        for j in range(2 * N_EXPERTS):
            @pl.when(zn_ref[j] > 0)
            def _():
                zero_block(j).start()
        for j in range(2 * N_EXPERTS):
            @pl.when(zn_ref[j] > 0)
            def _():
                zero_block(j).wait()

    nxt = (i + 1) % 3

    @pl.when(i >= 2)
    def _():
        tile_wait(nxt)

    @pl.when(i + 1 < n_tiles)
    def _():
        tile_load(i + 1, nxt).start()

    tile_load(i, slot).wait()
    base = i * tile
    for r in range(tile):
        src = stage_ref.at[slot, pl.ds(r * ch, ch), :]
        for prio, d_ref in enumerate((d1_ref, d2_ref)):
            dst = xs_hbm.at[pl.ds(pl.multiple_of(d_ref[base + r], ch), ch), :]
            _row_copy(src, dst, sems.at[slot]).start(priority=prio)

    @pl.when(i == n_tiles - 1)
    def _():
        if n_tiles > 1:
            tile_wait((i + 2) % 3)
        tile_wait(slot)


def _dispatch(d1, d2, zero_start, zero_n, m_tiles, n_tok, n_slots, tile=BLOCK):
    ch = m_tiles.shape[0] // n_tok
    n_tiles = n_tok // tile
    kern = functools.partial(_dispatch_kernel, tile=tile, n_tiles=n_tiles, ch=ch)
    grid_spec = pltpu.PrefetchScalarGridSpec(
        num_scalar_prefetch=4,
        grid=(n_tiles,),
        in_specs=[pl.BlockSpec(memory_space=pl.ANY)],
        out_specs=pl.BlockSpec(memory_space=pl.ANY),
        scratch_shapes=[pltpu.VMEM((MOE_BM * ch, LANES), F32), pltpu.VMEM((3, tile * ch, LANES), F32),
                        pltpu.SemaphoreType.DMA((3,)), pltpu.SemaphoreType.DMA((3,)),
                        pltpu.SemaphoreType.DMA(())],
    )
    return pl.pallas_call(
        kern,
        grid_spec=grid_spec,
        out_shape=jax.ShapeDtypeStruct((n_slots * ch, LANES), F32),
        compiler_params=pltpu.CompilerParams(dimension_semantics=("arbitrary",)),
        name="dispatch",
    )(d1, d2, zero_start, zero_n, m_tiles)


def _expert_kernel(be_ref, nu_ref, xs_ref, wg_ref, wu_ref, wd_ref, y_ref, *, ch):
    i = pl.program_id(0)

    @pl.when(i < nu_ref[0])
    def _():
        xs = _tiles_load(xs_ref, MOE_BM, ch)
        gt = _dot(xs, wg_ref[...])
        hb = gt * jax.nn.sigmoid(gt) * _dot(xs, wu_ref[...])
        _tiles_store(y_ref, _dot(hb, wd_ref[...]), ch)

    @pl.when(i >= nu_ref[0])
    def _():
        y_ref[...] = jnp.zeros_like(y_ref)


def _experts(block_e, n_used, xs_tiles, n_slots, w_gate, w_up, w_down):
    d, ff = w_gate.shape[-2:]
    ch = d // LANES
    last = lambda i, be, nu: (jnp.minimum(i, nu[0] - 1), 0)
    grid_spec = pltpu.PrefetchScalarGridSpec(
        num_scalar_prefetch=2,
        grid=(n_slots // MOE_BM,),
        in_specs=[pl.BlockSpec((MOE_BM * ch, LANES), last),
                  pl.BlockSpec((None, d, ff), lambda i, be, nu: (be[i], 0, 0)),
                  pl.BlockSpec((None, d, ff), lambda i, be, nu: (be[i], 0, 0)),
                  pl.BlockSpec((None, ff, d), lambda i, be, nu: (be[i], 0, 0))],
        out_specs=pl.BlockSpec((MOE_BM * ch, LANES), lambda i, be, nu: (i, 0)),
    )
    return pl.pallas_call(
        functools.partial(_expert_kernel, ch=ch),
        grid_spec=grid_spec,
        out_shape=jax.ShapeDtypeStruct((n_slots * ch, LANES), F32),
        compiler_params=pltpu.CompilerParams(
            dimension_semantics=("arbitrary",), vmem_limit_bytes=40 * 1024 * 1024),
        name="experts",
    )(block_e, n_used, xs_tiles, w_gate, w_up, w_down)


def _combine_kernel(d1_ref, d2_ref, h2_ref, route_ref, ys_hbm, g_ref, o_ref, ya_ref, yb_ref, sems,
                    *, tile, n_tiles, ch):
    i = pl.program_id(0)
    slot = i % 2
    lines = tile * ch

    def issue(step, s):
        base = step * tile
        for r in range(tile):
            for prio, (d_ref, y_ref) in enumerate(((d1_ref, ya_ref), (d2_ref, yb_ref))):
                src = ys_hbm.at[pl.ds(pl.multiple_of(d_ref[base + r], ch), ch), :]
                _row_copy(src, y_ref.at[s, pl.ds(r * ch, ch), :], sems.at[s]).start(priority=prio)

    @pl.when(i == 0)
    def _():
        issue(0, 0)

    @pl.when(i + 1 < n_tiles)
    def _():
        issue(i + 1, 1 - slot)

    _row_copy(ys_hbm.at[pl.ds(0, lines), :], ya_ref.at[slot], sems.at[slot]).wait()
    _row_copy(ys_hbm.at[pl.ds(0, lines), :], yb_ref.at[slot], sems.at[slot]).wait()
    rec = route_ref[...].T
    g1 = rec[:, R_G1:R_G1 + 1]
    g2 = rec[:, R_G2:R_G2 + 1]
    h = h2_ref[...] + (g1 * _tiles_load(ya_ref.at[slot], tile, ch) + g2 * _tiles_load(yb_ref.at[slot], tile, ch))
    o_ref[...] = _rms(h) * g_ref[...]


def _combine(d1, d2, h2, route, ys, g_final, tile=BLOCK):
    t, d = h2.shape
    ch = d // LANES
    n_tiles = t // tile
    kern = functools.partial(_combine_kernel, tile=tile, n_tiles=n_tiles, ch=ch)
    grid_spec = pltpu.PrefetchScalarGridSpec(
        num_scalar_prefetch=2,
        grid=(n_tiles,),
        in_specs=[pl.BlockSpec((tile, d), lambda i, a, b: (i, 0)),
                  pl.BlockSpec((LANES, tile), lambda i, a, b: (0, i)),
                  pl.BlockSpec(memory_space=pl.ANY),
                  pl.BlockSpec((1, d), lambda i, a, b: (0, 0))],
        out_specs=pl.BlockSpec((tile, d), lambda i, a, b: (i, 0)),
        scratch_shapes=[pltpu.VMEM((2, tile * ch, LANES), F32), pltpu.VMEM((2, tile * ch, LANES), F32),
                        pltpu.SemaphoreType.DMA((2,))],
    )
    return pl.pallas_call(
        kern,
        grid_spec=grid_spec,
        out_shape=jax.ShapeDtypeStruct((t, d), F32),
        compiler_params=pltpu.CompilerParams(dimension_semantics=("arbitrary",)),
        name="combine",
    )(d1, d2, h2, route, ys, g_final)


def _segment_layout(route, counts_row, n_tok, ch):
    counts = counts_row.astype(jnp.int32)
    n_slots = n_tok * TOP_K + N_EXPERTS * MOE_BM
    n_blocks = n_slots // MOE_BM
    padded = (counts + MOE_BM - 1) // MOE_BM * MOE_BM
    padded_end = jnp.cumsum(padded)
    padded_start = padded_end - padded
    e1 = route[R_E1].astype(jnp.int32)
    e2 = route[R_E2].astype(jnp.int32)
    onehot = jnp.arange(N_EXPERTS, dtype=jnp.int32)[None, :]
    start1 = jnp.sum(jnp.where(e1[:, None] == onehot, padded_start[None, :], 0), axis=1)
    start2 = jnp.sum(jnp.where(e2[:, None] == onehot, padded_start[None, :], 0), axis=1)
    d1 = (start1 + route[R_RANK1].astype(jnp.int32)) * ch
    d2 = (start2 + route[R_RANK2].astype(jnp.int32)) * ch
    block_e = jnp.sum((jnp.arange(n_blocks, dtype=jnp.int32)[:, None] * MOE_BM >= padded_end[None, :])
                      .astype(jnp.int32), axis=1)
    block_e = jnp.minimum(block_e, N_EXPERTS - 1)
    n_used = (padded_end[-1] // MOE_BM).reshape(1)
    trailing = n_used[0] + jnp.arange(N_EXPERTS, dtype=jnp.int32)
    zero_start = jnp.concatenate([jnp.maximum(padded_end - MOE_BM, 0),
                                  jnp.minimum(trailing, n_blocks - 1) * MOE_BM]) * ch
    zero_flag = jnp.concatenate([counts, (trailing < n_blocks).astype(jnp.int32)])
    return d1, d2, block_e, n_used, zero_start, zero_flag, n_slots


def kernel(x, meta_tokens, lb_logits, g_mix, w_in, sb_gain, hg_gain, w_out, g_ffn, w_router_group,
           b_router_group, w_router_expert, b_router_expert, w_expert_gate, w_expert_up, w_expert_down,
           g_final):
    b, seq, d = x.shape
    depth = w_in.shape[0]
    assert depth == 1, "single-layer block"
    assert seq % (Q_SUB * BLOCK) == 0
    t = b * seq
    tm = 512
    layer = 0

    xr = x.reshape(t, d)
    prefix = jnp.concatenate([jnp.zeros((LEAD_PAD, d), x.dtype), meta_tokens.astype(x.dtype)], axis=0)
    prefix_mask = (jnp.arange(BLOCK) >= LEAD_PAD).astype(F32)[:, None]
    lower_bounds = jnp.cumsum(jax.nn.softmax(lb_logits.astype(F32), axis=0), axis=0)
    lb = lower_bounds[layer][None, :]
    g_mix_l = g_mix[layer][None, :]
    w_in_l = w_in[layer].astype(BF16)

    q, k0, k1, v0, v1, hq, hk, hv, lf, hg = _proj(xr, jnp.ones((t, 1), F32), g_mix_l, w_in_l, lb, tm)
    _, mk0, mk1, mv0, mv1, _, mhk, mhv, mlf, _ = _proj(prefix, prefix_mask, g_mix_l, w_in_l, lb, BLOCK)

    o_sb = _sb_attention(q, (k0, k1, v0, v1), (mk0, mk1, mv0, mv1), sb_gain[layer][None, :], b, seq)
    o_hg = _hgrn2(hq, hk, hv, lf, hg, (mhk, mhv, mlf), hg_gain[layer][None, :], b, seq)

    eo = EXPERTS_PER_GROUP
    w_r = jnp.zeros((LANES, d), F32)
    w_r = w_r.at[:N_GROUPS].set(w_router_group[layer].T)
    w_r = w_r.at[eo:eo + N_EXPERTS].set(w_router_expert[layer].T)
    b_r = jnp.zeros((LANES, 1), F32)
    b_r = b_r.at[:N_GROUPS, 0].set(b_router_group[layer])
    b_r = b_r.at[eo:eo + N_EXPERTS, 0].set(b_router_expert[layer])
    h2, m_tiles, route, counts = _outproj(o_sb, o_hg, xr, w_out[layer].astype(BF16), g_ffn[layer][None, :],
                                          w_r, b_r, tm)

    d1, d2, block_e, n_used, zero_start, zero_n, n_slots = _segment_layout(route, counts[:, 0], t, d // LANES)
    xs_tiles = _dispatch(d1, d2, zero_start, zero_n, m_tiles, t, n_slots)
    ys_tiles = _experts(block_e, n_used, xs_tiles, n_slots,
                        w_expert_gate[layer], w_expert_up[layer], w_expert_down[layer])
    out = _combine(d1, d2, h2, route, ys_tiles, g_final[None, :])
    return out.reshape(b, seq, d)
```
